```python
import math
import jax
import jax.numpy as jnp
from jax import lax
import numpy as np


D_MODEL = 1024
BATCH = 16
SEQ = 2048
DEPTH = 2

CONV_WIDTH = 4
NORM_EPS = 1e-6
RWKV_HEADS = 4
D_A = D_MODEL // 4
RWKV_HEAD_DIM = D_A // RWKV_HEADS
RWKV_DECAY_LORA = 32
RWKV_AAA_LORA = 32
RWKV_GATE_LORA = 64
RWKV_GN_EPS = 64e-5
A_COLS = 3 * D_A + RWKV_DECAY_LORA + RWKV_AAA_LORA + RWKV_GATE_LORA
GDN_HEADS = 4
D_B = D_MODEL // 2
GDN_HEAD_DIM = D_B // GDN_HEADS
GDN_CHUNK = 64
B_COLS = 4 * D_B + 2 * GDN_HEADS
D_C = D_MODEL - D_A - D_B
LRU_BLOCKS = 4
LRU_BLOCK_DIM = D_C // LRU_BLOCKS
LRU_C = 8.0
C_COLS = 2 * D_C
D_MIX = D_A + D_B + D_C
N_IN = A_COLS + B_COLS + C_COLS
MOE_GROUPS = 4
MOE_EXPERTS_PER_GROUP = 8
MOE_EXPERTS = MOE_GROUPS * MOE_EXPERTS_PER_GROUP
MOE_TOP_K = 2
D_EXPERT = D_MODEL // 4
MOE_BLOCK = 128

kernel_name = 'hybrid_rwkv7_gdn_rglru_hmoe'


def rms_norm(x, gain):
    xf = x.astype(jnp.float32)
    y = xf * lax.rsqrt(jnp.mean(xf * xf, axis=-1, keepdims=True) + NORM_EPS)
    return (y * gain.astype(jnp.float32)).astype(x.dtype)


def l2_normalize(x):
    return x * lax.rsqrt(jnp.sum(x * x, axis=-1, keepdims=True) + 1e-6)


def causal_depthwise_conv(x, w):
    return lax.conv_general_dilated(
        x, w[:, None, :].astype(x.dtype), window_strides=(1,),
        padding=[(CONV_WIDTH - 1, 0)], dimension_numbers=('NWC', 'WIO', 'NWC'),
        feature_group_count=x.shape[-1])


def rwkv7_scan(r, w, k, v, kk, b):
    bsz, T, H, N = r.shape

    def step(state, inp):
        r_t, w_t, k_t, v_t, kk_t, b_t = inp
        sa = jnp.einsum('bhvk,bhk->bhv', state, kk_t)
        state = (state * w_t[:, :, None, :] - sa[..., None] * b_t[:, :, None, :]
                 + v_t[..., None] * k_t[:, :, None, :])
        return state, jnp.einsum('bhvk,bhk->bhv', state, r_t)

    xs = tuple(jnp.swapaxes(t, 0, 1) for t in (r, w, k, v, kk, b))
    state0 = jnp.zeros((bsz, H, N, N), r.dtype)
    _, o = lax.scan(step, state0, xs)
    return jnp.swapaxes(o, 0, 1)


def rwkv7_group(p, mu, w0, w2, a0, a2, g2, k_k, k_a, r_k, lnx_w, lnx_b):
    bsz, T, _ = p.shape
    H, N = RWKV_HEADS, RWKV_HEAD_DIM
    p_prev = jnp.pad(p, ((0, 0), (1, 0), (0, 0)))[:, :-1]
    xs = p + (p_prev - p) * mu
    r, k, v, w_lo, a_lo, g_lo = jnp.split(
        xs, [D_A, 2 * D_A, 3 * D_A, 3 * D_A + RWKV_DECAY_LORA,
             3 * D_A + RWKV_DECAY_LORA + RWKV_AAA_LORA], axis=-1)
    w_log = -jax.nn.softplus(-(w0 + jnp.tanh(w_lo) @ w2)) - 0.5
    decay = jnp.exp(-jnp.exp(w_log))
    a = jax.nn.sigmoid(a0 + a_lo @ a2)
    g = jax.nn.sigmoid(g_lo) @ g2
    kk = l2_normalize((k * k_k).reshape(bsz, T, H, N))
    k = k * (1.0 + (a - 1.0) * k_a)
    r_h, k_h, v_h, a_h, w_h = (t.reshape(bsz, T, H, N) for t in (r, k, v, a, decay))
    o = rwkv7_scan(r_h, w_h, k_h, v_h, kk, kk * a_h)
    mean = jnp.mean(o, axis=-1, keepdims=True)
    var = jnp.mean(jnp.square(o - mean), axis=-1, keepdims=True)
    o = ((o - mean) * lax.rsqrt(var + RWKV_GN_EPS)).reshape(bsz, T, D_A) * lnx_w + lnx_b
    bonus = jnp.sum(r_h * k_h * r_k, axis=-1, keepdims=True) * v_h
    return (o + bonus.reshape(bsz, T, D_A)) * g


def gated_delta_rule_chunked(q, k, v, g, beta):
    bsz, T, H, DK = q.shape
    DV = v.shape[-1]
    C = GDN_CHUNK
    n = T // C

    def to_chunks(t):
        return jnp.moveaxis(t.reshape((bsz, n, C, H) + t.shape[3:]), 3, 1)

    q = to_chunks(q * DK ** -0.5)
    k = to_chunks(k)
    v = to_chunks(v)
    beta = to_chunks(beta)
    g = jnp.cumsum(to_chunks(g), axis=-1)
    causal = jnp.tril(jnp.ones((C, C), dtype=bool))
    strict = jnp.tril(jnp.ones((C, C), dtype=bool), -1)
    decay = jnp.exp(jnp.where(causal, g[..., :, None] - g[..., None, :], -jnp.inf))
    kb = k * beta[..., None]
    lower = jnp.where(strict, jnp.einsum('bhncd,bhnsd->bhncs', kb, k) * decay, 0.0)
    eye = jnp.eye(C, dtype=lower.dtype)
    rhs = jnp.concatenate([v * beta[..., None], kb * jnp.exp(g)[..., None]], axis=-1)
    sol = lax.linalg.triangular_solve(eye + lower, rhs, left_side=True, lower=True,
                                      unit_diagonal=True)
    u, w = sol[..., :DV], sol[..., DV:]
    intra = jnp.where(causal, jnp.einsum('bhncd,bhnsd->bhncs', q, k) * decay, 0.0)

    def step(state, inp):
        q_c, k_c, u_c, w_c, g_c, a_c = inp
        v_new = u_c - jnp.einsum('bhcd,bhdv->bhcv', w_c, state)
        o_c = (jnp.einsum('bhcd,bhdv->bhcv', q_c * jnp.exp(g_c)[..., None], state)
               + jnp.einsum('bhcs,bhsv->bhcv', a_c, v_new))
        g_last = g_c[..., -1:]
        state = (state * jnp.exp(g_last)[..., None]
                 + jnp.einsum('bhcd,bhcv->bhdv', k_c * jnp.exp(g_last - g_c)[..., None], v_new))
        return state, o_c

    xs = tuple(jnp.moveaxis(t, 2, 0) for t in (q, k, u, w, g, intra))
    state0 = jnp.zeros((bsz, H, DK, DV), q.dtype)
    _, o = lax.scan(step, state0, xs)
    return o.transpose(1, 0, 3, 2, 4).reshape(bsz, T, H, DV)


def gdn_group(p, conv_w, a_log, dt_bias, norm_w):
    bsz, T, _ = p.shape
    H, Dh = GDN_HEADS, GDN_HEAD_DIM
    qkv, z, beta_lo, a_lo = jnp.split(p, [3 * D_B, 4 * D_B, 4 * D_B + H], axis=-1)
    qkv = jax.nn.silu(causal_depthwise_conv(qkv, conv_w))
    q, k, v = (t.reshape(bsz, T, H, Dh) for t in jnp.split(qkv, 3, axis=-1))
    q = l2_normalize(q)
    k = l2_normalize(k)
    beta = jax.nn.sigmoid(beta_lo)
    g = -jnp.exp(a_log) * jax.nn.softplus(a_lo + dt_bias)
    o = gated_delta_rule_chunked(q, k, v, g, beta)
    o = o * lax.rsqrt(jnp.mean(o * o, axis=-1, keepdims=True) + NORM_EPS) * norm_w
    o = o * jax.nn.silu(z.reshape(bsz, T, H, Dh))
    return o.reshape(bsz, T, D_B)


def _linear_combine(c1, c2):
    a1, b1 = c1
    a2, b2 = c2
    return a1 * a2, a2 * b1 + b2


def rglru_group(p, conv_w, conv_b, w_a, b_a, w_x, b_x, lam):
    bsz, T, _ = p.shape
    xb, gate = jnp.split(p, [D_C], axis=-1)
    xb = causal_depthwise_conv(xb, conv_w) + conv_b
    blocks = xb.reshape(bsz, T, LRU_BLOCKS, LRU_BLOCK_DIM)
    r = jax.nn.sigmoid(jnp.einsum('btnc,ncd->btnd', blocks, w_a.astype(blocks.dtype)).reshape(bsz, T, D_C) + b_a)
    i = jax.nn.sigmoid(jnp.einsum('btnc,ncd->btnd', blocks, w_x.astype(blocks.dtype)).reshape(bsz, T, D_C) + b_x)
    log_a = -LRU_C * r * jax.nn.softplus(-lam)
    a = jnp.exp(log_a)
    mult = jnp.sqrt(-jnp.expm1(2.0 * log_a))
    mult = jnp.where(jnp.arange(T)[None, :, None] == 0, 1.0, mult)
    _, h = lax.associative_scan(_linear_combine, (a, xb * i * mult), axis=1)
    return h * jax.nn.gelu(gate, approximate=True)


def hybrid_mixer(h, w_in, w_out, rwkv_mu, rwkv_w0, rwkv_w2, rwkv_a0, rwkv_a2, rwkv_g2,
                 rwkv_k_k, rwkv_k_a, rwkv_r_k, rwkv_lnx_w, rwkv_lnx_b,
                 gdn_conv_w, gdn_a_log, gdn_dt_bias, gdn_norm_w,
                 lru_conv_w, lru_conv_b, lru_w_a, lru_b_a, lru_w_x, lru_b_x, lru_lambda):
    proj = jnp.matmul(h, w_in).astype(jnp.float32)
    p_a, p_b, p_c = jnp.split(proj, [A_COLS, A_COLS + B_COLS], axis=-1)
    o_a = rwkv7_group(p_a, rwkv_mu, rwkv_w0, rwkv_w2, rwkv_a0, rwkv_a2, rwkv_g2,
                      rwkv_k_k, rwkv_k_a, rwkv_r_k, rwkv_lnx_w, rwkv_lnx_b)
    o_b = gdn_group(p_b, gdn_conv_w, gdn_a_log, gdn_dt_bias, gdn_norm_w)
    o_c = rglru_group(p_c, lru_conv_w, lru_conv_b, lru_w_a, lru_b_a, lru_w_x, lru_b_x, lru_lambda)
    mix = jnp.concatenate([o_a, o_b, o_c], axis=-1).astype(h.dtype)
    return jnp.matmul(mix, w_out)


def hierarchical_moe(h, w_group, b_group, w_expert, b_expert, w_gate, w_up, w_down):
    bsz, T, D = h.shape
    n_tok = bsz * T
    xt = h.reshape(n_tok, D)
    grp_logits = jnp.matmul(xt, w_group).astype(jnp.float32) + b_group.astype(jnp.float32)
    grp_idx = jnp.argmax(grp_logits, axis=-1)
    grp_w = jnp.take_along_axis(jax.nn.softmax(grp_logits, axis=-1), grp_idx[:, None], axis=-1)[:, 0]
    exp_logits = (jnp.matmul(xt, w_expert).astype(jnp.float32) + b_expert.astype(jnp.float32)
                  ).reshape(n_tok, MOE_GROUPS, MOE_EXPERTS_PER_GROUP)
    exp_logits = jnp.take_along_axis(exp_logits, grp_idx[:, None, None], axis=1)[:, 0]
    top_val, top_loc = lax.top_k(exp_logits, MOE_TOP_K)
    gate = jax.nn.softmax(top_val, axis=-1) * grp_w[:, None]
    eid = (grp_idx[:, None] * MOE_EXPERTS_PER_GROUP + top_loc).reshape(-1).astype(jnp.int32)
    n_assign = n_tok * MOE_TOP_K
    tok = jnp.arange(n_assign, dtype=jnp.int32) // MOE_TOP_K
    order = jnp.argsort(eid)
    eid_s, tok_s, gate_s = eid[order], tok[order], gate.reshape(-1)[order]
    counts = jnp.bincount(eid, length=MOE_EXPERTS)
    padded = (counts + MOE_BLOCK - 1) // MOE_BLOCK * MOE_BLOCK
    pad_end = jnp.cumsum(padded)
    pad_start = pad_end - padded
    start = jnp.cumsum(counts) - counts
    dest = pad_start[eid_s] + jnp.arange(n_assign, dtype=jnp.int32) - start[eid_s]
    n_blocks = (n_assign + MOE_EXPERTS * (MOE_BLOCK - 1) + MOE_BLOCK - 1) // MOE_BLOCK
    buf_len = n_blocks * MOE_BLOCK
    buf_tok = jnp.full((buf_len,), n_tok, dtype=jnp.int32).at[dest].set(tok_s)
    blk_expert = jnp.minimum(
        jnp.searchsorted(pad_end, jnp.arange(n_blocks) * MOE_BLOCK, side='right'), MOE_EXPERTS - 1)
    x_pad = jnp.concatenate([xt, jnp.zeros((1, D), xt.dtype)], axis=0)
    xb = x_pad[buf_tok].reshape(n_blocks, MOE_BLOCK, D)

    def expert_block(args):
        x_blk, e = args
        hid = jax.nn.silu(jnp.matmul(x_blk, w_gate[e])) * jnp.matmul(x_blk, w_up[e])
        return jnp.matmul(hid, w_down[e])

    yb = lax.map(expert_block, (xb, blk_expert)).reshape(buf_len, D)
    contrib = (yb[dest].astype(jnp.float32) * gate_s[:, None]).astype(h.dtype)
    y = jnp.zeros((n_tok, D), h.dtype).at[tok_s].add(contrib)
    return y.reshape(bsz, T, D)


def setup_inputs(seed: int = 0) -> dict:
    key = jax.random.key(seed)
    keys = iter(jax.random.split(key, 35))
    f32 = jnp.float32

    def nrm(shape, scale):
        return jax.random.normal(next(keys), shape, f32) * scale

    def uni(shape, lo, hi):
        return jax.random.uniform(next(keys), shape, f32, lo, hi)

    L = DEPTH
    x = nrm((BATCH, SEQ, D_MODEL), 1.0)
    norm_mix = 1.0 + nrm((L, D_MODEL), 0.02)
    norm_ffn = 1.0 + nrm((L, D_MODEL), 0.02)
    norm_final = 1.0 + nrm((D_MODEL,), 0.02)
    w_in = nrm((L, D_MODEL, N_IN), D_MODEL ** -0.5)
    w_out = nrm((L, D_MIX, D_MODEL), D_MIX ** -0.5)
    rwkv_mu = uni((L, A_COLS), 0.0, 1.0)
    rwkv_w0 = uni((L, D_A), -6.0, -1.0)
    rwkv_w2 = nrm((L, RWKV_DECAY_LORA, D_A), 0.5 * RWKV_DECAY_LORA ** -0.5)
    rwkv_a0 = nrm((L, D_A), 0.1)
    rwkv_a2 = nrm((L, RWKV_AAA_LORA, D_A), RWKV_AAA_LORA ** -0.5)
    rwkv_g2 = nrm((L, RWKV_GATE_LORA, D_A), RWKV_GATE_LORA ** -0.5)
    rwkv_k_k = 0.85 + nrm((L, D_A), 0.02)
    rwkv_k_a = 1.0 + nrm((L, D_A), 0.02)
    rwkv_r_k = nrm((L, RWKV_HEADS, RWKV_HEAD_DIM), 0.1)
    rwkv_lnx_w = 1.0 + nrm((L, D_A), 0.02)
    rwkv_lnx_b = nrm((L, D_A), 0.02)
    gdn_conv_w = nrm((L, CONV_WIDTH, 3 * D_B), CONV_WIDTH ** -0.5)
    gdn_a_log = jnp.log(uni((L, GDN_HEADS), 1.0, 16.0))
    dt = jnp.exp(uni((L, GDN_HEADS), math.log(1e-3), math.log(1e-1)))
    gdn_dt_bias = dt + jnp.log(-jnp.expm1(-dt))
    gdn_norm_w = 1.0 + nrm((L, GDN_HEAD_DIM), 0.02)
    lru_conv_w = nrm((L, CONV_WIDTH, D_C), CONV_WIDTH ** -0.5)
    lru_conv_b = nrm((L, D_C), 0.02)
    lru_w_a = nrm((L, LRU_BLOCKS, LRU_BLOCK_DIM, LRU_BLOCK_DIM), LRU_BLOCK_DIM ** -0.5)
    lru_b_a = nrm((L, D_C), 0.02)
    lru_w_x = nrm((L, LRU_BLOCKS, LRU_BLOCK_DIM, LRU_BLOCK_DIM), LRU_BLOCK_DIM ** -0.5)
    lru_b_x = nrm((L, D_C), 0.02)
    s = uni((L, D_C), 0.9, 0.999) ** (1.0 / LRU_C)
    lru_lambda = jnp.log(s) - jnp.log1p(-s)
    moe_w_group = nrm((L, D_MODEL, MOE_GROUPS), D_MODEL ** -0.5)
    moe_b_group = nrm((L, MOE_GROUPS), 0.01)
    moe_w_expert = nrm((L, D_MODEL, MOE_EXPERTS), D_MODEL ** -0.5)
    moe_b_expert = nrm((L, MOE_EXPERTS), 0.01)
    moe_w_gate = nrm((L, MOE_EXPERTS, D_MODEL, D_EXPERT), D_MODEL ** -0.5)
    moe_w_up = nrm((L, MOE_EXPERTS, D_MODEL, D_EXPERT), D_MODEL ** -0.5)
    moe_w_down = nrm((L, MOE_EXPERTS, D_EXPERT, D_MODEL), D_EXPERT ** -0.5)
    return {
        'x': x, 'norm_mix': norm_mix, 'norm_ffn': norm_ffn, 'norm_final': norm_final,
        'w_in': w_in, 'w_out': w_out,
        'rwkv_mu': rwkv_mu, 'rwkv_w0': rwkv_w0, 'rwkv_w2': rwkv_w2, 'rwkv_a0': rwkv_a0,
        'rwkv_a2': rwkv_a2, 'rwkv_g2': rwkv_g2, 'rwkv_k_k': rwkv_k_k, 'rwkv_k_a': rwkv_k_a,
        'rwkv_r_k': rwkv_r_k, 'rwkv_lnx_w': rwkv_lnx_w, 'rwkv_lnx_b': rwkv_lnx_b,
        'gdn_conv_w': gdn_conv_w, 'gdn_a_log': gdn_a_log, 'gdn_dt_bias': gdn_dt_bias,
        'gdn_norm_w': gdn_norm_w,
        'lru_conv_w': lru_conv_w, 'lru_conv_b': lru_conv_b, 'lru_w_a': lru_w_a, 'lru_b_a': lru_b_a,
        'lru_w_x': lru_w_x, 'lru_b_x': lru_b_x, 'lru_lambda': lru_lambda,
        'moe_w_group': moe_w_group, 'moe_b_group': moe_b_group, 'moe_w_expert': moe_w_expert,
        'moe_b_expert': moe_b_expert, 'moe_w_gate': moe_w_gate, 'moe_w_up': moe_w_up,
        'moe_w_down': moe_w_down,
    }


def reference(x, norm_mix, norm_ffn, norm_final, w_in, w_out,
              rwkv_mu, rwkv_w0, rwkv_w2, rwkv_a0, rwkv_a2, rwkv_g2, rwkv_k_k, rwkv_k_a,
              rwkv_r_k, rwkv_lnx_w, rwkv_lnx_b,
              gdn_conv_w, gdn_a_log, gdn_dt_bias, gdn_norm_w,
              lru_conv_w, lru_conv_b, lru_w_a, lru_b_a, lru_w_x, lru_b_x, lru_lambda,
              moe_w_group, moe_b_group, moe_w_expert, moe_b_expert, moe_w_gate, moe_w_up,
              moe_w_down):
    for l in range(DEPTH):
        h = rms_norm(x, norm_mix[l])
        x = x + hybrid_mixer(h, w_in[l], w_out[l],
                             rwkv_mu[l], rwkv_w0[l], rwkv_w2[l], rwkv_a0[l], rwkv_a2[l],
                             rwkv_g2[l], rwkv_k_k[l], rwkv_k_a[l], rwkv_r_k[l],
                             rwkv_lnx_w[l], rwkv_lnx_b[l],
                             gdn_conv_w[l], gdn_a_log[l], gdn_dt_bias[l], gdn_norm_w[l],
                             lru_conv_w[l], lru_conv_b[l], lru_w_a[l], lru_b_a[l],
                             lru_w_x[l], lru_b_x[l], lru_lambda[l])
        h = rms_norm(x, norm_ffn[l])
        x = x + hierarchical_moe(h, moe_w_group[l], moe_b_group[l], moe_w_expert[l],
                                 moe_b_expert[l], moe_w_gate[l], moe_w_up[l], moe_w_down[l])
    return rms_norm(x, norm_final)
```

```python
import functools

import jax
import jax.numpy as jnp
from jax import lax
from jax.experimental import pallas as pl
from jax.experimental.pallas import tpu as pltpu

F32 = jnp.float32
BF16 = jnp.bfloat16
HIGHEST = lax.Precision.HIGHEST

NORM_EPS = 1e-6
CONV_WIDTH = 4
CHUNK = 64
RWKV_HEADS = 4
RWKV_GN_EPS = 64e-5
GDN_HEADS = 4
LRU_C = 8.0
MOE_GROUPS = 4
MOE_PER_GROUP = 8
MOE_EXPERTS = MOE_GROUPS * MOE_PER_GROUP
MOE_BLOCK = 128
LANES = 128
NEG_BIG = -1e30
VMEM_LIMIT = 56 * 1024 * 1024


def _cparams(*sem):
    return pltpu.CompilerParams(dimension_semantics=sem, vmem_limit_bytes=VMEM_LIMIT)


def _bdot(a, b):
    return jnp.dot(a.astype(BF16), b.astype(BF16), preferred_element_type=F32)


def _bdot_nt(a, b):
    return lax.dot_general(a.astype(BF16), b.astype(BF16), (((1,), (1,)), ((), ())),
                           preferred_element_type=F32)


def _bdot_tn(a, b):
    return lax.dot_general(a.astype(BF16), b.astype(BF16), (((0,), (0,)), ((), ())),
                           preferred_element_type=F32)


def _exact_dot(a, b01):
    hi = a.astype(BF16)
    lo = (a - hi.astype(F32)).astype(BF16)
    return (jnp.dot(hi, b01, preferred_element_type=F32)
            + jnp.dot(lo, b01, preferred_element_type=F32))


def _softplus(x):
    return jnp.maximum(x, 0.0) + jnp.log1p(jnp.exp(-jnp.abs(x)))


def _sigmoid(x):
    return 1.0 / (1.0 + jnp.exp(-x))


def _silu(x):
    return x * _sigmoid(x)


def _iota(shape, dim):
    return lax.broadcasted_iota(jnp.int32, shape, dim)


def _head_ones(width, head_dim):
    r = _iota((width, width), 0) // head_dim
    c = _iota((width, width), 1) // head_dim
    return jnp.where(r == c, 1.0, 0.0).astype(BF16)


def _stack_heads(x, n_heads):
    c, w = x.shape
    hd = w // n_heads
    lane_head = _iota((c, w), 1) // hd
    return jnp.concatenate([jnp.where(lane_head == h, x, 0.0) for h in range(n_heads)], axis=0)


def _unstack_heads(xs, n_heads):
    c = xs.shape[0] // n_heads
    out = xs[0:c]
    for h in range(1, n_heads):
        out = out + xs[h * c:(h + 1) * c]
    return out


def _block_tri_masks(n, c):
    r = _iota((n, n), 0)
    q = _iota((n, n), 1)
    same = (r // c) == (q // c)
    return same & (q < r), same & (q <= r)


def _unit_lower_inverse(l_mat, c):
    n = l_mat.shape[0]
    eye = jnp.where(_iota((n, n), 0) == _iota((n, n), 1), 1.0, 0.0).astype(F32)
    p = eye + l_mat
    m = l_mat
    span = 1
    while span * 2 < c:
        m = _bdot(m, m)
        p = p + _bdot(m, p)
        span *= 2
    return p


def _inproj_body(x_ref, g_ref, w_ref, *out_refs):
    x = x_ref[...]
    h = x * lax.rsqrt(jnp.mean(x * x, axis=-1, keepdims=True) + NORM_EPS) * g_ref[...]
    hb = h.astype(BF16)
    off = 0
    for o_ref in out_refs:
        wdt = o_ref.shape[-1]
        o_ref[...] = jnp.dot(hb, w_ref[:, off:off + wdt], preferred_element_type=F32)
        off += wdt


def _in_proj(xt, gain, w_cat, widths, tm=512):
    n, d = xt.shape
    ntot = w_cat.shape[1]
    return pl.pallas_call(
        _inproj_body,
        grid=(n // tm,),
        in_specs=[pl.BlockSpec((tm, d), lambda i: (i, 0)),
                  pl.BlockSpec((1, d), lambda i: (0, 0)),
                  pl.BlockSpec((d, ntot), lambda i: (0, 0))],
        out_specs=[pl.BlockSpec((tm, w), lambda i: (i, 0)) for w in widths],
        out_shape=[jax.ShapeDtypeStruct((n, w), F32) for w in widths],
        compiler_params=_cparams("parallel"),
        name="in_proj",
    )(xt, gain.reshape(1, d), w_cat)


def _rwkv_body(p_ref, mu_ref, w0_ref, a0_ref, kk_ref, ka_ref, rk_ref, lnw_ref, lnb_ref,
               w2_ref, a2_ref, g2_ref, o_ref, pbuf, state):
    c = CHUNK
    da = o_ref.shape[-1]
    nh = RWKV_HEADS
    hd = da // nh

    @pl.when(pl.program_id(1) == 0)
    def _():
        pbuf[0:8, :] = jnp.zeros((8, pbuf.shape[1]), F32)
        state[...] = jnp.zeros_like(state)

    pbuf[8:8 + c, :] = p_ref[...]
    p = pbuf[8:8 + c, :]
    p_prev = pbuf[7:7 + c, :]
    pbuf[0:8, :] = pbuf[c:c + 8, :]
    xs = p + (p_prev - p) * mu_ref[...]
    r = xs[:, 0:da]
    k = xs[:, da:2 * da]
    v = xs[:, 2 * da:3 * da]
    lo = xs[:, 3 * da:]

    w_log = -_softplus(-(w0_ref[...] + _bdot(jnp.tanh(lo), w2_ref[...]))) - 0.5
    lw = -jnp.exp(w_log)
    a = _sigmoid(a0_ref[...] + _bdot(lo, a2_ref[...]))
    g = _bdot(_sigmoid(lo), g2_ref[...])

    ones_h = _head_ones(da, hd)
    kkr = k * kk_ref[...]
    kk = kkr * lax.rsqrt(_exact_dot(kkr * kkr, ones_h) + 1e-6)
    k2 = k * (1.0 + (a - 1.0) * ka_ref[...])
    alpha = -(kk * a)

    tri = jnp.where(_iota((c, c), 1) <= _iota((c, c), 0), 1.0, 0.0).astype(F32)
    cum = jnp.dot(tri, lw, precision=HIGHEST, preferred_element_type=F32)
    cum_last = cum[c - 1:c, :]
    e_pos = jnp.exp(cum)
    e_neg = jnp.exp(-cum)
    e_tail = jnp.exp(cum_last - cum)

    rt = _stack_heads(r * e_pos, nh)
    bt = _stack_heads(kk * jnp.exp(cum - lw), nh)
    at = _stack_heads(alpha * e_neg, nh)
    kt = _stack_heads(k2 * e_neg, nh)
    vs = _stack_heads(v, nh)
    a_end = _stack_heads(alpha * e_tail, nh)
    k_end = _stack_heads(k2 * e_tail, nh)

    n = nh * c
    strict, incl = _block_tri_masks(n, c)
    gram = _bdot_nt(jnp.concatenate([bt, rt], axis=0), jnp.concatenate([at, kt], axis=0))
    l_a = jnp.where(strict, gram[0:n, 0:n], 0.0)
    l_k = jnp.where(strict, gram[0:n, n:], 0.0)
    s_a = jnp.where(incl, gram[n:, 0:n], 0.0)
    s_k = jnp.where(incl, gram[n:, n:], 0.0)

    t_inv = _unit_lower_inverse(l_a, c)
    wu = _bdot(t_inv, jnp.concatenate([bt, _bdot(l_k, vs)], axis=1))
    w_mat = wu[:, 0:da]
    u0 = wu[:, da:]

    s_old = state[...]
    ws_rs = _bdot_nt(jnp.concatenate([w_mat, rt], axis=0), s_old)
    u = u0 + ws_rs[0:n]
    uv = jnp.concatenate([u, vs], axis=0)
    o_st = ws_rs[n:] + _bdot(jnp.concatenate([s_a, s_k], axis=1), uv)
    state[...] = s_old * jnp.exp(cum_last) + _bdot_tn(uv, jnp.concatenate([a_end, k_end], axis=0))
    o = _unstack_heads(o_st, nh)

    inv_hd = 1.0 / hd
    mean = _exact_dot(o, ones_h) * inv_hd
    cen = o - mean
    var = _exact_dot(cen * cen, ones_h) * inv_hd
    o_n = cen * lax.rsqrt(var + RWKV_GN_EPS) * lnw_ref[...] + lnb_ref[...]
    bonus = _exact_dot(r * k2 * rk_ref[...], ones_h) * v
    o_ref[...] = (o_n + bonus) * g


def _rwkv_group(p_a, bsz, seq, mu, w0, w2, a0, a2, g2, k_k, k_a, r_k, lnx_w, lnx_b):
    n, a_cols = p_a.shape
    da = w0.shape[-1]
    n_lora = a_cols - 3 * da
    nchunk = seq // CHUNK
    d_dec, d_aaa = w2.shape[0], a2.shape[0]
    w2p = jnp.zeros((n_lora, da), F32).at[0:d_dec].set(w2).astype(BF16)
    a2p = jnp.zeros((n_lora, da), F32).at[d_dec:d_dec + d_aaa].set(a2).astype(BF16)
    g2p = jnp.zeros((n_lora, da), F32).at[d_dec + d_aaa:].set(g2).astype(BF16)
    row = lambda t: t.reshape(1, -1).astype(F32)
    vec = lambda w: pl.BlockSpec((1, w), lambda b, c: (0, 0))
    mat = pl.BlockSpec((n_lora, da), lambda b, c: (0, 0))
    return pl.pallas_call(
        _rwkv_body,
        grid=(bsz, nchunk),
        in_specs=[pl.BlockSpec((CHUNK, a_cols), lambda b, c: (b * nchunk + c, 0)),
                  vec(a_cols)] + [vec(da)] * 7 + [mat] * 3,
        out_specs=pl.BlockSpec((CHUNK, da), lambda b, c: (b * nchunk + c, 0)),
        out_shape=jax.ShapeDtypeStruct((n, da), F32),
        scratch_shapes=[pltpu.VMEM((CHUNK + 8, a_cols), F32), pltpu.VMEM((da, da), F32)],
        compiler_params=_cparams("parallel", "arbitrary"),
        name="rwkv7",
    )(p_a, row(mu), row(w0), row(a0), row(k_k), row(k_a), row(r_k), row(lnx_w), row(lnx_b), w2p, a2p, g2p)


def _gdn_body(qkv_ref, z_ref, ba_ref, cw_ref, alog_ref, dtb_ref, nw_ref, selb_ref, selg_ref,
              o_ref, xbuf, state):
    c = CHUNK
    db = o_ref.shape[-1]
    nh = GDN_HEADS
    hd = db // nh
    n = nh * c

    @pl.when(pl.program_id(1) == 0)
    def _():
        xbuf[0:8, :] = jnp.zeros((8, xbuf.shape[1]), F32)
        state[...] = jnp.zeros_like(state)

    xbuf[8:8 + c, :] = qkv_ref[...]
    conv = cw_ref[0:1, :] * xbuf[5:5 + c, :]
    for j in range(1, CONV_WIDTH):
        conv = conv + cw_ref[j:j + 1, :] * xbuf[5 + j:5 + j + c, :]
    xbuf[0:8, :] = xbuf[c:c + 8, :]
    act = _silu(conv)
    q = act[:, 0:db]
    k = act[:, db:2 * db]
    v = act[:, 2 * db:]
    ones_h = _head_ones(db, hd)
    q = q * lax.rsqrt(_exact_dot(q * q, ones_h) + 1e-6) * (hd ** -0.5)
    k = k * lax.rsqrt(_exact_dot(k * k, ones_h) + 1e-6)

    ba = ba_ref[...]
    beta_l = _sigmoid(ba)
    g_l = -jnp.exp(alog_ref[...]) * _softplus(ba + dtb_ref[...])
    tri = jnp.where(_iota((c, c), 1) <= _iota((c, c), 0), 1.0, 0.0).astype(F32)
    gcum_l = jnp.dot(tri, g_l, precision=HIGHEST, preferred_element_type=F32)
    beta = jnp.dot(beta_l, selb_ref[...], precision=HIGHEST, preferred_element_type=F32)
    gcum = jnp.dot(gcum_l, selg_ref[...], precision=HIGHEST, preferred_element_type=F32)
    g_last = gcum[c - 1:c, :]

    gcol = jnp.concatenate([gcum[:, h * hd:h * hd + LANES] for h in range(nh)], axis=0)
    gcol = jnp.concatenate([gcol] * (n // LANES), axis=1)
    strict, incl = _block_tri_masks(n, c)
    dmat = jnp.exp(jnp.where(incl, gcol - gcol.T, NEG_BIG))

    kb = k * beta
    ks = _stack_heads(k, nh)
    gram = _bdot_nt(jnp.concatenate([_stack_heads(kb, nh), _stack_heads(q, nh)], axis=0), ks)
    lower = jnp.where(strict, gram[0:n] * dmat, 0.0)
    intra = jnp.where(incl, gram[n:] * dmat, 0.0)
    t_inv = _unit_lower_inverse(-lower, c)
    e_g = jnp.exp(gcum)
    sol = _bdot(t_inv, jnp.concatenate([_stack_heads(v * beta, nh), _stack_heads(kb * e_g, nh)], axis=1))
    u_s = sol[:, 0:db]
    w_s = sol[:, db:]

    s_old = state[...]
    wq = _bdot(jnp.concatenate([w_s, _stack_heads(q * e_g, nh)], axis=0), s_old)
    v_new = u_s - wq[0:n]
    o_st = wq[n:] + _bdot(intra, v_new)
    state[...] = s_old * jnp.exp(g_last) + _bdot_tn(_stack_heads(k * jnp.exp(g_last - gcum), nh), v_new)
    o = _unstack_heads(o_st, nh)
    o = o * lax.rsqrt(_exact_dot(o * o, ones_h) * (1.0 / hd) + NORM_EPS) * nw_ref[...]
    o_ref[...] = o * _silu(z_ref[...])


def _gdn_group(qkv, z, ba, bsz, seq, conv_w, a_log, dt_bias, norm_w):
    n, w3 = qkv.shape
    db = z.shape[-1]
    nh = GDN_HEADS
    hd = db // nh
    nchunk = seq // CHUNK
    alog_row = jnp.zeros((1, LANES), F32).at[0, nh:2 * nh].set(a_log)
    dtb_row = jnp.zeros((1, LANES), F32).at[0, nh:2 * nh].set(dt_bias)
    lane_head = jnp.arange(db) // hd
    selb = (jnp.arange(LANES)[:, None] == lane_head[None, :]).astype(F32)
    selg = (jnp.arange(LANES)[:, None] == (lane_head[None, :] + nh)).astype(F32)
    nw_row = jnp.tile(norm_w, nh).reshape(1, db)
    rows = lambda w: pl.BlockSpec((CHUNK, w), lambda b, c: (b * nchunk + c, 0))
    const = lambda r, w: pl.BlockSpec((r, w), lambda b, c: (0, 0))
    return pl.pallas_call(
        _gdn_body,
        grid=(bsz, nchunk),
        in_specs=[rows(w3), rows(db), rows(LANES), const(CONV_WIDTH, w3), const(1, LANES), const(1, LANES),
                  const(1, db), const(LANES, db), const(LANES, db)],
        out_specs=rows(db),
        out_shape=jax.ShapeDtypeStruct((n, db), F32),
        scratch_shapes=[pltpu.VMEM((CHUNK + 8, w3), F32), pltpu.VMEM((db, db), F32)],
        compiler_params=_cparams("parallel", "arbitrary"),
        name="gdn",
    )(qkv, z, ba, conv_w, alog_row, dtb_row, nw_row, selb, selg)


def _gdn_group_from_proj(p_b, conv_w, a_log, dt_bias, norm_w):
    bsz, seq, _ = p_b.shape
    db = norm_w.shape[0] * GDN_HEADS
    flat = p_b.reshape(bsz * seq, -1)
    qkv, z, ba = flat[:, :3 * db], flat[:, 3 * db:4 * db], flat[:, 4 * db:]
    ba = jnp.pad(ba, ((0, 0), (0, LANES - ba.shape[1])))
    return _gdn_group(qkv, z, ba, bsz, seq, conv_w, a_log, dt_bias, norm_w).reshape(bsz, seq, db)


def _lru_body(p_ref, cw_ref, cb_ref, wax_ref, ba_ref, bx_ref, lam_ref, o_ref, xbuf, hcar):
    tt = p_ref.shape[0]
    dc = o_ref.shape[-1]

    @pl.when(pl.program_id(1) == 0)
    def _():
        xbuf[0:8, :] = jnp.zeros((8, dc), F32)
        hcar[...] = jnp.zeros_like(hcar)

    xbuf[8:8 + tt, :] = p_ref[:, 0:dc]
    gate = p_ref[:, dc:]
    conv = cb_ref[...] + cw_ref[0:1, :] * xbuf[5:5 + tt, :]
    for j in range(1, CONV_WIDTH):
        conv = conv + cw_ref[j:j + 1, :] * xbuf[5 + j:5 + j + tt, :]
    xbuf[0:8, :] = xbuf[tt:tt + 8, :]

    ri = _bdot(conv, wax_ref[...])
    r = _sigmoid(ri[:, 0:dc] + ba_ref[...])
    i = _sigmoid(ri[:, dc:] + bx_ref[...])
    log_a = -LRU_C * r * _softplus(-lam_ref[...])
    a = jnp.exp(log_a)
    mult = jnp.sqrt(jnp.tanh(-log_a) * (a * a + 1.0))
    row = _iota((tt, dc), 0)
    mult = jnp.where((row == 0) & (pl.program_id(1) == 0), 1.0, mult)
    bv = conv * i * mult

    av = a
    d = 1
    while d < tt:
        a_sh = jnp.where(row >= d, pltpu.roll(av, d, 0), 1.0)
        b_sh = jnp.where(row >= d, pltpu.roll(bv, d, 0), 0.0)
        bv = av * b_sh + bv
        av = av * a_sh
        d *= 2
    h = av * hcar[...] + bv
    hcar[...] = h[tt - 1:tt, :]
    gelu = 0.5 * gate * (1.0 + jnp.tanh(0.7978845608028654 * (gate + 0.044715 * gate * gate * gate)))
    o_ref[...] = h * gelu


def _block_diag(w):
    nb, bi, bo = w.shape
    out = jnp.zeros((nb * bi, nb * bo), w.dtype)
    for b in range(nb):
        out = out.at[b * bi:(b + 1) * bi, b * bo:(b + 1) * bo].set(w[b])
    return out


def _lru_group(p_c, bsz, seq, conv_w, conv_b, w_a, b_a, w_x, b_x, lam, tt=256):
    n, c_cols = p_c.shape
    dc = c_cols // 2
    tt = min(tt, seq)
    nt = seq // tt
    wax = jnp.concatenate([_block_diag(w_a), _block_diag(w_x)], axis=1).astype(BF16)
    row = lambda t: t.reshape(1, -1).astype(F32)
    const = lambda r, w: pl.BlockSpec((r, w), lambda b, c: (0, 0))
    return pl.pallas_call(
        _lru_body,
        grid=(bsz, nt),
        in_specs=[pl.BlockSpec((tt, c_cols), lambda b, c: (b * nt + c, 0)),
                  const(CONV_WIDTH, dc), const(1, dc), const(dc, 2 * dc), const(1, dc), const(1, dc), const(1, dc)],
        out_specs=pl.BlockSpec((tt, dc), lambda b, c: (b * nt + c, 0)),
        out_shape=jax.ShapeDtypeStruct((n, dc), F32),
        scratch_shapes=[pltpu.VMEM((tt + 8, dc), F32), pltpu.VMEM((1, dc), F32)],
        compiler_params=_cparams("parallel", "arbitrary"),
        name="rglru",
    )(p_c, conv_w, row(conv_b), wax, row(b_a), row(b_x), row(lam))


def _outproj_body(x_ref, ma_ref, mb_ref, mc_ref, wo_ref, g_ref, wrh_ref, wrl_ref, br_ref,
                  xo_ref, h_ref, ri_ref, rf_ref):
    da = ma_ref.shape[-1]
    db = mb_ref.shape[-1]
    tm = x_ref.shape[0]
    mix = (jnp.dot(ma_ref[...].astype(BF16), wo_ref[0:da, :], preferred_element_type=F32)
           + jnp.dot(mb_ref[...].astype(BF16), wo_ref[da:da + db, :], preferred_element_type=F32)
           + jnp.dot(mc_ref[...].astype(BF16), wo_ref[da + db:, :], preferred_element_type=F32))
    x = x_ref[...] + mix
    xo_ref[...] = x
    h = x * lax.rsqrt(jnp.mean(x * x, axis=-1, keepdims=True) + NORM_EPS) * g_ref[...]
    h_ref[...] = h

    h_hi = h.astype(BF16)
    h_lo = (h - h_hi.astype(F32)).astype(BF16)
    logits = (jnp.dot(h_hi, wrh_ref[...], preferred_element_type=F32)
              + jnp.dot(h_lo, wrh_ref[...], preferred_element_type=F32)
              + jnp.dot(h_hi, wrl_ref[...], preferred_element_type=F32)) + br_ref[...]
    lane = _iota((tm, LANES), 1)
    far = 4 * LANES
    gmask = lane < MOE_GROUPS
    gl = jnp.where(gmask, logits, NEG_BIG)
    gmax = jnp.max(gl, axis=-1, keepdims=True)
    gidx = jnp.min(jnp.where(gl == gmax, lane, far), axis=-1, keepdims=True)
    gsum = jnp.sum(jnp.where(gmask, jnp.exp(gl - gmax), 0.0), axis=-1, keepdims=True)
    grp_w = 1.0 / gsum
    eidx = lane - MOE_GROUPS
    emask = (eidx >= 0) & (eidx < MOE_EXPERTS) & ((eidx // MOE_PER_GROUP) == gidx)
    el = jnp.where(emask, logits, NEG_BIG)
    v1 = jnp.max(el, axis=-1, keepdims=True)
    i1 = jnp.min(jnp.where(el == v1, lane, far), axis=-1, keepdims=True)
    el2 = jnp.where(lane == i1, NEG_BIG, el)
    v2 = jnp.max(el2, axis=-1, keepdims=True)
    i2 = jnp.min(jnp.where(el2 == v2, lane, far), axis=-1, keepdims=True)
    pr = jnp.exp(v2 - v1)
    g0 = grp_w / (1.0 + pr)
    g1 = g0 * pr
    ri_ref[...] = jnp.where(lane == 0, i1 - MOE_GROUPS, jnp.where(lane == 1, i2 - MOE_GROUPS, 0))
    rf_ref[...] = jnp.where(lane == 0, g0, jnp.where(lane == 1, g1, 0.0))


def _out_proj_router(xt, o_a, o_b, o_c, w_out, gain, w_group, b_group, w_expert, b_expert, tm=512):
    n, d = xt.shape
    tm = min(tm, n)
    ng, ne = w_group.shape[1], w_expert.shape[1]
    wr = jnp.zeros((d, LANES), F32).at[:, 0:ng].set(w_group).at[:, ng:ng + ne].set(w_expert)
    wr_hi = wr.astype(BF16)
    wr_lo = (wr - wr_hi.astype(F32)).astype(BF16)
    br = jnp.zeros((1, LANES), F32).at[0, 0:ng].set(b_group).at[0, ng:ng + ne].set(b_expert)
    rows = lambda w: pl.BlockSpec((tm, w), lambda i: (i, 0))
    const = lambda r, w: pl.BlockSpec((r, w), lambda i: (0, 0))
    return pl.pallas_call(
        _outproj_body,
        grid=(n // tm,),
        in_specs=[rows(d), rows(o_a.shape[1]), rows(o_b.shape[1]), rows(o_c.shape[1]),
                  const(d, d), const(1, d), const(d, LANES), const(d, LANES), const(1, LANES)],
        out_specs=[rows(d), rows(d), rows(LANES), rows(LANES)],
        out_shape=[jax.ShapeDtypeStruct((n, d), F32), jax.ShapeDtypeStruct((n, d), F32),
                   jax.ShapeDtypeStruct((n, LANES), jnp.int32), jax.ShapeDtypeStruct((n, LANES), F32)],
        compiler_params=_cparams("parallel"),
        name="out_proj_router",
    )(xt, o_a, o_b, o_c, w_out.astype(BF16), gain.reshape(1, d), wr_hi, wr_lo, br)


def _rank_body(ri_ref, rank_ref, cnt_ref, carry):
    tr = ri_ref.shape[0]

    @pl.when(pl.program_id(0) == 0)
    def _():
        carry[...] = jnp.zeros_like(carry)

    ri = ri_ref[...]
    lane = _iota((tr, LANES), 1)
    oh0 = lane == ri[:, 0:1]
    oh1 = lane == ri[:, 1:2]
    oh = jnp.where(oh0 | oh1, 1.0, 0.0)
    earlier = jnp.where(_iota((tr, tr), 1) < _iota((tr, tr), 0), 1.0, 0.0).astype(BF16)
    cnt = jnp.dot(earlier, oh.astype(BF16), preferred_element_type=F32) + carry[...]
    r0 = jnp.sum(jnp.where(oh0, cnt, 0.0), axis=-1, keepdims=True)
    r1 = jnp.sum(jnp.where(oh1, cnt, 0.0), axis=-1, keepdims=True)
    rank_ref[...] = jnp.where(lane == 0, r0, jnp.where(lane == 1, r1, 0.0)).astype(jnp.int32)
    carry[...] = carry[...] + jnp.sum(oh, axis=0, keepdims=True)
    cnt_ref[...] = carry[...].astype(jnp.int32)


def _expert_ranks(ri, tr=512):
    n = ri.shape[0]
    tr = min(tr, n)
    return pl.pallas_call(
        _rank_body,
        grid=(n // tr,),
        in_specs=[pl.BlockSpec((tr, LANES), lambda i: (i, 0))],
        out_specs=[pl.BlockSpec((tr, LANES), lambda i: (i, 0)), pl.BlockSpec((1, LANES), lambda i: (0, 0))],
        out_shape=[jax.ShapeDtypeStruct((n, LANES), jnp.int32), jax.ShapeDtypeStruct((1, LANES), jnp.int32)],
        scratch_shapes=[pltpu.VMEM((1, LANES), F32)],
        compiler_params=_cparams("arbitrary"),
        name="expert_ranks",
    )(ri)


def _padded_starts(cnt_row):
    cnt = jnp.broadcast_to(cnt_row, (8, LANES))
    padded = (((cnt + (MOE_BLOCK - 1)) // MOE_BLOCK) * MOE_BLOCK).astype(F32)
    before = jnp.where(_iota((LANES, LANES), 0) < _iota((LANES, LANES), 1), 1.0, 0.0).astype(F32)
    start = jnp.dot(padded, before, precision=HIGHEST, preferred_element_type=F32)
    return start, start + padded


def _dest_body(ri_ref, rank_ref, cnt_ref, dest_ref):
    tr = ri_ref.shape[0]
    start, _ = _padded_starts(cnt_ref[...])
    start = start[0:1, :]
    ri = ri_ref[...]
    rank = rank_ref[...]
    lane = _iota((tr, LANES), 1)
    s0 = jnp.sum(jnp.where(lane == ri[:, 0:1], start, 0.0), axis=-1, keepdims=True).astype(jnp.int32)
    s1 = jnp.sum(jnp.where(lane == ri[:, 1:2], start, 0.0), axis=-1, keepdims=True).astype(jnp.int32)
    dest_ref[...] = jnp.where(lane == 0, s0 + rank[:, 0:1], jnp.where(lane == 1, s1 + rank[:, 1:2], 0))


def _destinations(ri, rank, cnt, tr=512):
    n = ri.shape[0]
    tr = min(tr, n)
    rows = pl.BlockSpec((tr, LANES), lambda i: (i, 0))
    return pl.pallas_call(
        _dest_body,
        grid=(n // tr,),
        in_specs=[rows, rows, pl.BlockSpec((1, LANES), lambda i: (0, 0))],
        out_specs=rows,
        out_shape=jax.ShapeDtypeStruct((n, LANES), jnp.int32),
        compiler_params=_cparams("parallel"),
        name="moe_destinations",
    )(ri, rank, cnt)


def _blkexp_body(cnt_ref, be_ref):
    nbp = be_ref.shape[0]
    _, end = _padded_starts(cnt_ref[...])
    end = end[0:1, :]
    lane = _iota((nbp, LANES), 1)
    row_start = (_iota((nbp, LANES), 0) * MOE_BLOCK).astype(F32)
    done = jnp.where((lane < MOE_EXPERTS) & (end <= row_start), 1.0, 0.0)
    be = jnp.minimum(jnp.sum(done, axis=-1, keepdims=True), MOE_EXPERTS - 1.0)
    be_ref[...] = jnp.broadcast_to(be, (nbp, LANES)).astype(jnp.int32)


def _block_experts(cnt, n_blocks):
    nbp = -(-n_blocks // 8) * 8
    out = pl.pallas_call(
        _blkexp_body,
        grid=(1,),
        in_specs=[pl.BlockSpec((1, LANES), lambda i: (0, 0))],
        out_specs=pl.BlockSpec((nbp, LANES), lambda i: (0, 0)),
        out_shape=jax.ShapeDtypeStruct((nbp, LANES), jnp.int32),
        name="moe_block_experts",
    )(cnt)
    return out[:n_blocks, 0]


def _scatter_body(d0_ref, d1_ref, h_hbm, xz_hbm, xb_hbm, sem):
    del xz_hbm
    ts = d0_ref.shape[0] * d0_ref.shape[1]
    base = pl.program_id(0) * ts

    def copies(t):
        src = h_hbm.at[pl.ds(base + t, 1)]
        i, j = t // LANES, t % LANES
        return (pltpu.make_async_copy(src, xb_hbm.at[pl.ds(d0_ref[i, j], 1)], sem),
                pltpu.make_async_copy(src, xb_hbm.at[pl.ds(d1_ref[i, j], 1)], sem))

    def start(t, carry):
        for cp in copies(t):
            cp.start()
        return carry

    def wait(t, carry):
        for cp in copies(t):
            cp.wait()
        return carry

    lax.fori_loop(0, ts, start, 0)
    lax.fori_loop(0, ts, wait, 0)


def _scatter_rows(h2, dest0, dest1, n_rows, ts):
    n, d = h2.shape
    smem = pl.BlockSpec((ts // LANES, LANES), lambda i: (i, 0), memory_space=pltpu.SMEM)
    anyspec = pl.BlockSpec(memory_space=pl.ANY)
    return pl.pallas_call(
        _scatter_body,
        grid=(n // ts,),
        in_specs=[smem, smem, anyspec, anyspec],
        out_specs=anyspec,
        out_shape=jax.ShapeDtypeStruct((n_rows, d), F32),
        scratch_shapes=[pltpu.SemaphoreType.DMA(())],
        input_output_aliases={3: 0},
        compiler_params=_cparams("arbitrary"),
        name="moe_scatter",
    )(dest0, dest1, h2, jnp.zeros((n_rows, d), F32))


def _expert_body(be_ref, x_ref, wgu_ref, wd_ref, y_ref):
    del be_ref
    de = wd_ref.shape[1]
    gu = jnp.dot(x_ref[...].astype(BF16), wgu_ref[0], preferred_element_type=F32)
    hid = _silu(gu[:, 0:de]) * gu[:, de:]
    y_ref[...] = jnp.dot(hid.astype(BF16), wd_ref[0], preferred_element_type=F32)


def _expert_mlp(xb, blk_expert, w_gu, w_down):
    n_rows, d = xb.shape
    nb = n_rows // MOE_BLOCK
    de = w_down.shape[1]
    grid_spec = pltpu.PrefetchScalarGridSpec(
        num_scalar_prefetch=1,
        grid=(nb,),
        in_specs=[pl.BlockSpec((MOE_BLOCK, d), lambda i, be: (i, 0)),
                  pl.BlockSpec((1, d, 2 * de), lambda i, be: (be[i], 0, 0)),
                  pl.BlockSpec((1, de, d), lambda i, be: (be[i], 0, 0))],
        out_specs=pl.BlockSpec((MOE_BLOCK, d), lambda i, be: (i, 0)),
    )
    return pl.pallas_call(
        _expert_body,
        grid_spec=grid_spec,
        out_shape=jax.ShapeDtypeStruct((n_rows, d), F32),
        compiler_params=_cparams("arbitrary"),
        name="moe_experts",
    )(blk_expert, xb, w_gu, w_down)


def _combine_body(final_norm, d0_ref, d1_ref, rf_ref, x_ref, g_ref, yb_hbm, o_ref, gbuf, sem):
    ts = x_ref.shape[0]

    def copies(t):
        i, j = t // LANES, t % LANES
        return (pltpu.make_async_copy(yb_hbm.at[pl.ds(d0_ref[i, j], 1)], gbuf.at[0, pl.ds(t, 1)], sem),
                pltpu.make_async_copy(yb_hbm.at[pl.ds(d1_ref[i, j], 1)], gbuf.at[1, pl.ds(t, 1)], sem))

    def start(t, carry):
        for cp in copies(t):
            cp.start()
        return carry

    def wait(t, carry):
        for cp in copies(t):
            cp.wait()
        return carry

    lax.fori_loop(0, ts, start, 0)
    lax.fori_loop(0, ts, wait, 0)
    rf = rf_ref[...]
    x = x_ref[...] + (gbuf[0] * rf[:, 0:1] + gbuf[1] * rf[:, 1:2])
    if final_norm:
        x = x * lax.rsqrt(jnp.mean(x * x, axis=-1, keepdims=True) + NORM_EPS) * g_ref[...]
    o_ref[...] = x


def _combine(x_mid, yb, dest0, dest1, rf, final_gain, ts):
    n, d = x_mid.shape
    final_norm = final_gain is not None
    gain = (final_gain if final_norm else jnp.ones((d,), F32)).reshape(1, d)
    smem = pl.BlockSpec((ts // LANES, LANES), lambda i: (i, 0), memory_space=pltpu.SMEM)
    rows = lambda w: pl.BlockSpec((ts, w), lambda i: (i, 0))
    return pl.pallas_call(
        functools.partial(_combine_body, final_norm),
        grid=(n // ts,),
        in_specs=[smem, smem, rows(LANES), rows(d), pl.BlockSpec((1, d), lambda i: (0, 0)),
                  pl.BlockSpec(memory_space=pl.ANY)],
        out_specs=rows(d),
        out_shape=jax.ShapeDtypeStruct((n, d), F32),
        scratch_shapes=[pltpu.VMEM((2, ts, d), F32), pltpu.SemaphoreType.DMA(())],
        compiler_params=_cparams("arbitrary"),
        name="moe_combine",
    )(dest0, dest1, rf, x_mid, gain, yb)


def _moe(x_mid, h2, ri, rf, w_gate, w_up, w_down, final_gain):
    n, d = x_mid.shape
    n_assign = 2 * n
    n_blocks = (n_assign + MOE_EXPERTS * (MOE_BLOCK - 1) + MOE_BLOCK - 1) // MOE_BLOCK
    ts = min(1024, n)
    rank, cnt = _expert_ranks(ri)
    dest = _destinations(ri, rank, cnt)
    blk_expert = _block_experts(cnt, n_blocks)
    dest0 = dest[:, 0].reshape(n // LANES, LANES)
    dest1 = dest[:, 1].reshape(n // LANES, LANES)
    xb = _scatter_rows(h2, dest0, dest1, n_blocks * MOE_BLOCK, ts)
    w_gu = jnp.concatenate([w_gate, w_up], axis=-1).astype(BF16)
    yb = _expert_mlp(xb, blk_expert, w_gu, w_down.astype(BF16))
    return _combine(x_mid, yb, dest0, dest1, rf, final_gain, ts)


def kernel(x, norm_mix, norm_ffn, norm_final, w_in, w_out, rwkv_mu, rwkv_w0, rwkv_w2, rwkv_a0, rwkv_a2, rwkv_g2, rwkv_k_k, rwkv_k_a, rwkv_r_k, rwkv_lnx_w, rwkv_lnx_b, gdn_conv_w, gdn_a_log, gdn_dt_bias, gdn_norm_w, lru_conv_w, lru_conv_b, lru_w_a, lru_b_a, lru_w_x, lru_b_x, lru_lambda, moe_w_group, moe_b_group, moe_w_expert, moe_b_expert, moe_w_gate, moe_w_up, moe_w_down):
    bsz, seq, d = x.shape
    depth = w_in.shape[0]
    n = bsz * seq
    a_cols = rwkv_mu.shape[1]
    db = gdn_norm_w.shape[1] * GDN_HEADS
    dc = lru_conv_b.shape[1]
    n_ba = 2 * GDN_HEADS
    widths = (a_cols, 3 * db, db, 2 * dc, LANES)
    xt = x.reshape(n, d)
    for l in range(depth):
        w = w_in[l]
        b0 = a_cols
        c0 = b0 + 4 * db + n_ba
        w_cat = jnp.concatenate(
            [w[:, 0:a_cols], w[:, b0:b0 + 4 * db], w[:, c0:c0 + 2 * dc],
             w[:, b0 + 4 * db:c0], jnp.zeros((d, LANES - n_ba), F32)], axis=1).astype(BF16)
        p_a, p_qkv, p_z, p_c, p_ba = _in_proj(xt, norm_mix[l], w_cat, widths, tm=min(512, n))
        o_a = _rwkv_group(p_a, bsz, seq, rwkv_mu[l], rwkv_w0[l], rwkv_w2[l], rwkv_a0[l], rwkv_a2[l], rwkv_g2[l],
                          rwkv_k_k[l], rwkv_k_a[l], rwkv_r_k[l], rwkv_lnx_w[l], rwkv_lnx_b[l])
        o_b = _gdn_group(p_qkv, p_z, p_ba, bsz, seq, gdn_conv_w[l], gdn_a_log[l], gdn_dt_bias[l], gdn_norm_w[l])
        o_c = _lru_group(p_c, bsz, seq, lru_conv_w[l], lru_conv_b[l], lru_w_a[l], lru_b_a[l], lru_w_x[l],
                         lru_b_x[l], lru_lambda[l])
        x_mid, h2, ri, rf = _out_proj_router(xt, o_a, o_b, o_c, w_out[l], norm_ffn[l], moe_w_group[l],
                                             moe_b_group[l], moe_w_expert[l], moe_b_expert[l])
        xt = _moe(x_mid, h2, ri, rf, moe_w_gate[l], moe_w_up[l], moe_w_down[l],
                  norm_final if l == depth - 1 else None)
    return xt.reshape(bsz, seq, d)
```

```python
import functools

import jax
import jax.numpy as jnp
from jax import lax
from jax.experimental import pallas as pl
from jax.experimental.pallas import tpu as pltpu

F32 = jnp.float32
BF16 = jnp.bfloat16
HIGHEST = lax.Precision.HIGHEST

NORM_EPS = 1e-6
CONV_WIDTH = 4
CHUNK = 64
SCAN_TILE = 256
SCAN_SEQS = 1
RWKV_HEADS = 4
RWKV_GN_EPS = 64e-5
GDN_HEADS = 4
LRU_C = 8.0
MOE_GROUPS = 4
MOE_PER_GROUP = 8
MOE_EXPERTS = MOE_GROUPS * MOE_PER_GROUP
MOE_BLOCK = 128
LANES = 128
NEG_BIG = -1e30
VMEM_LIMIT = 56 * 1024 * 1024


def _cparams(*sem):
    return pltpu.CompilerParams(dimension_semantics=sem, vmem_limit_bytes=VMEM_LIMIT)


def _bdot(a, b):
    return jnp.dot(a.astype(BF16), b.astype(BF16), preferred_element_type=F32)


def _bdot_nt(a, b):
    return lax.dot_general(a.astype(BF16), b.astype(BF16), (((1,), (1,)), ((), ())),
                           preferred_element_type=F32)


def _bdot_tn(a, b):
    return lax.dot_general(a.astype(BF16), b.astype(BF16), (((0,), (0,)), ((), ())),
                           preferred_element_type=F32)


def _exact_dot(a, b01):
    hi = a.astype(BF16)
    lo = (a - hi.astype(F32)).astype(BF16)
    return (jnp.dot(hi, b01, preferred_element_type=F32)
            + jnp.dot(lo, b01, preferred_element_type=F32))


def _exact_dot_left(a01, b):
    hi = b.astype(BF16)
    lo = (b - hi.astype(F32)).astype(BF16)
    return (jnp.dot(a01, hi, preferred_element_type=F32)
            + jnp.dot(a01, lo, preferred_element_type=F32))


def _softplus(x):
    return jnp.maximum(x, 0.0) + jnp.log1p(jnp.exp(-jnp.abs(x)))


def _sigmoid(x):
    return 1.0 / (1.0 + jnp.exp(-x))


def _silu(x):
    return x * _sigmoid(x)


def _iota(shape, dim):
    return lax.broadcasted_iota(jnp.int32, shape, dim)


def _head_ones(width, head_dim):
    r = _iota((width, width), 0) // head_dim
    c = _iota((width, width), 1) // head_dim
    return jnp.where(r == c, 1.0, 0.0).astype(BF16)


def _stack_heads(x, n_heads):
    c, w = x.shape
    hd = w // n_heads
    xb = x.astype(BF16)
    lane_head = _iota((c, w), 1) // hd
    return jnp.concatenate([jnp.where(lane_head == h, xb, jnp.zeros_like(xb)) for h in range(n_heads)],
                           axis=0)


def _tri_masks(c, n_heads):
    r = _iota((c, n_heads * c), 0)
    s = _iota((c, n_heads * c), 1) % c
    return s < r, s <= r


def _unit_lower_inverses(l_mats, n_heads):
    shape = l_mats[0].shape
    c = shape[0]
    eye = jnp.where(_iota(shape, 1) % c == _iota(shape, 0), 1.0, 0.0).astype(F32)
    ps = [eye + l for l in l_mats]
    ms = list(l_mats)
    span = 1
    while span * 2 < c:
        ms = [_bdot(m, _stack_heads(m, n_heads)) for m in ms]
        ps = [p + _bdot(m, _stack_heads(p, n_heads)) for m, p in zip(ms, ps)]
        span *= 2
    return ps


def _head_block_mask(width, head_dim):
    r = _iota((width, width), 0) // head_dim
    c = _iota((width, width), 1) // head_dim
    return r == c


def _inproj_body(x_ref, g_ref, w_ref, *out_refs):
    x = x_ref[...]
    h = x * lax.rsqrt(jnp.mean(x * x, axis=-1, keepdims=True) + NORM_EPS) * g_ref[...]
    hb = h.astype(BF16)
    off = 0
    for o_ref in out_refs:
        wdt = o_ref.shape[-1]
        o_ref[...] = jnp.dot(hb, w_ref[:, off:off + wdt], preferred_element_type=F32)
        off += wdt


def _in_proj(xt, gain, w_cat, widths, tm=512):
    n, d = xt.shape
    ntot = w_cat.shape[1]
    return pl.pallas_call(
        _inproj_body,
        grid=(n // tm,),
        in_specs=[pl.BlockSpec((tm, d), lambda i: (i, 0)),
                  pl.BlockSpec((1, d), lambda i: (0, 0)),
                  pl.BlockSpec((d, ntot), lambda i: (0, 0))],
        out_specs=[pl.BlockSpec((tm, w), lambda i: (i, 0)) for w in widths],
        out_shape=[jax.ShapeDtypeStruct((n, w), F32) for w in widths],
        compiler_params=_cparams("parallel"),
        name="in_proj",
    )(xt, gain.reshape(1, d), w_cat)


def _per_sequence(one_sequence):
    def body(*refs):
        n_seq = refs[0].shape[0]
        xbuf, state = refs[-2:]

        @pl.when(pl.program_id(1) == 0)
        def _():
            xbuf[:, 0:8, :] = jnp.zeros((n_seq, 8, xbuf.shape[2]), F32)
            state[...] = jnp.zeros_like(state)

        for s in range(n_seq):
            one_sequence(*[r.at[s] if len(r.shape) == 3 else r for r in refs])
    return body


def _rwkv_one(p_ref, mu_ref, w0_ref, a0_ref, kk_ref, ka_ref, rk_ref, lnw_ref, lnb_ref,
              w2_ref, a2_ref, g2_ref, o_ref, pbuf, state):
    c = CHUNK
    tt, da = o_ref.shape
    nh = RWKV_HEADS
    hd = da // nh

    pbuf[8:8 + tt, :] = p_ref[...]
    p = pbuf[8:8 + tt, :]
    p_prev = pbuf[7:7 + tt, :]
    pbuf[0:8, :] = pbuf[tt:tt + 8, :]
    xs = p + (p_prev - p) * mu_ref[...]
    r = xs[:, 0:da]
    k = xs[:, da:2 * da]
    v = xs[:, 2 * da:3 * da]
    lo = xs[:, 3 * da:]

    w_log = -_softplus(-(w0_ref[...] + _bdot(jnp.tanh(lo), w2_ref[...]))) - 0.5
    lw = -jnp.exp(w_log)
    a = _sigmoid(a0_ref[...] + _bdot(lo, a2_ref[...]))
    g = _bdot(_sigmoid(lo), g2_ref[...])

    ones_h = _head_ones(da, hd)
    kkr = k * kk_ref[...]
    kk = kkr * lax.rsqrt(_exact_dot(kkr * kkr, ones_h) + 1e-6)
    k2 = k * (1.0 + (a - 1.0) * ka_ref[...])
    alpha = -(kk * a)

    rr = _iota((tt, tt), 0)
    cc = _iota((tt, tt), 1)
    tri = jnp.where((cc <= rr) & (cc // c == rr // c), 1.0, 0.0).astype(BF16)
    cum = _exact_dot_left(tri, lw)
    n_chunks = tt // c
    chunk_rows = [slice(g * c, (g + 1) * c) for g in range(n_chunks)]
    cum_ends = [cum[(g + 1) * c - 1:(g + 1) * c, :] for g in range(n_chunks)]
    cum_last = jnp.concatenate([jnp.broadcast_to(ce, (c, da)) for ce in cum_ends], axis=0)
    e_pos = jnp.exp(cum)
    e_neg = jnp.exp(-cum)
    e_tail = jnp.exp(cum_last - cum)
    r_t = r * e_pos
    b_t = kk * jnp.exp(cum - lw)
    a_t = alpha * e_neg
    k_t = k2 * e_neg
    a_end = alpha * e_tail
    k_end = k2 * e_tail

    nl = nh * c
    strict, incl = _tri_masks(c, nh)
    grams = [_bdot_nt(jnp.concatenate([b_t[rows], r_t[rows]], axis=0),
                      jnp.concatenate([_stack_heads(a_t[rows], nh), _stack_heads(k_t[rows], nh)], axis=0))
             for rows in chunk_rows]
    t_invs = _unit_lower_inverses([jnp.where(strict, gm[0:c, 0:nl], 0.0) for gm in grams], nh)
    vss = [_stack_heads(v[rows], nh) for rows in chunk_rows]
    lkvs = [_bdot(jnp.where(strict, gm[0:c, nl:], 0.0), vs) for gm, vs in zip(grams, vss)]
    wus = [_bdot(t_inv, jnp.concatenate([_stack_heads(b_t[rows], nh), _stack_heads(lkv, nh)], axis=1))
           for t_inv, lkv, rows in zip(t_invs, lkvs, chunk_rows)]
    s_aks = [jnp.concatenate([jnp.where(incl, gm[c:, 0:nl], 0.0), jnp.where(incl, gm[c:, nl:], 0.0)], axis=1)
             for gm in grams]

    s_cur = state[...]
    block_diag = _head_block_mask(da, hd)
    outs = []
    for rows, ce, wu, s_ak, vs in zip(chunk_rows, cum_ends, wus, s_aks, vss):
        w_mat, u0 = wu[:, 0:da], wu[:, da:]
        ws_rs = _bdot_nt(jnp.concatenate([w_mat, r_t[rows]], axis=0), s_cur)
        u = u0 + ws_rs[0:c]
        outs.append(ws_rs[c:] + _bdot(s_ak, jnp.concatenate([_stack_heads(u, nh), vs], axis=0)))
        upd = _bdot_tn(jnp.concatenate([u, v[rows]], axis=0),
                       jnp.concatenate([a_end[rows], k_end[rows]], axis=0))
        s_cur = s_cur * jnp.exp(ce) + jnp.where(block_diag, upd, 0.0)
    state[...] = s_cur
    o = jnp.concatenate(outs, axis=0)

    inv_hd = 1.0 / hd
    mean = _exact_dot(o, ones_h) * inv_hd
    cen = o - mean
    var = _exact_dot(cen * cen, ones_h) * inv_hd
    o_n = cen * lax.rsqrt(var + RWKV_GN_EPS) * lnw_ref[...] + lnb_ref[...]
    bonus = _exact_dot(r * k2 * rk_ref[...], ones_h) * v
    o_ref[...] = (o_n + bonus) * g


def _rwkv_group(p_a, bsz, seq, mu, w0, w2, a0, a2, g2, k_k, k_a, r_k, lnx_w, lnx_b, tt=SCAN_TILE):
    n, a_cols = p_a.shape
    da = w0.shape[-1]
    n_lora = a_cols - 3 * da
    tt = min(tt, seq)
    nchunk = seq // tt
    d_dec, d_aaa = w2.shape[0], a2.shape[0]
    w2p = jnp.zeros((n_lora, da), F32).at[0:d_dec].set(w2).astype(BF16)
    a2p = jnp.zeros((n_lora, da), F32).at[d_dec:d_dec + d_aaa].set(a2).astype(BF16)
    g2p = jnp.zeros((n_lora, da), F32).at[d_dec + d_aaa:].set(g2).astype(BF16)
    row = lambda t: t.reshape(1, -1).astype(F32)
    vec = lambda w: pl.BlockSpec((1, w), lambda b, c: (0, 0))
    mat = pl.BlockSpec((n_lora, da), lambda b, c: (0, 0))
    nb = min(SCAN_SEQS, bsz)
    seqs = lambda w: pl.BlockSpec((nb, tt, w), lambda b, c: (b, c, 0))
    out = pl.pallas_call(
        _per_sequence(_rwkv_one),
        grid=(bsz // nb, nchunk),
        in_specs=[seqs(a_cols), vec(a_cols)] + [vec(da)] * 7 + [mat] * 3,
        out_specs=seqs(da),
        out_shape=jax.ShapeDtypeStruct((bsz, seq, da), F32),
        scratch_shapes=[pltpu.VMEM((nb, tt + 8, a_cols), F32), pltpu.VMEM((nb, da, da), F32)],
        compiler_params=_cparams("parallel", "arbitrary"),
        name="rwkv7",
    )(p_a.reshape(bsz, seq, a_cols), row(mu), row(w0), row(a0), row(k_k), row(k_a), row(r_k), row(lnx_w),
      row(lnx_b), w2p, a2p, g2p)
    return out.reshape(n, da)


def _gdn_one(qkv_ref, z_ref, ba_ref, cw_ref, alog_ref, dtb_ref, nw_ref, o_ref, xbuf, state):
    c = CHUNK
    tt, db = o_ref.shape
    nh = GDN_HEADS
    hd = db // nh
    nl = nh * c

    xbuf[8:8 + tt, :] = qkv_ref[...]
    conv = cw_ref[0:1, :] * xbuf[5:5 + tt, :]
    for j in range(1, CONV_WIDTH):
        conv = conv + cw_ref[j:j + 1, :] * xbuf[5 + j:5 + j + tt, :]
    xbuf[0:8, :] = xbuf[tt:tt + 8, :]
    act = _silu(conv)
    q = act[:, 0:db]
    k = act[:, db:2 * db]
    v = act[:, 2 * db:]

    def per_head(col_of_head):
        return jnp.concatenate([jnp.broadcast_to(col_of_head(h), (tt, hd)) for h in range(nh)], axis=1)

    def head_sumsq(x):
        return per_head(lambda h: jnp.sum(x[:, h * hd:(h + 1) * hd] * x[:, h * hd:(h + 1) * hd],
                                          axis=-1, keepdims=True))

    q = q * lax.rsqrt(head_sumsq(q) + 1e-6) * (hd ** -0.5)
    k = k * lax.rsqrt(head_sumsq(k) + 1e-6)

    ba = ba_ref[...]
    beta_l = _sigmoid(ba)
    g_l = -jnp.exp(alog_ref[...]) * _softplus(ba + dtb_ref[...])
    row_in_chunk = _iota((tt, LANES), 0) % c
    gcum_l = g_l
    d = 1
    while d < c:
        gcum_l = gcum_l + jnp.where(row_in_chunk >= d, pltpu.roll(gcum_l, d, 0), 0.0)
        d *= 2
    n_chunks = tt // c
    chunk_rows = [slice(g * c, (g + 1) * c) for g in range(n_chunks)]
    g_last_l = jnp.concatenate([jnp.broadcast_to(gcum_l[(g + 1) * c - 1:(g + 1) * c, :], (c, LANES))
                                for g in range(n_chunks)], axis=0)
    e_g_l = jnp.exp(gcum_l)
    e_tail_l = jnp.exp(g_last_l - gcum_l)
    beta = per_head(lambda h: beta_l[:, h:h + 1])
    e_g = per_head(lambda h: e_g_l[:, nh + h:nh + h + 1])
    e_tail = per_head(lambda h: e_tail_l[:, nh + h:nh + h + 1])
    kb = k * beta
    vb = v * beta
    kbe = kb * e_g
    qe = q * e_g
    k_end = k * e_tail

    strict, incl = _tri_masks(c, nh)
    lane_head = _iota((c, nl), 1) // c
    on_diag = (_iota((c, nl), 1) % c) == _iota((c, nl), 0)
    dmats = []
    for rows in chunk_rows:
        gc = gcum_l[rows]
        gcol = jnp.broadcast_to(gc[:, nh:nh + 1], (c, nl))
        for h in range(1, nh):
            gcol = jnp.where(lane_head == h, jnp.broadcast_to(gc[:, nh + h:nh + h + 1], (c, nl)), gcol)
        grow = jnp.sum(jnp.where(on_diag, gcol, 0.0), axis=0, keepdims=True)
        dmats.append(jnp.exp(jnp.where(incl, gcol - grow, NEG_BIG)))
    grams = [_bdot_nt(jnp.concatenate([kb[rows], q[rows]], axis=0), _stack_heads(k[rows], nh))
             for rows in chunk_rows]
    t_invs = _unit_lower_inverses([jnp.where(strict, -(gm[0:c] * dm), 0.0) for gm, dm in zip(grams, dmats)], nh)
    intras = [jnp.where(incl, gm[c:] * dm, 0.0) for gm, dm in zip(grams, dmats)]
    sols = [_bdot(t_inv, jnp.concatenate([_stack_heads(vb[rows], nh), _stack_heads(kbe[rows], nh)], axis=1))
            for t_inv, rows in zip(t_invs, chunk_rows)]

    s_cur = state[...]
    block_diag = _head_block_mask(db, hd)
    outs = []
    for g, (rows, sol, intra) in enumerate(zip(chunk_rows, sols, intras)):
        u_s, w_s = sol[:, 0:db], sol[:, db:]
        wq = _bdot(jnp.concatenate([w_s, qe[rows]], axis=0), s_cur)
        v_new = u_s - wq[0:c]
        outs.append(wq[c:] + _bdot(intra, _stack_heads(v_new, nh)))
        upd = _bdot_tn(k_end[rows], v_new)
        s_cur = s_cur * e_g[(g + 1) * c - 1:(g + 1) * c, :] + jnp.where(block_diag, upd, 0.0)
    state[...] = s_cur
    o = jnp.concatenate(outs, axis=0)
    o = o * lax.rsqrt(head_sumsq(o) * (1.0 / hd) + NORM_EPS) * nw_ref[...]
    o_ref[...] = o * _silu(z_ref[...])


def _gdn_group(qkv, z, ba, bsz, seq, conv_w, a_log, dt_bias, norm_w, tt=SCAN_TILE):
    n, w3 = qkv.shape
    db = z.shape[-1]
    nh = GDN_HEADS
    tt = min(tt, seq)
    nchunk = seq // tt
    alog_row = jnp.zeros((1, LANES), F32).at[0, nh:2 * nh].set(a_log)
    dtb_row = jnp.zeros((1, LANES), F32).at[0, nh:2 * nh].set(dt_bias)
    nw_row = jnp.tile(norm_w, nh).reshape(1, db)
    nb = min(SCAN_SEQS, bsz)
    seqs = lambda w: pl.BlockSpec((nb, tt, w), lambda b, c: (b, c, 0))
    const = lambda r, w: pl.BlockSpec((r, w), lambda b, c: (0, 0))
    out = pl.pallas_call(
        _per_sequence(_gdn_one),
        grid=(bsz // nb, nchunk),
        in_specs=[seqs(w3), seqs(db), seqs(LANES), const(CONV_WIDTH, w3), const(1, LANES), const(1, LANES),
                  const(1, db)],
        out_specs=seqs(db),
        out_shape=jax.ShapeDtypeStruct((bsz, seq, db), F32),
        scratch_shapes=[pltpu.VMEM((nb, tt + 8, w3), F32), pltpu.VMEM((nb, db, db), F32)],
        compiler_params=_cparams("parallel", "arbitrary"),
        name="gdn",
    )(qkv.reshape(bsz, seq, w3), z.reshape(bsz, seq, db), ba.reshape(bsz, seq, LANES), conv_w, alog_row,
      dtb_row, nw_row)
    return out.reshape(n, db)


def _gdn_group_from_proj(p_b, conv_w, a_log, dt_bias, norm_w):
    bsz, seq, _ = p_b.shape
    db = norm_w.shape[0] * GDN_HEADS
    flat = p_b.reshape(bsz * seq, -1)
    qkv, z, ba = flat[:, :3 * db], flat[:, 3 * db:4 * db], flat[:, 4 * db:]
    ba = jnp.pad(ba, ((0, 0), (0, LANES - ba.shape[1])))
    return _gdn_group(qkv, z, ba, bsz, seq, conv_w, a_log, dt_bias, norm_w).reshape(bsz, seq, db)


def _lru_body(p_ref, cw_ref, cb_ref, wax_ref, ba_ref, bx_ref, lam_ref, o_ref, xbuf, hcar):
    tt = p_ref.shape[0]
    dc = o_ref.shape[-1]

    @pl.when(pl.program_id(1) == 0)
    def _():
        xbuf[0:8, :] = jnp.zeros((8, dc), F32)
        hcar[...] = jnp.zeros_like(hcar)

    xbuf[8:8 + tt, :] = p_ref[:, 0:dc]
    gate = p_ref[:, dc:]
    conv = cb_ref[...] + cw_ref[0:1, :] * xbuf[5:5 + tt, :]
    for j in range(1, CONV_WIDTH):
        conv = conv + cw_ref[j:j + 1, :] * xbuf[5 + j:5 + j + tt, :]
    xbuf[0:8, :] = xbuf[tt:tt + 8, :]

    ri = _bdot(conv, wax_ref[...])
    r = _sigmoid(ri[:, 0:dc] + ba_ref[...])
    i = _sigmoid(ri[:, dc:] + bx_ref[...])
    log_a = -LRU_C * r * _softplus(-lam_ref[...])
    a = jnp.exp(log_a)
    mult = jnp.sqrt(jnp.tanh(-log_a) * (a * a + 1.0))
    row = _iota((tt, dc), 0)
    mult = jnp.where((row == 0) & (pl.program_id(1) == 0), 1.0, mult)
    bv = conv * i * mult

    av = a
    d = 1
    while d < tt:
        a_sh = jnp.where(row >= d, pltpu.roll(av, d, 0), 1.0)
        b_sh = jnp.where(row >= d, pltpu.roll(bv, d, 0), 0.0)
        bv = av * b_sh + bv
        av = av * a_sh
        d *= 2
    h = av * hcar[...] + bv
    hcar[...] = h[tt - 1:tt, :]
    gelu = 0.5 * gate * (1.0 + jnp.tanh(0.7978845608028654 * (gate + 0.044715 * gate * gate * gate)))
    o_ref[...] = h * gelu


def _block_diag(w):
    nb, bi, bo = w.shape
    out = jnp.zeros((nb * bi, nb * bo), w.dtype)
    for b in range(nb):
        out = out.at[b * bi:(b + 1) * bi, b * bo:(b + 1) * bo].set(w[b])
    return out


def _lru_group(p_c, bsz, seq, conv_w, conv_b, w_a, b_a, w_x, b_x, lam, tt=256):
    n, c_cols = p_c.shape
    dc = c_cols // 2
    tt = min(tt, seq)
    nt = seq // tt
    wax = jnp.concatenate([_block_diag(w_a), _block_diag(w_x)], axis=1).astype(BF16)
    row = lambda t: t.reshape(1, -1).astype(F32)
    const = lambda r, w: pl.BlockSpec((r, w), lambda b, c: (0, 0))
    return pl.pallas_call(
        _lru_body,
        grid=(bsz, nt),
        in_specs=[pl.BlockSpec((tt, c_cols), lambda b, c: (b * nt + c, 0)),
                  const(CONV_WIDTH, dc), const(1, dc), const(dc, 2 * dc), const(1, dc), const(1, dc), const(1, dc)],
        out_specs=pl.BlockSpec((tt, dc), lambda b, c: (b * nt + c, 0)),
        out_shape=jax.ShapeDtypeStruct((n, dc), F32),
        scratch_shapes=[pltpu.VMEM((tt + 8, dc), F32), pltpu.VMEM((1, dc), F32)],
        compiler_params=_cparams("parallel", "arbitrary"),
        name="rglru",
    )(p_c, conv_w, row(conv_b), wax, row(b_a), row(b_x), row(lam))


def _outproj_body(x_ref, ma_ref, mb_ref, mc_ref, wo_ref, g_ref, wrh_ref, wrl_ref, br_ref,
                  xo_ref, h_ref, ri_ref, rf_ref):
    da = ma_ref.shape[-1]
    db = mb_ref.shape[-1]
    tm = x_ref.shape[0]
    mix = (jnp.dot(ma_ref[...].astype(BF16), wo_ref[0:da, :], preferred_element_type=F32)
           + jnp.dot(mb_ref[...].astype(BF16), wo_ref[da:da + db, :], preferred_element_type=F32)
           + jnp.dot(mc_ref[...].astype(BF16), wo_ref[da + db:, :], preferred_element_type=F32))
    x = x_ref[...] + mix
    xo_ref[...] = x
    h = x * lax.rsqrt(jnp.mean(x * x, axis=-1, keepdims=True) + NORM_EPS) * g_ref[...]
    h_ref[...] = h

    h_hi = h.astype(BF16)
    h_lo = (h - h_hi.astype(F32)).astype(BF16)
    logits = (jnp.dot(h_hi, wrh_ref[...], preferred_element_type=F32)
              + jnp.dot(h_lo, wrh_ref[...], preferred_element_type=F32)
              + jnp.dot(h_hi, wrl_ref[...], preferred_element_type=F32)) + br_ref[...]
    lane = _iota((tm, LANES), 1)
    far = 4 * LANES
    gmask = lane < MOE_GROUPS
    gl = jnp.where(gmask, logits, NEG_BIG)
    gmax = jnp.max(gl, axis=-1, keepdims=True)
    gidx = jnp.min(jnp.where(gl == gmax, lane, far), axis=-1, keepdims=True)
    gsum = jnp.sum(jnp.where(gmask, jnp.exp(gl - gmax), 0.0), axis=-1, keepdims=True)
    grp_w = 1.0 / gsum
    eidx = lane - MOE_GROUPS
    emask = (eidx >= 0) & (eidx < MOE_EXPERTS) & ((eidx // MOE_PER_GROUP) == gidx)
    el = jnp.where(emask, logits, NEG_BIG)
    v1 = jnp.max(el, axis=-1, keepdims=True)
    i1 = jnp.min(jnp.where(el == v1, lane, far), axis=-1, keepdims=True)
    el2 = jnp.where(lane == i1, NEG_BIG, el)
    v2 = jnp.max(el2, axis=-1, keepdims=True)
    i2 = jnp.min(jnp.where(el2 == v2, lane, far), axis=-1, keepdims=True)
    pr = jnp.exp(v2 - v1)
    g0 = grp_w / (1.0 + pr)
    g1 = g0 * pr
    ri_ref[...] = jnp.where(lane == 0, i1 - MOE_GROUPS, jnp.where(lane == 1, i2 - MOE_GROUPS, 0))
    rf_ref[...] = jnp.where(lane == 0, g0, jnp.where(lane == 1, g1, 0.0))


def _out_proj_router(xt, o_a, o_b, o_c, w_out, gain, w_group, b_group, w_expert, b_expert, tm=512):
    n, d = xt.shape
    tm = min(tm, n)
    ng, ne = w_group.shape[1], w_expert.shape[1]
    wr = jnp.zeros((d, LANES), F32).at[:, 0:ng].set(w_group).at[:, ng:ng + ne].set(w_expert)
    wr_hi = wr.astype(BF16)
    wr_lo = (wr - wr_hi.astype(F32)).astype(BF16)
    br = jnp.zeros((1, LANES), F32).at[0, 0:ng].set(b_group).at[0, ng:ng + ne].set(b_expert)
    rows = lambda w: pl.BlockSpec((tm, w), lambda i: (i, 0))
    const = lambda r, w: pl.BlockSpec((r, w), lambda i: (0, 0))
    return pl.pallas_call(
        _outproj_body,
        grid=(n // tm,),
        in_specs=[rows(d), rows(o_a.shape[1]), rows(o_b.shape[1]), rows(o_c.shape[1]),
                  const(d, d), const(1, d), const(d, LANES), const(d, LANES), const(1, LANES)],
        out_specs=[rows(d), rows(d), rows(LANES), rows(LANES)],
        out_shape=[jax.ShapeDtypeStruct((n, d), F32), jax.ShapeDtypeStruct((n, d), F32),
                   jax.ShapeDtypeStruct((n, LANES), jnp.int32), jax.ShapeDtypeStruct((n, LANES), F32)],
        compiler_params=_cparams("parallel"),
        name="out_proj_router",
    )(xt, o_a, o_b, o_c, w_out.astype(BF16), gain.reshape(1, d), wr_hi, wr_lo, br)


def _rank_body(ri_ref, rank_ref, cnt_ref, carry):
    tr = ri_ref.shape[0]

    @pl.when(pl.program_id(0) == 0)
    def _():
        carry[...] = jnp.zeros_like(carry)

    ri = ri_ref[...]
    lane = _iota((tr, LANES), 1)
    oh0 = lane == ri[:, 0:1]
    oh1 = lane == ri[:, 1:2]
    oh = jnp.where(oh0 | oh1, 1.0, 0.0)
    earlier = jnp.where(_iota((tr, tr), 1) < _iota((tr, tr), 0), 1.0, 0.0).astype(BF16)
    cnt = jnp.dot(earlier, oh.astype(BF16), preferred_element_type=F32) + carry[...]
    r0 = jnp.sum(jnp.where(oh0, cnt, 0.0), axis=-1, keepdims=True)
    r1 = jnp.sum(jnp.where(oh1, cnt, 0.0), axis=-1, keepdims=True)
    rank_ref[...] = jnp.where(lane == 0, r0, jnp.where(lane == 1, r1, 0.0)).astype(jnp.int32)
    carry[...] = carry[...] + jnp.sum(oh, axis=0, keepdims=True)
    cnt_ref[...] = carry[...].astype(jnp.int32)


def _expert_ranks(ri, tr=512):
    n = ri.shape[0]
    tr = min(tr, n)
    return pl.pallas_call(
        _rank_body,
        grid=(n // tr,),
        in_specs=[pl.BlockSpec((tr, LANES), lambda i: (i, 0))],
        out_specs=[pl.BlockSpec((tr, LANES), lambda i: (i, 0)), pl.BlockSpec((1, LANES), lambda i: (0, 0))],
        out_shape=[jax.ShapeDtypeStruct((n, LANES), jnp.int32), jax.ShapeDtypeStruct((1, LANES), jnp.int32)],
        scratch_shapes=[pltpu.VMEM((1, LANES), F32)],
        compiler_params=_cparams("arbitrary"),
        name="expert_ranks",
    )(ri)


def _padded_starts(cnt_row):
    cnt = jnp.broadcast_to(cnt_row, (8, LANES))
    padded = (((cnt + (MOE_BLOCK - 1)) // MOE_BLOCK) * MOE_BLOCK).astype(F32)
    before = jnp.where(_iota((LANES, LANES), 0) < _iota((LANES, LANES), 1), 1.0, 0.0).astype(F32)
    start = jnp.dot(padded, before, precision=HIGHEST, preferred_element_type=F32)
    return start, start + padded


def _dest_body(ri_ref, rank_ref, cnt_ref, dest_ref):
    tr = ri_ref.shape[0]
    start, _ = _padded_starts(cnt_ref[...])
    start = start[0:1, :]
    ri = ri_ref[...]
    rank = rank_ref[...]
    lane = _iota((tr, LANES), 1)
    s0 = jnp.sum(jnp.where(lane == ri[:, 0:1], start, 0.0), axis=-1, keepdims=True).astype(jnp.int32)
    s1 = jnp.sum(jnp.where(lane == ri[:, 1:2], start, 0.0), axis=-1, keepdims=True).astype(jnp.int32)
    dest_ref[...] = jnp.where(lane == 0, s0 + rank[:, 0:1], jnp.where(lane == 1, s1 + rank[:, 1:2], 0))


def _destinations(ri, rank, cnt, tr=512):
    n = ri.shape[0]
    tr = min(tr, n)
    rows = pl.BlockSpec((tr, LANES), lambda i: (i, 0))
    return pl.pallas_call(
        _dest_body,
        grid=(n // tr,),
        in_specs=[rows, rows, pl.BlockSpec((1, LANES), lambda i: (0, 0))],
        out_specs=rows,
        out_shape=jax.ShapeDtypeStruct((n, LANES), jnp.int32),
        compiler_params=_cparams("parallel"),
        name="moe_destinations",
    )(ri, rank, cnt)


def _blkexp_body(cnt_ref, be_ref):
    nbp = be_ref.shape[0]
    _, end = _padded_starts(cnt_ref[...])
    end = end[0:1, :]
    lane = _iota((nbp, LANES), 1)
    row_start = (_iota((nbp, LANES), 0) * MOE_BLOCK).astype(F32)
    done = jnp.where((lane < MOE_EXPERTS) & (end <= row_start), 1.0, 0.0)
    be = jnp.minimum(jnp.sum(done, axis=-1, keepdims=True), MOE_EXPERTS - 1.0)
    be_ref[...] = jnp.broadcast_to(be, (nbp, LANES)).astype(jnp.int32)


def _block_experts(cnt, n_blocks):
    nbp = -(-n_blocks // 8) * 8
    out = pl.pallas_call(
        _blkexp_body,
        grid=(1,),
        in_specs=[pl.BlockSpec((1, LANES), lambda i: (0, 0))],
        out_specs=pl.BlockSpec((nbp, LANES), lambda i: (0, 0)),
        out_shape=jax.ShapeDtypeStruct((nbp, LANES), jnp.int32),
        name="moe_block_experts",
    )(cnt)
    return out[:n_blocks, 0]


def _scatter_body(d0_ref, d1_ref, h_ref, xz_hbm, xb_hbm, sem):
    del xz_hbm
    ts = h_ref.shape[0]

    def copies(t):
        src = h_ref.at[pl.ds(t, 1)]
        i, j = t // LANES, t % LANES
        return (pltpu.make_async_copy(src, xb_hbm.at[pl.ds(d0_ref[i, j], 1)], sem),
                pltpu.make_async_copy(src, xb_hbm.at[pl.ds(d1_ref[i, j], 1)], sem))

    def start(t, carry):
        for prio, cp in enumerate(copies(t)):
            cp.start(priority=prio)
        return carry

    def wait(t, carry):
        for cp in copies(t):
            cp.wait()
        return carry

    lax.fori_loop(0, ts, start, 0)
    lax.fori_loop(0, ts, wait, 0)


def _scatter_rows(h2, dest0, dest1, n_rows, ts):
    n, d = h2.shape
    smem = pl.BlockSpec((ts // LANES, LANES), lambda i: (i, 0), memory_space=pltpu.SMEM)
    anyspec = pl.BlockSpec(memory_space=pl.ANY)
    return pl.pallas_call(
        _scatter_body,
        grid=(n // ts,),
        in_specs=[smem, smem, pl.BlockSpec((ts, d), lambda i: (i, 0)), anyspec],
        out_specs=anyspec,
        out_shape=jax.ShapeDtypeStruct((n_rows, d), F32),
        scratch_shapes=[pltpu.SemaphoreType.DMA(())],
        input_output_aliases={3: 0},
        compiler_params=_cparams("arbitrary"),
        name="moe_scatter",
    )(dest0, dest1, h2, jnp.zeros((n_rows, d), F32))


def _expert_body(be_ref, x_ref, wgu_ref, wd_ref, y_ref):
    del be_ref
    de = wd_ref.shape[1]
    gu = jnp.dot(x_ref[...].astype(BF16), wgu_ref[0], preferred_element_type=F32)
    hid = _silu(gu[:, 0:de]) * gu[:, de:]
    y_ref[...] = jnp.dot(hid.astype(BF16), wd_ref[0], preferred_element_type=F32)


def _expert_mlp(xb, blk_expert, w_gu, w_down):
    n_rows, d = xb.shape
    nb = n_rows // MOE_BLOCK
    de = w_down.shape[1]
    grid_spec = pltpu.PrefetchScalarGridSpec(
        num_scalar_prefetch=1,
        grid=(nb,),
        in_specs=[pl.BlockSpec((MOE_BLOCK, d), lambda i, be: (i, 0)),
                  pl.BlockSpec((1, d, 2 * de), lambda i, be: (be[i], 0, 0)),
                  pl.BlockSpec((1, de, d), lambda i, be: (be[i], 0, 0))],
        out_specs=pl.BlockSpec((MOE_BLOCK, d), lambda i, be: (i, 0)),
    )
    return pl.pallas_call(
        _expert_body,
        grid_spec=grid_spec,
        out_shape=jax.ShapeDtypeStruct((n_rows, d), F32),
        compiler_params=_cparams("arbitrary"),
        name="moe_experts",
    )(blk_expert, xb, w_gu, w_down)


def _combine_body(final_norm, d0_ref, d1_ref, rf_ref, x_ref, g_ref, yb_hbm, o_ref, gbuf, sem):
    ts = x_ref.shape[0]

    def copies(t):
        i, j = t // LANES, t % LANES
        return (pltpu.make_async_copy(yb_hbm.at[pl.ds(d0_ref[i, j], 1)], gbuf.at[0, pl.ds(t, 1)], sem),
                pltpu.make_async_copy(yb_hbm.at[pl.ds(d1_ref[i, j], 1)], gbuf.at[1, pl.ds(t, 1)], sem))

    def start(t, carry):
        for prio, cp in enumerate(copies(t)):
            cp.start(priority=prio)
        return carry

    def wait(t, carry):
        for cp in copies(t):
            cp.wait()
        return carry

    lax.fori_loop(0, ts, start, 0)
    lax.fori_loop(0, ts, wait, 0)
    rf = rf_ref[...]
    x = x_ref[...] + (gbuf[0] * rf[:, 0:1] + gbuf[1] * rf[:, 1:2])
    if final_norm:
        x = x * lax.rsqrt(jnp.mean(x * x, axis=-1, keepdims=True) + NORM_EPS) * g_ref[...]
    o_ref[...] = x


def _combine(x_mid, yb, dest0, dest1, rf, final_gain, ts):
    n, d = x_mid.shape
    final_norm = final_gain is not None
    gain = (final_gain if final_norm else jnp.ones((d,), F32)).reshape(1, d)
    smem = pl.BlockSpec((ts // LANES, LANES), lambda i: (i, 0), memory_space=pltpu.SMEM)
    rows = lambda w: pl.BlockSpec((ts, w), lambda i: (i, 0))
    return pl.pallas_call(
        functools.partial(_combine_body, final_norm),
        grid=(n // ts,),
        in_specs=[smem, smem, rows(LANES), rows(d), pl.BlockSpec((1, d), lambda i: (0, 0)),
                  pl.BlockSpec(memory_space=pl.ANY)],
        out_specs=rows(d),
        out_shape=jax.ShapeDtypeStruct((n, d), F32),
        scratch_shapes=[pltpu.VMEM((2, ts, d), F32), pltpu.SemaphoreType.DMA(())],
        compiler_params=_cparams("arbitrary"),
        name="moe_combine",
    )(dest0, dest1, rf, x_mid, gain, yb)


def _moe(x_mid, h2, ri, rf, w_gate, w_up, w_down, final_gain):
    n, d = x_mid.shape
    n_assign = 2 * n
    n_blocks = (n_assign + MOE_EXPERTS * (MOE_BLOCK - 1) + MOE_BLOCK - 1) // MOE_BLOCK
    ts = min(1024, n)
    rank, cnt = _expert_ranks(ri)
    dest = _destinations(ri, rank, cnt)
    blk_expert = _block_experts(cnt, n_blocks)
    dest0 = dest[:, 0].reshape(n // LANES, LANES)
    dest1 = dest[:, 1].reshape(n // LANES, LANES)
    xb = _scatter_rows(h2, dest0, dest1, n_blocks * MOE_BLOCK, ts)
    w_gu = jnp.concatenate([w_gate, w_up], axis=-1).astype(BF16)
    yb = _expert_mlp(xb, blk_expert, w_gu, w_down.astype(BF16))
    return _combine(x_mid, yb, dest0, dest1, rf, final_gain, ts)


def kernel(x, norm_mix, norm_ffn, norm_final, w_in, w_out, rwkv_mu, rwkv_w0, rwkv_w2, rwkv_a0, rwkv_a2, rwkv_g2, rwkv_k_k, rwkv_k_a, rwkv_r_k, rwkv_lnx_w, rwkv_lnx_b, gdn_conv_w, gdn_a_log, gdn_dt_bias, gdn_norm_w, lru_conv_w, lru_conv_b, lru_w_a, lru_b_a, lru_w_x, lru_b_x, lru_lambda, moe_w_group, moe_b_group, moe_w_expert, moe_b_expert, moe_w_gate, moe_w_up, moe_w_down):
    bsz, seq, d = x.shape
    depth = w_in.shape[0]
    n = bsz * seq
    a_cols = rwkv_mu.shape[1]
    db = gdn_norm_w.shape[1] * GDN_HEADS
    dc = lru_conv_b.shape[1]
    n_ba = 2 * GDN_HEADS
    widths = (a_cols, 3 * db, db, 2 * dc, LANES)
    xt = x.reshape(n, d)
    for l in range(depth):
        w = w_in[l]
        b0 = a_cols
        c0 = b0 + 4 * db + n_ba
        w_cat = jnp.concatenate(
            [w[:, 0:a_cols], w[:, b0:b0 + 4 * db], w[:, c0:c0 + 2 * dc],
             w[:, b0 + 4 * db:c0], jnp.zeros((d, LANES - n_ba), F32)], axis=1).astype(BF16)
        p_a, p_qkv, p_z, p_c, p_ba = _in_proj(xt, norm_mix[l], w_cat, widths, tm=min(512, n))
        o_a = _rwkv_group(p_a, bsz, seq, rwkv_mu[l], rwkv_w0[l], rwkv_w2[l], rwkv_a0[l], rwkv_a2[l], rwkv_g2[l],
                          rwkv_k_k[l], rwkv_k_a[l], rwkv_r_k[l], rwkv_lnx_w[l], rwkv_lnx_b[l])
        o_b = _gdn_group(p_qkv, p_z, p_ba, bsz, seq, gdn_conv_w[l], gdn_a_log[l], gdn_dt_bias[l], gdn_norm_w[l])
        o_c = _lru_group(p_c, bsz, seq, lru_conv_w[l], lru_conv_b[l], lru_w_a[l], lru_b_a[l], lru_w_x[l],
                         lru_b_x[l], lru_lambda[l])
        x_mid, h2, ri, rf = _out_proj_router(xt, o_a, o_b, o_c, w_out[l], norm_ffn[l], moe_w_group[l],
                                             moe_b_group[l], moe_w_expert[l], moe_b_expert[l])
        xt = _moe(x_mid, h2, ri, rf, moe_w_gate[l], moe_w_up[l], moe_w_down[l],
                  norm_final if l == depth - 1 else None)
    return xt.reshape(bsz, seq, d)
```

```python
import functools

import jax
import jax.numpy as jnp
from jax import lax
from jax.experimental import pallas as pl
from jax.experimental.pallas import tpu as pltpu

F32 = jnp.float32
BF16 = jnp.bfloat16
HIGHEST = lax.Precision.HIGHEST

NORM_EPS = 1e-6
CONV_WIDTH = 4
CHUNK = 64
SCAN_TILE = 256
SCAN_SEQS = 1
RWKV_HEADS = 4
RWKV_GN_EPS = 64e-5
GDN_HEADS = 4
LRU_C = 8.0
MOE_GROUPS = 4
MOE_PER_GROUP = 8
MOE_EXPERTS = MOE_GROUPS * MOE_PER_GROUP
MOE_BLOCK = 256
LANES = 128
NEG_BIG = -1e30
VMEM_LIMIT = 56 * 1024 * 1024


def _cparams(*sem):
    return pltpu.CompilerParams(dimension_semantics=sem, vmem_limit_bytes=VMEM_LIMIT)


def _bdot(a, b):
    return jnp.dot(a.astype(BF16), b.astype(BF16), preferred_element_type=F32)


def _bdot_nt(a, b):
    return lax.dot_general(a.astype(BF16), b.astype(BF16), (((1,), (1,)), ((), ())),
                           preferred_element_type=F32)


def _bdot_tn(a, b):
    return lax.dot_general(a.astype(BF16), b.astype(BF16), (((0,), (0,)), ((), ())),
                           preferred_element_type=F32)


def _exact_dot(a, b01):
    hi = a.astype(BF16)
    lo = (a - hi.astype(F32)).astype(BF16)
    return (jnp.dot(hi, b01, preferred_element_type=F32)
            + jnp.dot(lo, b01, preferred_element_type=F32))


def _exact_dot_left(a01, b):
    hi = b.astype(BF16)
    lo = (b - hi.astype(F32)).astype(BF16)
    return (jnp.dot(a01, hi, preferred_element_type=F32)
            + jnp.dot(a01, lo, preferred_element_type=F32))


def _softplus(x):
    return jnp.maximum(x, 0.0) + jnp.log1p(jnp.exp(-jnp.abs(x)))


def _sigmoid(x):
    return 1.0 / (1.0 + jnp.exp(-x))


def _silu(x):
    return x * _sigmoid(x)


def _iota(shape, dim):
    return lax.broadcasted_iota(jnp.int32, shape, dim)


def _head_ones(width, head_dim):
    r = _iota((width, width), 0) // head_dim
    c = _iota((width, width), 1) // head_dim
    return jnp.where(r == c, 1.0, 0.0).astype(BF16)


def _stack_heads(x, n_heads):
    c, w = x.shape
    hd = w // n_heads
    xb = x.astype(BF16)
    lane_head = _iota((c, w), 1) // hd
    return jnp.concatenate([jnp.where(lane_head == h, xb, jnp.zeros_like(xb)) for h in range(n_heads)],
                           axis=0)


def _tri_masks(c, n_heads):
    r = _iota((c, n_heads * c), 0)
    s = _iota((c, n_heads * c), 1) % c
    return s < r, s <= r


def _unit_lower_inverses(l_mats, n_heads):
    shape = l_mats[0].shape
    c = shape[0]
    eye = jnp.where(_iota(shape, 1) % c == _iota(shape, 0), 1.0, 0.0).astype(F32)
    ps = [eye + l for l in l_mats]
    ms = [_bdot(l, _stack_heads(l, n_heads)) for l in l_mats]
    span = 2
    while span < c:
        last = span * 2 >= c
        nxt = [_bdot(p if last else jnp.concatenate([m, p], axis=0), _stack_heads(m, n_heads))
               for m, p in zip(ms, ps)]
        ps = [p + (r if last else r[c:]) for p, r in zip(ps, nxt)]
        ms = [None if last else r[0:c] for r in nxt]
        span *= 2
    return ps


def _head_block_mask(width, head_dim):
    r = _iota((width, width), 0) // head_dim
    c = _iota((width, width), 1) // head_dim
    return r == c


def _inproj_body(x_ref, g_ref, w_ref, *out_refs):
    x = x_ref[...]
    h = x * lax.rsqrt(jnp.mean(x * x, axis=-1, keepdims=True) + NORM_EPS) * g_ref[...]
    hb = h.astype(BF16)
    off = 0
    for o_ref in out_refs:
        wdt = o_ref.shape[-1]
        o_ref[...] = jnp.dot(hb, w_ref[:, off:off + wdt], preferred_element_type=F32)
        off += wdt


def _in_proj(xt, gain, w_cat, widths, tm=512):
    n, d = xt.shape
    ntot = w_cat.shape[1]
    return pl.pallas_call(
        _inproj_body,
        grid=(n // tm,),
        in_specs=[pl.BlockSpec((tm, d), lambda i: (i, 0)),
                  pl.BlockSpec((1, d), lambda i: (0, 0)),
                  pl.BlockSpec((d, ntot), lambda i: (0, 0))],
        out_specs=[pl.BlockSpec((tm, w), lambda i: (i, 0)) for w in widths],
        out_shape=[jax.ShapeDtypeStruct((n, w), F32) for w in widths],
        compiler_params=_cparams("parallel"),
        name="in_proj",
    )(xt, gain.reshape(1, d), w_cat)


def _per_sequence(one_sequence):
    def body(*refs):
        n_seq = refs[0].shape[0]
        xbuf, state = refs[-2:]

        @pl.when(pl.program_id(1) == 0)
        def _():
            xbuf[:, 0:8, :] = jnp.zeros((n_seq, 8, xbuf.shape[2]), F32)
            state[...] = jnp.zeros_like(state)

        for s in range(n_seq):
            one_sequence(*[r.at[s] if len(r.shape) == 3 else r for r in refs])
    return body


def _rwkv_one(p_ref, mu_ref, w0_ref, a0_ref, kk_ref, ka_ref, rk_ref, lnw_ref, lnb_ref,
              w2_ref, a2_ref, g2_ref, o_ref, pbuf, state):
    c = CHUNK
    tt, da = o_ref.shape
    nh = RWKV_HEADS
    hd = da // nh

    pbuf[8:8 + tt, :] = p_ref[...]
    p = pbuf[8:8 + tt, :]
    p_prev = pbuf[7:7 + tt, :]
    pbuf[0:8, :] = pbuf[tt:tt + 8, :]
    xs = p + (p_prev - p) * mu_ref[...]
    r = xs[:, 0:da]
    k = xs[:, da:2 * da]
    v = xs[:, 2 * da:3 * da]
    lo = xs[:, 3 * da:]

    w_log = -_softplus(-(w0_ref[...] + _bdot(jnp.tanh(lo), w2_ref[...]))) - 0.5
    lw = -jnp.exp(w_log)
    a = _sigmoid(a0_ref[...] + _bdot(lo, a2_ref[...]))
    g = _bdot(_sigmoid(lo), g2_ref[...])

    ones_h = _head_ones(da, hd)
    kkr = k * kk_ref[...]
    kk = kkr * lax.rsqrt(_exact_dot(kkr * kkr, ones_h) + 1e-6)
    k2 = k * (1.0 + (a - 1.0) * ka_ref[...])
    alpha = -(kk * a)

    rr = _iota((tt, tt), 0)
    cc = _iota((tt, tt), 1)
    tri = jnp.where((cc <= rr) & (cc // c == rr // c), 1.0, 0.0).astype(BF16)
    cum = _exact_dot_left(tri, lw)
    n_chunks = tt // c
    chunk_rows = [slice(g * c, (g + 1) * c) for g in range(n_chunks)]
    cum_ends = [cum[(g + 1) * c - 1:(g + 1) * c, :] for g in range(n_chunks)]
    cum_last = jnp.concatenate([jnp.broadcast_to(ce, (c, da)) for ce in cum_ends], axis=0)
    e_pos = jnp.exp(cum)
    e_neg = jnp.exp(-cum)
    e_tail = jnp.exp(cum_last - cum)
    r_t = r * e_pos
    b_t = kk * jnp.exp(cum - lw)
    a_t = alpha * e_neg
    k_t = k2 * e_neg
    a_end = alpha * e_tail
    k_end = k2 * e_tail

    nl = nh * c
    strict, incl = _tri_masks(c, nh)
    grams = [_bdot_nt(jnp.concatenate([b_t[rows], r_t[rows]], axis=0),
                      jnp.concatenate([_stack_heads(a_t[rows], nh), _stack_heads(k_t[rows], nh)], axis=0))
             for rows in chunk_rows]
    t_invs = _unit_lower_inverses([jnp.where(strict, gm[0:c, 0:nl], 0.0) for gm in grams], nh)
    vss = [_stack_heads(v[rows], nh) for rows in chunk_rows]
    lkvs = [_bdot(jnp.where(strict, gm[0:c, nl:], 0.0), vs) for gm, vs in zip(grams, vss)]
    wus = [_bdot(t_inv, jnp.concatenate([_stack_heads(b_t[rows], nh), _stack_heads(lkv, nh)], axis=1))
           for t_inv, lkv, rows in zip(t_invs, lkvs, chunk_rows)]
    s_aks = [jnp.concatenate([jnp.where(incl, gm[c:, 0:nl], 0.0), jnp.where(incl, gm[c:, nl:], 0.0)], axis=1)
             for gm in grams]

    s_cur = state[...]
    block_diag = _head_block_mask(da, hd)
    outs = []
    for rows, ce, wu, s_ak, vs in zip(chunk_rows, cum_ends, wus, s_aks, vss):
        w_mat, u0 = wu[:, 0:da], wu[:, da:]
        ws_rs = _bdot_nt(jnp.concatenate([w_mat, r_t[rows]], axis=0), s_cur)
        u = u0 + ws_rs[0:c]
        outs.append(ws_rs[c:] + _bdot(s_ak, jnp.concatenate([_stack_heads(u, nh), vs], axis=0)))
        upd = _bdot_tn(jnp.concatenate([u, v[rows]], axis=0),
                       jnp.concatenate([a_end[rows], k_end[rows]], axis=0))
        s_cur = s_cur * jnp.exp(ce) + jnp.where(block_diag, upd, 0.0)
    state[...] = s_cur
    o = jnp.concatenate(outs, axis=0)

    inv_hd = 1.0 / hd
    mean = _exact_dot(o, ones_h) * inv_hd
    cen = o - mean
    var = _exact_dot(cen * cen, ones_h) * inv_hd
    o_n = cen * lax.rsqrt(var + RWKV_GN_EPS) * lnw_ref[...] + lnb_ref[...]
    bonus = _exact_dot(r * k2 * rk_ref[...], ones_h) * v
    o_ref[...] = (o_n + bonus) * g


def _rwkv_group(p_a, bsz, seq, mu, w0, w2, a0, a2, g2, k_k, k_a, r_k, lnx_w, lnx_b, tt=SCAN_TILE):
    n, a_cols = p_a.shape
    da = w0.shape[-1]
    n_lora = a_cols - 3 * da
    tt = min(tt, seq)
    nchunk = seq // tt
    d_dec, d_aaa = w2.shape[0], a2.shape[0]
    w2p = jnp.zeros((n_lora, da), F32).at[0:d_dec].set(w2).astype(BF16)
    a2p = jnp.zeros((n_lora, da), F32).at[d_dec:d_dec + d_aaa].set(a2).astype(BF16)
    g2p = jnp.zeros((n_lora, da), F32).at[d_dec + d_aaa:].set(g2).astype(BF16)
    row = lambda t: t.reshape(1, -1).astype(F32)
    vec = lambda w: pl.BlockSpec((1, w), lambda b, c: (0, 0))
    mat = pl.BlockSpec((n_lora, da), lambda b, c: (0, 0))
    nb = min(SCAN_SEQS, bsz)
    seqs = lambda w: pl.BlockSpec((nb, tt, w), lambda b, c: (b, c, 0))
    out = pl.pallas_call(
        _per_sequence(_rwkv_one),
        grid=(bsz // nb, nchunk),
        in_specs=[seqs(a_cols), vec(a_cols)] + [vec(da)] * 7 + [mat] * 3,
        out_specs=seqs(da),
        out_shape=jax.ShapeDtypeStruct((bsz, seq, da), F32),
        scratch_shapes=[pltpu.VMEM((nb, tt + 8, a_cols), F32), pltpu.VMEM((nb, da, da), F32)],
        compiler_params=_cparams("parallel", "arbitrary"),
        name="rwkv7",
    )(p_a.reshape(bsz, seq, a_cols), row(mu), row(w0), row(a0), row(k_k), row(k_a), row(r_k), row(lnx_w),
      row(lnx_b), w2p, a2p, g2p)
    return out.reshape(n, da)


def _gdn_one(qkv_ref, z_ref, ba_ref, cw_ref, alog_ref, dtb_ref, nw_ref, o_ref, xbuf, state):
    c = CHUNK
    tt, db = o_ref.shape
    nh = GDN_HEADS
    hd = db // nh
    nl = nh * c

    xbuf[8:8 + tt, :] = qkv_ref[...]
    conv = cw_ref[0:1, :] * xbuf[5:5 + tt, :]
    for j in range(1, CONV_WIDTH):
        conv = conv + cw_ref[j:j + 1, :] * xbuf[5 + j:5 + j + tt, :]
    xbuf[0:8, :] = xbuf[tt:tt + 8, :]
    act = _silu(conv)
    q = act[:, 0:db]
    k = act[:, db:2 * db]
    v = act[:, 2 * db:]

    def per_head(col_of_head):
        return jnp.concatenate([jnp.broadcast_to(col_of_head(h), (tt, hd)) for h in range(nh)], axis=1)

    def head_sumsq(x):
        return per_head(lambda h: jnp.sum(x[:, h * hd:(h + 1) * hd] * x[:, h * hd:(h + 1) * hd],
                                          axis=-1, keepdims=True))

    q = q * lax.rsqrt(head_sumsq(q) + 1e-6) * (hd ** -0.5)
    k = k * lax.rsqrt(head_sumsq(k) + 1e-6)

    ba = ba_ref[...]
    beta_l = _sigmoid(ba)
    g_l = -jnp.exp(alog_ref[...]) * _softplus(ba + dtb_ref[...])
    row_in_chunk = _iota((tt, LANES), 0) % c
    gcum_l = g_l
    d = 1
    while d < c:
        gcum_l = gcum_l + jnp.where(row_in_chunk >= d, pltpu.roll(gcum_l, d, 0), 0.0)
        d *= 2
    n_chunks = tt // c
    chunk_rows = [slice(g * c, (g + 1) * c) for g in range(n_chunks)]
    g_last_l = jnp.concatenate([jnp.broadcast_to(gcum_l[(g + 1) * c - 1:(g + 1) * c, :], (c, LANES))
                                for g in range(n_chunks)], axis=0)
    e_g_l = jnp.exp(gcum_l)
    e_tail_l = jnp.exp(g_last_l - gcum_l)
    beta = per_head(lambda h: beta_l[:, h:h + 1])
    e_g = per_head(lambda h: e_g_l[:, nh + h:nh + h + 1])
    e_tail = per_head(lambda h: e_tail_l[:, nh + h:nh + h + 1])
    kb = k * beta
    vb = v * beta
    kbe = kb * e_g
    qe = q * e_g
    k_end = k * e_tail

    strict, incl = _tri_masks(c, nh)
    lane_head = _iota((c, nl), 1) // c
    on_diag = (_iota((c, nl), 1) % c) == _iota((c, nl), 0)
    dmats = []
    for rows in chunk_rows:
        gc = gcum_l[rows]
        gcol = jnp.broadcast_to(gc[:, nh:nh + 1], (c, nl))
        for h in range(1, nh):
            gcol = jnp.where(lane_head == h, jnp.broadcast_to(gc[:, nh + h:nh + h + 1], (c, nl)), gcol)
        grow = jnp.sum(jnp.where(on_diag, gcol, 0.0), axis=0, keepdims=True)
        dmats.append(jnp.exp(jnp.where(incl, gcol - grow, NEG_BIG)))
    grams = [_bdot_nt(jnp.concatenate([kb[rows], q[rows]], axis=0), _stack_heads(k[rows], nh))
             for rows in chunk_rows]
    t_invs = _unit_lower_inverses([jnp.where(strict, -(gm[0:c] * dm), 0.0) for gm, dm in zip(grams, dmats)], nh)
    intras = [jnp.where(incl, gm[c:] * dm, 0.0) for gm, dm in zip(grams, dmats)]
    sols = [_bdot(t_inv, jnp.concatenate([_stack_heads(vb[rows], nh), _stack_heads(kbe[rows], nh)], axis=1))
            for t_inv, rows in zip(t_invs, chunk_rows)]

    s_cur = state[...]
    block_diag = _head_block_mask(db, hd)
    outs = []
    for g, (rows, sol, intra) in enumerate(zip(chunk_rows, sols, intras)):
        u_s, w_s = sol[:, 0:db], sol[:, db:]
        wq = _bdot(jnp.concatenate([w_s, qe[rows]], axis=0), s_cur)
        v_new = u_s - wq[0:c]
        outs.append(wq[c:] + _bdot(intra, _stack_heads(v_new, nh)))
        upd = _bdot_tn(k_end[rows], v_new)
        s_cur = s_cur * e_g[(g + 1) * c - 1:(g + 1) * c, :] + jnp.where(block_diag, upd, 0.0)
    state[...] = s_cur
    o = jnp.concatenate(outs, axis=0)
    o = o * lax.rsqrt(head_sumsq(o) * (1.0 / hd) + NORM_EPS) * nw_ref[...]
    o_ref[...] = o * _silu(z_ref[...])


def _gdn_group(qkv, z, ba, bsz, seq, conv_w, a_log, dt_bias, norm_w, tt=SCAN_TILE):
    n, w3 = qkv.shape
    db = z.shape[-1]
    nh = GDN_HEADS
    tt = min(tt, seq)
    nchunk = seq // tt
    alog_row = jnp.zeros((1, LANES), F32).at[0, nh:2 * nh].set(a_log)
    dtb_row = jnp.zeros((1, LANES), F32).at[0, nh:2 * nh].set(dt_bias)
    nw_row = jnp.tile(norm_w, nh).reshape(1, db)
    nb = min(SCAN_SEQS, bsz)
    seqs = lambda w: pl.BlockSpec((nb, tt, w), lambda b, c: (b, c, 0))
    const = lambda r, w: pl.BlockSpec((r, w), lambda b, c: (0, 0))
    out = pl.pallas_call(
        _per_sequence(_gdn_one),
        grid=(bsz // nb, nchunk),
        in_specs=[seqs(w3), seqs(db), seqs(LANES), const(CONV_WIDTH, w3), const(1, LANES), const(1, LANES),
                  const(1, db)],
        out_specs=seqs(db),
        out_shape=jax.ShapeDtypeStruct((bsz, seq, db), F32),
        scratch_shapes=[pltpu.VMEM((nb, tt + 8, w3), F32), pltpu.VMEM((nb, db, db), F32)],
        compiler_params=_cparams("parallel", "arbitrary"),
        name="gdn",
    )(qkv.reshape(bsz, seq, w3), z.reshape(bsz, seq, db), ba.reshape(bsz, seq, LANES), conv_w, alog_row,
      dtb_row, nw_row)
    return out.reshape(n, db)


def _gdn_group_from_proj(p_b, conv_w, a_log, dt_bias, norm_w):
    bsz, seq, _ = p_b.shape
    db = norm_w.shape[0] * GDN_HEADS
    flat = p_b.reshape(bsz * seq, -1)
    qkv, z, ba = flat[:, :3 * db], flat[:, 3 * db:4 * db], flat[:, 4 * db:]
    ba = jnp.pad(ba, ((0, 0), (0, LANES - ba.shape[1])))
    return _gdn_group(qkv, z, ba, bsz, seq, conv_w, a_log, dt_bias, norm_w).reshape(bsz, seq, db)


def _lru_body(p_ref, cw_ref, cb_ref, wax_ref, ba_ref, bx_ref, lam_ref, o_ref, xbuf, hcar):
    tt = p_ref.shape[0]
    dc = o_ref.shape[-1]

    @pl.when(pl.program_id(1) == 0)
    def _():
        xbuf[0:8, :] = jnp.zeros((8, dc), F32)
        hcar[...] = jnp.zeros_like(hcar)

    xbuf[8:8 + tt, :] = p_ref[:, 0:dc]
    gate = p_ref[:, dc:]
    conv = cb_ref[...] + cw_ref[0:1, :] * xbuf[5:5 + tt, :]
    for j in range(1, CONV_WIDTH):
        conv = conv + cw_ref[j:j + 1, :] * xbuf[5 + j:5 + j + tt, :]
    xbuf[0:8, :] = xbuf[tt:tt + 8, :]

    ri = _bdot(conv, wax_ref[...])
    r = _sigmoid(ri[:, 0:dc] + ba_ref[...])
    i = _sigmoid(ri[:, dc:] + bx_ref[...])
    log_a = -LRU_C * r * _softplus(-lam_ref[...])
    a = jnp.exp(log_a)
    mult = jnp.sqrt(jnp.tanh(-log_a) * (a * a + 1.0))
    row = _iota((tt, dc), 0)
    mult = jnp.where((row == 0) & (pl.program_id(1) == 0), 1.0, mult)
    bv = conv * i * mult

    av = a
    d = 1
    while d < tt:
        a_sh = jnp.where(row >= d, pltpu.roll(av, d, 0), 1.0)
        b_sh = jnp.where(row >= d, pltpu.roll(bv, d, 0), 0.0)
        bv = av * b_sh + bv
        av = av * a_sh
        d *= 2
    h = av * hcar[...] + bv
    hcar[...] = h[tt - 1:tt, :]
    gelu = 0.5 * gate * (1.0 + jnp.tanh(0.7978845608028654 * (gate + 0.044715 * gate * gate * gate)))
    o_ref[...] = h * gelu


def _block_diag(w):
    nb, bi, bo = w.shape
    out = jnp.zeros((nb * bi, nb * bo), w.dtype)
    for b in range(nb):
        out = out.at[b * bi:(b + 1) * bi, b * bo:(b + 1) * bo].set(w[b])
    return out


def _lru_group(p_c, bsz, seq, conv_w, conv_b, w_a, b_a, w_x, b_x, lam, tt=256):
    n, c_cols = p_c.shape
    dc = c_cols // 2
    tt = min(tt, seq)
    nt = seq // tt
    wax = jnp.concatenate([_block_diag(w_a), _block_diag(w_x)], axis=1).astype(BF16)
    row = lambda t: t.reshape(1, -1).astype(F32)
    const = lambda r, w: pl.BlockSpec((r, w), lambda b, c: (0, 0))
    return pl.pallas_call(
        _lru_body,
        grid=(bsz, nt),
        in_specs=[pl.BlockSpec((tt, c_cols), lambda b, c: (b * nt + c, 0)),
                  const(CONV_WIDTH, dc), const(1, dc), const(dc, 2 * dc), const(1, dc), const(1, dc), const(1, dc)],
        out_specs=pl.BlockSpec((tt, dc), lambda b, c: (b * nt + c, 0)),
        out_shape=jax.ShapeDtypeStruct((n, dc), F32),
        scratch_shapes=[pltpu.VMEM((tt + 8, dc), F32), pltpu.VMEM((1, dc), F32)],
        compiler_params=_cparams("parallel", "arbitrary"),
        name="rglru",
    )(p_c, conv_w, row(conv_b), wax, row(b_a), row(b_x), row(lam))


def _outproj_body(x_ref, ma_ref, mb_ref, mc_ref, wo_ref, g_ref, wrh_ref, wrl_ref, br_ref,
                  xo_ref, h_ref, ri_ref, rf_ref):
    da = ma_ref.shape[-1]
    db = mb_ref.shape[-1]
    tm = x_ref.shape[0]
    mix = (jnp.dot(ma_ref[...].astype(BF16), wo_ref[0:da, :], preferred_element_type=F32)
           + jnp.dot(mb_ref[...].astype(BF16), wo_ref[da:da + db, :], preferred_element_type=F32)
           + jnp.dot(mc_ref[...].astype(BF16), wo_ref[da + db:, :], preferred_element_type=F32))
    x = x_ref[...] + mix
    xo_ref[...] = x
    h = x * lax.rsqrt(jnp.mean(x * x, axis=-1, keepdims=True) + NORM_EPS) * g_ref[...]
    h_ref[...] = h

    h_hi = h.astype(BF16)
    h_lo = (h - h_hi.astype(F32)).astype(BF16)
    logits = (jnp.dot(h_hi, wrh_ref[...], preferred_element_type=F32)
              + jnp.dot(h_lo, wrh_ref[...], preferred_element_type=F32)
              + jnp.dot(h_hi, wrl_ref[...], preferred_element_type=F32)) + br_ref[...]
    lane = _iota((tm, LANES), 1)
    far = 4 * LANES
    gmask = lane < MOE_GROUPS
    gl = jnp.where(gmask, logits, NEG_BIG)
    gmax = jnp.max(gl, axis=-1, keepdims=True)
    gidx = jnp.min(jnp.where(gl == gmax, lane, far), axis=-1, keepdims=True)
    gsum = jnp.sum(jnp.where(gmask, jnp.exp(gl - gmax), 0.0), axis=-1, keepdims=True)
    grp_w = 1.0 / gsum
    eidx = lane - MOE_GROUPS
    emask = (eidx >= 0) & (eidx < MOE_EXPERTS) & ((eidx // MOE_PER_GROUP) == gidx)
    el = jnp.where(emask, logits, NEG_BIG)
    v1 = jnp.max(el, axis=-1, keepdims=True)
    i1 = jnp.min(jnp.where(el == v1, lane, far), axis=-1, keepdims=True)
    el2 = jnp.where(lane == i1, NEG_BIG, el)
    v2 = jnp.max(el2, axis=-1, keepdims=True)
    i2 = jnp.min(jnp.where(el2 == v2, lane, far), axis=-1, keepdims=True)
    pr = jnp.exp(v2 - v1)
    g0 = grp_w / (1.0 + pr)
    g1 = g0 * pr
    ri_ref[...] = jnp.where(lane == 0, i1 - MOE_GROUPS, jnp.where(lane == 1, i2 - MOE_GROUPS, 0))
    rf_ref[...] = jnp.where(lane == 0, g0, jnp.where(lane == 1, g1, 0.0))


def _out_proj_router(xt, o_a, o_b, o_c, w_out, gain, w_group, b_group, w_expert, b_expert, tm=512):
    n, d = xt.shape
    tm = min(tm, n)
    ng, ne = w_group.shape[1], w_expert.shape[1]
    wr = jnp.zeros((d, LANES), F32).at[:, 0:ng].set(w_group).at[:, ng:ng + ne].set(w_expert)
    wr_hi = wr.astype(BF16)
    wr_lo = (wr - wr_hi.astype(F32)).astype(BF16)
    br = jnp.zeros((1, LANES), F32).at[0, 0:ng].set(b_group).at[0, ng:ng + ne].set(b_expert)
    rows = lambda w: pl.BlockSpec((tm, w), lambda i: (i, 0))
    const = lambda r, w: pl.BlockSpec((r, w), lambda i: (0, 0))
    return pl.pallas_call(
        _outproj_body,
        grid=(n // tm,),
        in_specs=[rows(d), rows(o_a.shape[1]), rows(o_b.shape[1]), rows(o_c.shape[1]),
                  const(d, d), const(1, d), const(d, LANES), const(d, LANES), const(1, LANES)],
        out_specs=[rows(d), rows(d), rows(LANES), rows(LANES)],
        out_shape=[jax.ShapeDtypeStruct((n, d), F32), jax.ShapeDtypeStruct((n, d), F32),
                   jax.ShapeDtypeStruct((n, LANES), jnp.int32), jax.ShapeDtypeStruct((n, LANES), F32)],
        compiler_params=_cparams("parallel"),
        name="out_proj_router",
    )(xt, o_a, o_b, o_c, w_out.astype(BF16), gain.reshape(1, d), wr_hi, wr_lo, br)


def _rank_body(ri_ref, rank_ref, cnt_ref, carry):
    tr = ri_ref.shape[0]

    @pl.when(pl.program_id(0) == 0)
    def _():
        carry[...] = jnp.zeros_like(carry)

    ri = ri_ref[...]
    lane = _iota((tr, LANES), 1)
    oh0 = lane == ri[:, 0:1]
    oh1 = lane == ri[:, 1:2]
    oh = jnp.where(oh0 | oh1, 1.0, 0.0)
    earlier = jnp.where(_iota((tr, tr), 1) < _iota((tr, tr), 0), 1.0, 0.0).astype(BF16)
    cnt = jnp.dot(earlier, oh.astype(BF16), preferred_element_type=F32) + carry[...]
    r0 = jnp.sum(jnp.where(oh0, cnt, 0.0), axis=-1, keepdims=True)
    r1 = jnp.sum(jnp.where(oh1, cnt, 0.0), axis=-1, keepdims=True)
    rank_ref[...] = jnp.where(lane == 0, r0, jnp.where(lane == 1, r1, 0.0)).astype(jnp.int32)
    carry[...] = carry[...] + jnp.sum(oh, axis=0, keepdims=True)
    cnt_ref[...] = carry[...].astype(jnp.int32)


def _expert_ranks(ri, tr=512):
    n = ri.shape[0]
    tr = min(tr, n)
    return pl.pallas_call(
        _rank_body,
        grid=(n // tr,),
        in_specs=[pl.BlockSpec((tr, LANES), lambda i: (i, 0))],
        out_specs=[pl.BlockSpec((tr, LANES), lambda i: (i, 0)), pl.BlockSpec((1, LANES), lambda i: (0, 0))],
        out_shape=[jax.ShapeDtypeStruct((n, LANES), jnp.int32), jax.ShapeDtypeStruct((1, LANES), jnp.int32)],
        scratch_shapes=[pltpu.VMEM((1, LANES), F32)],
        compiler_params=_cparams("arbitrary"),
        name="expert_ranks",
    )(ri)


def _padded_starts(cnt_row):
    cnt = jnp.broadcast_to(cnt_row, (8, LANES))
    padded = (((cnt + (MOE_BLOCK - 1)) // MOE_BLOCK) * MOE_BLOCK).astype(F32)
    before = jnp.where(_iota((LANES, LANES), 0) < _iota((LANES, LANES), 1), 1.0, 0.0).astype(F32)
    start = jnp.dot(padded, before, precision=HIGHEST, preferred_element_type=F32)
    return start, start + padded


def _dest_body(ri_ref, rank_ref, cnt_ref, dest_ref):
    tr = ri_ref.shape[0]
    start, _ = _padded_starts(cnt_ref[...])
    start = start[0:1, :]
    ri = ri_ref[...]
    rank = rank_ref[...]
    lane = _iota((tr, LANES), 1)
    s0 = jnp.sum(jnp.where(lane == ri[:, 0:1], start, 0.0), axis=-1, keepdims=True).astype(jnp.int32)
    s1 = jnp.sum(jnp.where(lane == ri[:, 1:2], start, 0.0), axis=-1, keepdims=True).astype(jnp.int32)
    dest_ref[...] = jnp.where(lane == 0, s0 + rank[:, 0:1], jnp.where(lane == 1, s1 + rank[:, 1:2], 0))


def _destinations(ri, rank, cnt, tr=512):
    n = ri.shape[0]
    tr = min(tr, n)
    rows = pl.BlockSpec((tr, LANES), lambda i: (i, 0))
    return pl.pallas_call(
        _dest_body,
        grid=(n // tr,),
        in_specs=[rows, rows, pl.BlockSpec((1, LANES), lambda i: (0, 0))],
        out_specs=rows,
        out_shape=jax.ShapeDtypeStruct((n, LANES), jnp.int32),
        compiler_params=_cparams("parallel"),
        name="moe_destinations",
    )(ri, rank, cnt)


def _blkexp_body(cnt_ref, be_ref):
    nbp = be_ref.shape[0]
    _, end = _padded_starts(cnt_ref[...])
    end = end[0:1, :]
    lane = _iota((nbp, LANES), 1)
    row_start = (_iota((nbp, LANES), 0) * MOE_BLOCK).astype(F32)
    done = jnp.where((lane < MOE_EXPERTS) & (end <= row_start), 1.0, 0.0)
    be = jnp.minimum(jnp.sum(done, axis=-1, keepdims=True), MOE_EXPERTS - 1.0)
    be_ref[...] = jnp.broadcast_to(be, (nbp, LANES)).astype(jnp.int32)


def _block_experts(cnt, n_blocks):
    nbp = -(-n_blocks // 8) * 8
    out = pl.pallas_call(
        _blkexp_body,
        grid=(1,),
        in_specs=[pl.BlockSpec((1, LANES), lambda i: (0, 0))],
        out_specs=pl.BlockSpec((nbp, LANES), lambda i: (0, 0)),
        out_shape=jax.ShapeDtypeStruct((nbp, LANES), jnp.int32),
        name="moe_block_experts",
    )(cnt)
    return out[:n_blocks, 0]


def _for_each_row(n_rows, fn):
    def lane_row(i, carry):
        for j in range(LANES):
            fn(i * LANES + j, i, j)
        return carry

    lax.fori_loop(0, n_rows // LANES, lane_row, 0)


def _scatter_body(d0_ref, d1_ref, h_ref, xz_hbm, xb_hbm, sem):
    del xz_hbm
    ts = h_ref.shape[0]

    def copies(t, i, j):
        src = h_ref.at[pl.ds(t, 1)]
        return (pltpu.make_async_copy(src, xb_hbm.at[pl.ds(d0_ref[i, j], 1)], sem),
                pltpu.make_async_copy(src, xb_hbm.at[pl.ds(d1_ref[i, j], 1)], sem))

    def start(t, i, j):
        for prio, cp in enumerate(copies(t, i, j)):
            cp.start(priority=prio)

    def wait(t, i, j):
        for cp in copies(t, i, j):
            cp.wait()

    _for_each_row(ts, start)
    _for_each_row(ts, wait)


def _scatter_rows(h2, dest0, dest1, n_rows, ts):
    n, d = h2.shape
    smem = pl.BlockSpec((ts // LANES, LANES), lambda i: (i, 0), memory_space=pltpu.SMEM)
    anyspec = pl.BlockSpec(memory_space=pl.ANY)
    return pl.pallas_call(
        _scatter_body,
        grid=(n // ts,),
        in_specs=[smem, smem, pl.BlockSpec((ts, d), lambda i: (i, 0)), anyspec],
        out_specs=anyspec,
        out_shape=jax.ShapeDtypeStruct((n_rows, d), F32),
        scratch_shapes=[pltpu.SemaphoreType.DMA(())],
        input_output_aliases={3: 0},
        compiler_params=_cparams("arbitrary"),
        name="moe_scatter",
    )(dest0, dest1, h2, jnp.zeros((n_rows, d), F32))


def _expert_body(be_ref, x_ref, wgu_ref, wd_ref, y_ref):
    del be_ref
    de = wd_ref.shape[1]
    gu = jnp.dot(x_ref[...].astype(BF16), wgu_ref[0], preferred_element_type=F32)
    hid = _silu(gu[:, 0:de]) * gu[:, de:]
    y_ref[...] = jnp.dot(hid.astype(BF16), wd_ref[0], preferred_element_type=F32)


def _expert_mlp(xb, blk_expert, w_gu, w_down):
    n_rows, d = xb.shape
    nb = n_rows // MOE_BLOCK
    de = w_down.shape[1]
    grid_spec = pltpu.PrefetchScalarGridSpec(
        num_scalar_prefetch=1,
        grid=(nb,),
        in_specs=[pl.BlockSpec((MOE_BLOCK, d), lambda i, be: (i, 0)),
                  pl.BlockSpec((1, d, 2 * de), lambda i, be: (be[i], 0, 0)),
                  pl.BlockSpec((1, de, d), lambda i, be: (be[i], 0, 0))],
        out_specs=pl.BlockSpec((MOE_BLOCK, d), lambda i, be: (i, 0)),
    )
    return pl.pallas_call(
        _expert_body,
        grid_spec=grid_spec,
        out_shape=jax.ShapeDtypeStruct((n_rows, d), F32),
        compiler_params=_cparams("arbitrary"),
        name="moe_experts",
    )(blk_expert, xb, w_gu, w_down)


def _combine_body(final_norm, d0_ref, d1_ref, rf_ref, x_ref, g_ref, yb_hbm, o_ref, gbuf, sem):
    ts = x_ref.shape[0]

    def copies(t, i, j):
        return (pltpu.make_async_copy(yb_hbm.at[pl.ds(d0_ref[i, j], 1)], gbuf.at[0, pl.ds(t, 1)], sem),
                pltpu.make_async_copy(yb_hbm.at[pl.ds(d1_ref[i, j], 1)], gbuf.at[1, pl.ds(t, 1)], sem))

    def start(t, i, j):
        for prio, cp in enumerate(copies(t, i, j)):
            cp.start(priority=prio)

    def wait(t, i, j):
        for cp in copies(t, i, j):
            cp.wait()

    _for_each_row(ts, start)
    _for_each_row(ts, wait)
    rf = rf_ref[...]
    x = x_ref[...] + (gbuf[0] * rf[:, 0:1] + gbuf[1] * rf[:, 1:2])
    if final_norm:
        x = x * lax.rsqrt(jnp.mean(x * x, axis=-1, keepdims=True) + NORM_EPS) * g_ref[...]
    o_ref[...] = x


def _combine(x_mid, yb, dest0, dest1, rf, final_gain, ts):
    n, d = x_mid.shape
    final_norm = final_gain is not None
    gain = (final_gain if final_norm else jnp.ones((d,), F32)).reshape(1, d)
    smem = pl.BlockSpec((ts // LANES, LANES), lambda i: (i, 0), memory_space=pltpu.SMEM)
    rows = lambda w: pl.BlockSpec((ts, w), lambda i: (i, 0))
    return pl.pallas_call(
        functools.partial(_combine_body, final_norm),
        grid=(n // ts,),
        in_specs=[smem, smem, rows(LANES), rows(d), pl.BlockSpec((1, d), lambda i: (0, 0)),
                  pl.BlockSpec(memory_space=pl.ANY)],
        out_specs=rows(d),
        out_shape=jax.ShapeDtypeStruct((n, d), F32),
        scratch_shapes=[pltpu.VMEM((2, ts, d), F32), pltpu.SemaphoreType.DMA(())],
        compiler_params=_cparams("arbitrary"),
        name="moe_combine",
    )(dest0, dest1, rf, x_mid, gain, yb)


def _moe(x_mid, h2, ri, rf, w_gate, w_up, w_down, final_gain):
    n, d = x_mid.shape
    n_assign = 2 * n
    n_blocks = (n_assign + MOE_EXPERTS * (MOE_BLOCK - 1) + MOE_BLOCK - 1) // MOE_BLOCK
    ts = min(1024, n)
    rank, cnt = _expert_ranks(ri)
    dest = _destinations(ri, rank, cnt)
    blk_expert = _block_experts(cnt, n_blocks)
    dest0 = dest[:, 0].reshape(n // LANES, LANES)
    dest1 = dest[:, 1].reshape(n // LANES, LANES)
    xb = _scatter_rows(h2, dest0, dest1, n_blocks * MOE_BLOCK, ts)
    w_gu = jnp.concatenate([w_gate, w_up], axis=-1).astype(BF16)
    yb = _expert_mlp(xb, blk_expert, w_gu, w_down.astype(BF16))
    return _combine(x_mid, yb, dest0, dest1, rf, final_gain, ts)


def kernel(x, norm_mix, norm_ffn, norm_final, w_in, w_out, rwkv_mu, rwkv_w0, rwkv_w2, rwkv_a0, rwkv_a2, rwkv_g2, rwkv_k_k, rwkv_k_a, rwkv_r_k, rwkv_lnx_w, rwkv_lnx_b, gdn_conv_w, gdn_a_log, gdn_dt_bias, gdn_norm_w, lru_conv_w, lru_conv_b, lru_w_a, lru_b_a, lru_w_x, lru_b_x, lru_lambda, moe_w_group, moe_b_group, moe_w_expert, moe_b_expert, moe_w_gate, moe_w_up, moe_w_down):
    bsz, seq, d = x.shape
    depth = w_in.shape[0]
    n = bsz * seq
    a_cols = rwkv_mu.shape[1]
    db = gdn_norm_w.shape[1] * GDN_HEADS
    dc = lru_conv_b.shape[1]
    n_ba = 2 * GDN_HEADS
    widths = (a_cols, 3 * db, db, 2 * dc, LANES)
    xt = x.reshape(n, d)
    for l in range(depth):
        w = w_in[l]
        b0 = a_cols
        c0 = b0 + 4 * db + n_ba
        w_cat = jnp.concatenate(
            [w[:, 0:a_cols], w[:, b0:b0 + 4 * db], w[:, c0:c0 + 2 * dc],
             w[:, b0 + 4 * db:c0], jnp.zeros((d, LANES - n_ba), F32)], axis=1).astype(BF16)
        p_a, p_qkv, p_z, p_c, p_ba = _in_proj(xt, norm_mix[l], w_cat, widths, tm=min(512, n))
        o_a = _rwkv_group(p_a, bsz, seq, rwkv_mu[l], rwkv_w0[l], rwkv_w2[l], rwkv_a0[l], rwkv_a2[l], rwkv_g2[l],
                          rwkv_k_k[l], rwkv_k_a[l], rwkv_r_k[l], rwkv_lnx_w[l], rwkv_lnx_b[l])
        o_b = _gdn_group(p_qkv, p_z, p_ba, bsz, seq, gdn_conv_w[l], gdn_a_log[l], gdn_dt_bias[l], gdn_norm_w[l])
        o_c = _lru_group(p_c, bsz, seq, lru_conv_w[l], lru_conv_b[l], lru_w_a[l], lru_b_a[l], lru_w_x[l],
                         lru_b_x[l], lru_lambda[l])
        x_mid, h2, ri, rf = _out_proj_router(xt, o_a, o_b, o_c, w_out[l], norm_ffn[l], moe_w_group[l],
                                             moe_b_group[l], moe_w_expert[l], moe_b_expert[l])
        xt = _moe(x_mid, h2, ri, rf, moe_w_gate[l], moe_w_up[l], moe_w_down[l],
                  norm_final if l == depth - 1 else None)
    return xt.reshape(bsz, seq, d)
```

```python
import functools

import jax
import jax.numpy as jnp
from jax import lax
from jax.experimental import pallas as pl
from jax.experimental.pallas import tpu as pltpu

F32 = jnp.float32
BF16 = jnp.bfloat16
HIGHEST = lax.Precision.HIGHEST

NORM_EPS = 1e-6
CONV_WIDTH = 4
CHUNK = 64
SCAN_TILE = 256
SCAN_SEQS = 1
RWKV_HEADS = 4
RWKV_GN_EPS = 64e-5
GDN_HEADS = 4
LRU_C = 8.0
MOE_GROUPS = 4
MOE_PER_GROUP = 8
MOE_EXPERTS = MOE_GROUPS * MOE_PER_GROUP
MOE_BLOCK = 256
LANES = 128
NEG_BIG = -1e30
VMEM_LIMIT = 56 * 1024 * 1024


def _cparams(*sem):
    return pltpu.CompilerParams(dimension_semantics=sem, vmem_limit_bytes=VMEM_LIMIT)


def _bdot(a, b):
    return jnp.dot(a.astype(BF16), b.astype(BF16), preferred_element_type=F32)


def _bdot_nt(a, b):
    return lax.dot_general(a.astype(BF16), b.astype(BF16), (((1,), (1,)), ((), ())),
                           preferred_element_type=F32)


def _bdot_tn(a, b):
    return lax.dot_general(a.astype(BF16), b.astype(BF16), (((0,), (0,)), ((), ())),
                           preferred_element_type=F32)


def _exact_dot(a, b01):
    hi = a.astype(BF16)
    lo = (a - hi.astype(F32)).astype(BF16)
    return (jnp.dot(hi, b01, preferred_element_type=F32)
            + jnp.dot(lo, b01, preferred_element_type=F32))


def _exact_dot_left(a01, b):
    hi = b.astype(BF16)
    lo = (b - hi.astype(F32)).astype(BF16)
    return (jnp.dot(a01, hi, preferred_element_type=F32)
            + jnp.dot(a01, lo, preferred_element_type=F32))


def _softplus(x):
    return jnp.maximum(x, 0.0) + jnp.log1p(jnp.exp(-jnp.abs(x)))


def _sigmoid(x):
    return 1.0 / (1.0 + jnp.exp(-x))


def _silu(x):
    return x * _sigmoid(x)


def _iota(shape, dim):
    return lax.broadcasted_iota(jnp.int32, shape, dim)


def _head_ones(width, head_dim):
    r = _iota((width, width), 0) // head_dim
    c = _iota((width, width), 1) // head_dim
    return jnp.where(r == c, 1.0, 0.0).astype(BF16)


def _stack_heads(x, n_heads):
    c, w = x.shape
    hd = w // n_heads
    xb = x.astype(BF16)
    lane_head = _iota((c, w), 1) // hd
    return jnp.concatenate([jnp.where(lane_head == h, xb, jnp.zeros_like(xb)) for h in range(n_heads)],
                           axis=0)


def _tri_masks(c, n_heads):
    r = _iota((c, n_heads * c), 0)
    s = _iota((c, n_heads * c), 1) % c
    return s < r, s <= r


def _unit_lower_inverses(l_mats, n_heads):
    shape = l_mats[0].shape
    c = shape[0]
    eye = jnp.where(_iota(shape, 1) % c == _iota(shape, 0), 1.0, 0.0).astype(F32)
    ps = [eye + l for l in l_mats]
    ms = [_bdot(l, _stack_heads(l, n_heads)) for l in l_mats]
    span = 2
    while span < c:
        last = span * 2 >= c
        nxt = [_bdot(p if last else jnp.concatenate([m, p], axis=0), _stack_heads(m, n_heads))
               for m, p in zip(ms, ps)]
        ps = [p + (r if last else r[c:]) for p, r in zip(ps, nxt)]
        ms = [None if last else r[0:c] for r in nxt]
        span *= 2
    return ps


def _head_block_mask(width, head_dim):
    r = _iota((width, width), 0) // head_dim
    c = _iota((width, width), 1) // head_dim
    return r == c


def _shift_rows(y, tail, j):
    rolled = pltpu.roll(y, j, 0)
    head = jnp.where(_iota(tail.shape, 0) < j, pltpu.roll(tail, j, 0), rolled[0:8])
    return jnp.concatenate([head, rolled[8:]], axis=0)


def _causal_conv(y, tail, cw):
    acc = cw[CONV_WIDTH - 1:CONV_WIDTH, :] * y
    for j in range(1, CONV_WIDTH):
        acc = acc + cw[CONV_WIDTH - 1 - j:CONV_WIDTH - j, :] * _shift_rows(y, tail, j)
    return acc


def _inproj_body(tiles_per_seq, x_ref, g_ref, w_ref, mu_ref, gcw_ref, lcw_ref, lcb_ref,
                 a_ref, qkv_ref, z_ref, c_ref, ba_ref, atail, qtail, ctail):
    tm = x_ref.shape[0]
    tails = (atail, qtail, ctail)

    @pl.when(pl.program_id(0) % tiles_per_seq == 0)
    def _():
        for tail in tails:
            tail[...] = jnp.zeros_like(tail)

    x = x_ref[...]
    h = x * lax.rsqrt(jnp.mean(x * x, axis=-1, keepdims=True) + NORM_EPS) * g_ref[...]
    hb = h.astype(BF16)
    wa, wq, wz, wc = a_ref.shape[-1], qkv_ref.shape[-1], z_ref.shape[-1], c_ref.shape[-1]
    dc = ctail.shape[1]

    def proj(off, width):
        return jnp.dot(hb, w_ref[:, off:off + width], preferred_element_type=F32)

    for s in range(3):
        cols = slice(s * (wq // 3), (s + 1) * (wq // 3))
        y = proj(wa + cols.start, wq // 3)
        qkv_ref[:, cols] = _silu(_causal_conv(y, qtail[:, cols], gcw_ref[:, cols]))
        qtail[:, cols] = y[tm - 8:tm, :]
    p = proj(0, wa)
    a_ref[...] = p + (_shift_rows(p, atail[...], 1) - p) * mu_ref[...]
    atail[...] = p[tm - 8:tm, :]
    z_ref[...] = proj(wa + wq, wz)
    pc = proj(wa + wq + wz, wc)
    c_ref[:, 0:dc] = lcb_ref[...] + _causal_conv(pc[:, 0:dc], ctail[...], lcw_ref[...])
    c_ref[:, dc:] = pc[:, dc:]
    ctail[...] = pc[tm - 8:tm, 0:dc]
    ba_ref[...] = proj(wa + wq + wz + wc, ba_ref.shape[-1])


def _in_proj(xt, seq, gain, w_cat, widths, mu, gdn_conv_w, lru_conv_w, lru_conv_b, tm=512):
    n, d = xt.shape
    ntot = w_cat.shape[1]
    tm = min(tm, seq)
    dc = lru_conv_b.shape[0]
    const = lambda r, w: pl.BlockSpec((r, w), lambda i: (0, 0))
    return pl.pallas_call(
        functools.partial(_inproj_body, seq // tm),
        grid=(n // tm,),
        in_specs=[pl.BlockSpec((tm, d), lambda i: (i, 0)), const(1, d), const(d, ntot),
                  const(1, widths[0]), const(CONV_WIDTH, widths[1]), const(CONV_WIDTH, dc), const(1, dc)],
        out_specs=[pl.BlockSpec((tm, w), lambda i: (i, 0)) for w in widths],
        out_shape=[jax.ShapeDtypeStruct((n, w), F32) for w in widths],
        scratch_shapes=[pltpu.VMEM((8, widths[0]), F32), pltpu.VMEM((8, widths[1]), F32),
                        pltpu.VMEM((8, dc), F32)],
        compiler_params=_cparams("arbitrary"),
        name="in_proj",
    )(xt, gain.reshape(1, d), w_cat, mu.reshape(1, -1), gdn_conv_w, lru_conv_w, lru_conv_b.reshape(1, dc))


def _per_sequence(one_sequence):
    def body(*refs):
        n_seq = refs[0].shape[0]
        state = refs[-1]

        @pl.when(pl.program_id(1) == 0)
        def _():
            state[...] = jnp.zeros_like(state)

        for s in range(n_seq):
            one_sequence(*[r.at[s] if len(r.shape) == 3 else r for r in refs])
    return body


def _rwkv_one(xs_ref, w0_ref, a0_ref, kk_ref, ka_ref, rk_ref, lnw_ref, lnb_ref,
              w2_ref, a2_ref, g2_ref, o_ref, state):
    c = CHUNK
    tt, da = o_ref.shape
    nh = RWKV_HEADS
    hd = da // nh

    xs = xs_ref[...]
    r = xs[:, 0:da]
    k = xs[:, da:2 * da]
    v = xs[:, 2 * da:3 * da]
    lo = xs[:, 3 * da:]

    w_log = -_softplus(-(w0_ref[...] + _bdot(jnp.tanh(lo), w2_ref[...]))) - 0.5
    lw = -jnp.exp(w_log)
    a = _sigmoid(a0_ref[...] + _bdot(lo, a2_ref[...]))
    g = _bdot(_sigmoid(lo), g2_ref[...])

    ones_h = _head_ones(da, hd)
    kkr = k * kk_ref[...]
    kk = kkr * lax.rsqrt(_exact_dot(kkr * kkr, ones_h) + 1e-6)
    k2 = k * (1.0 + (a - 1.0) * ka_ref[...])
    alpha = -(kk * a)

    rr = _iota((tt, tt), 0)
    cc = _iota((tt, tt), 1)
    tri = jnp.where((cc <= rr) & (cc // c == rr // c), 1.0, 0.0).astype(BF16)
    cum = _exact_dot_left(tri, lw)
    n_chunks = tt // c
    chunk_rows = [slice(g * c, (g + 1) * c) for g in range(n_chunks)]
    cum_ends = [cum[(g + 1) * c - 1:(g + 1) * c, :] for g in range(n_chunks)]
    cum_last = jnp.concatenate([jnp.broadcast_to(ce, (c, da)) for ce in cum_ends], axis=0)
    e_pos = jnp.exp(cum)
    e_neg = jnp.exp(-cum)
    e_tail = jnp.exp(cum_last - cum)
    r_t = r * e_pos
    b_t = kk * jnp.exp(cum - lw)
    a_t = alpha * e_neg
    k_t = k2 * e_neg
    a_end = alpha * e_tail
    k_end = k2 * e_tail

    nl = nh * c
    strict, incl = _tri_masks(c, nh)
    grams = [_bdot_nt(jnp.concatenate([b_t[rows], r_t[rows]], axis=0),
                      jnp.concatenate([_stack_heads(a_t[rows], nh), _stack_heads(k_t[rows], nh)], axis=0))
             for rows in chunk_rows]
    t_invs = _unit_lower_inverses([jnp.where(strict, gm[0:c, 0:nl], 0.0) for gm in grams], nh)
    vss = [_stack_heads(v[rows], nh) for rows in chunk_rows]
    lkvs = [_bdot(jnp.where(strict, gm[0:c, nl:], 0.0), vs) for gm, vs in zip(grams, vss)]
    wus = [_bdot(t_inv, jnp.concatenate([_stack_heads(b_t[rows], nh), _stack_heads(lkv, nh)], axis=1))
           for t_inv, lkv, rows in zip(t_invs, lkvs, chunk_rows)]
    s_as = [jnp.where(incl, gm[c:, 0:nl], 0.0) for gm in grams]
    s_ks = [jnp.where(incl, gm[c:, nl:], 0.0) for gm in grams]
    block_diag = _head_block_mask(da, hd)
    w_mats = [wu[:, 0:da] for wu in wus]
    u0s = [wu[:, da:] for wu in wus]
    q_effs = [r_t[rows] + _bdot(s_a, _stack_heads(w_mat, nh)) for rows, s_a, w_mat in zip(chunk_rows, s_as, w_mats)]
    o_consts = [_bdot(jnp.concatenate([s_a, s_k], axis=1), jnp.concatenate([_stack_heads(u0, nh), vs], axis=0))
                for s_a, s_k, u0, vs in zip(s_as, s_ks, u0s, vss)]
    s_lins = [jnp.where(block_diag, _bdot_tn(w_mat, a_end[rows]), 0.0) for w_mat, rows in zip(w_mats, chunk_rows)]
    s_consts = [jnp.where(block_diag, _bdot_tn(jnp.concatenate([u0, v[rows]], axis=0),
                                               jnp.concatenate([a_end[rows], k_end[rows]], axis=0)), 0.0)
                for u0, rows in zip(u0s, chunk_rows)]

    s_cur = state[...]
    outs = []
    for ce, q_eff, o_const, s_lin, s_const in zip(cum_ends, q_effs, o_consts, s_lins, s_consts):
        outs.append(_bdot_nt(q_eff, s_cur) + o_const)
        s_cur = s_cur * jnp.exp(ce) + _bdot(s_cur, s_lin) + s_const
    state[...] = s_cur
    o = jnp.concatenate(outs, axis=0)

    inv_hd = 1.0 / hd
    mean = _exact_dot(o, ones_h) * inv_hd
    cen = o - mean
    var = _exact_dot(cen * cen, ones_h) * inv_hd
    o_n = cen * lax.rsqrt(var + RWKV_GN_EPS) * lnw_ref[...] + lnb_ref[...]
    bonus = _exact_dot(r * k2 * rk_ref[...], ones_h) * v
    o_ref[...] = (o_n + bonus) * g


def _rwkv_group(p_a, bsz, seq, w0, w2, a0, a2, g2, k_k, k_a, r_k, lnx_w, lnx_b, tt=SCAN_TILE):
    n, a_cols = p_a.shape
    da = w0.shape[-1]
    n_lora = a_cols - 3 * da
    tt = min(tt, seq)
    nchunk = seq // tt
    d_dec, d_aaa = w2.shape[0], a2.shape[0]
    w2p = jnp.zeros((n_lora, da), F32).at[0:d_dec].set(w2).astype(BF16)
    a2p = jnp.zeros((n_lora, da), F32).at[d_dec:d_dec + d_aaa].set(a2).astype(BF16)
    g2p = jnp.zeros((n_lora, da), F32).at[d_dec + d_aaa:].set(g2).astype(BF16)
    row = lambda t: t.reshape(1, -1).astype(F32)
    vec = lambda w: pl.BlockSpec((1, w), lambda b, c: (0, 0))
    mat = pl.BlockSpec((n_lora, da), lambda b, c: (0, 0))
    nb = min(SCAN_SEQS, bsz)
    seqs = lambda w: pl.BlockSpec((nb, tt, w), lambda b, c: (b, c, 0))
    out = pl.pallas_call(
        _per_sequence(_rwkv_one),
        grid=(bsz // nb, nchunk),
        in_specs=[seqs(a_cols)] + [vec(da)] * 7 + [mat] * 3,
        out_specs=seqs(da),
        out_shape=jax.ShapeDtypeStruct((bsz, seq, da), F32),
        scratch_shapes=[pltpu.VMEM((nb, da, da), F32)],
        compiler_params=_cparams("parallel", "arbitrary"),
        name="rwkv7",
    )(p_a.reshape(bsz, seq, a_cols), row(w0), row(a0), row(k_k), row(k_a), row(r_k), row(lnx_w),
      row(lnx_b), w2p, a2p, g2p)
    return out.reshape(n, da)


def _gdn_one(qkv_ref, z_ref, ba_ref, alog_ref, dtb_ref, nw_ref, o_ref, state):
    c = CHUNK
    tt, db = o_ref.shape
    nh = GDN_HEADS
    hd = db // nh
    nl = nh * c

    q = qkv_ref[:, 0:db]
    k = qkv_ref[:, db:2 * db]
    v = qkv_ref[:, 2 * db:]

    def per_head(col_of_head):
        return jnp.concatenate([jnp.broadcast_to(col_of_head(h), (tt, hd)) for h in range(nh)], axis=1)

    def head_sumsq(x):
        return per_head(lambda h: jnp.sum(x[:, h * hd:(h + 1) * hd] * x[:, h * hd:(h + 1) * hd],
                                          axis=-1, keepdims=True))

    q = q * lax.rsqrt(head_sumsq(q) + 1e-6) * (hd ** -0.5)
    k = k * lax.rsqrt(head_sumsq(k) + 1e-6)

    ba = ba_ref[...]
    beta_l = _sigmoid(ba)
    g_l = -jnp.exp(alog_ref[...]) * _softplus(ba + dtb_ref[...])
    row_in_chunk = _iota((tt, LANES), 0) % c
    gcum_l = g_l
    d = 1
    while d < c:
        gcum_l = gcum_l + jnp.where(row_in_chunk >= d, pltpu.roll(gcum_l, d, 0), 0.0)
        d *= 2
    n_chunks = tt // c
    chunk_rows = [slice(g * c, (g + 1) * c) for g in range(n_chunks)]
    g_last_l = jnp.concatenate([jnp.broadcast_to(gcum_l[(g + 1) * c - 1:(g + 1) * c, :], (c, LANES))
                                for g in range(n_chunks)], axis=0)
    e_g_l = jnp.exp(gcum_l)
    e_tail_l = jnp.exp(g_last_l - gcum_l)
    beta = per_head(lambda h: beta_l[:, h:h + 1])
    e_g = per_head(lambda h: e_g_l[:, nh + h:nh + h + 1])
    e_tail = per_head(lambda h: e_tail_l[:, nh + h:nh + h + 1])
    kb = k * beta
    vb = v * beta
    kbe = kb * e_g
    qe = q * e_g
    k_end = k * e_tail

    strict, incl = _tri_masks(c, nh)
    lane_head = _iota((c, nl), 1) // c
    on_diag = (_iota((c, nl), 1) % c) == _iota((c, nl), 0)
    dmats = []
    for rows in chunk_rows:
        gc = gcum_l[rows]
        gcol = jnp.broadcast_to(gc[:, nh:nh + 1], (c, nl))
        for h in range(1, nh):
            gcol = jnp.where(lane_head == h, jnp.broadcast_to(gc[:, nh + h:nh + h + 1], (c, nl)), gcol)
        grow = jnp.sum(jnp.where(on_diag, gcol, 0.0), axis=0, keepdims=True)
        dmats.append(jnp.exp(jnp.where(incl, gcol - grow, NEG_BIG)))
    grams = [_bdot_nt(jnp.concatenate([kb[rows], q[rows]], axis=0), _stack_heads(k[rows], nh))
             for rows in chunk_rows]
    t_invs = _unit_lower_inverses([jnp.where(strict, -(gm[0:c] * dm), 0.0) for gm, dm in zip(grams, dmats)], nh)
    intras = [jnp.where(incl, gm[c:] * dm, 0.0) for gm, dm in zip(grams, dmats)]
    sols = [_bdot(t_inv, jnp.concatenate([_stack_heads(vb[rows], nh), _stack_heads(kbe[rows], nh)], axis=1))
            for t_inv, rows in zip(t_invs, chunk_rows)]

    s_cur = state[...]
    block_diag = _head_block_mask(db, hd)
    outs = []
    for g, (rows, sol, intra) in enumerate(zip(chunk_rows, sols, intras)):
        u_s, w_s = sol[:, 0:db], sol[:, db:]
        wq = _bdot(jnp.concatenate([w_s, qe[rows]], axis=0), s_cur)
        v_new = u_s - wq[0:c]
        outs.append(wq[c:] + _bdot(intra, _stack_heads(v_new, nh)))
        upd = _bdot_tn(k_end[rows], v_new)
        s_cur = s_cur * e_g[(g + 1) * c - 1:(g + 1) * c, :] + jnp.where(block_diag, upd, 0.0)
    state[...] = s_cur
    o = jnp.concatenate(outs, axis=0)
    o = o * lax.rsqrt(head_sumsq(o) * (1.0 / hd) + NORM_EPS) * nw_ref[...]
    o_ref[...] = o * _silu(z_ref[...])


def _gdn_group(qkv, z, ba, bsz, seq, a_log, dt_bias, norm_w, tt=SCAN_TILE):
    n, w3 = qkv.shape
    db = z.shape[-1]
    nh = GDN_HEADS
    tt = min(tt, seq)
    nchunk = seq // tt
    alog_row = jnp.zeros((1, LANES), F32).at[0, nh:2 * nh].set(a_log)
    dtb_row = jnp.zeros((1, LANES), F32).at[0, nh:2 * nh].set(dt_bias)
    nw_row = jnp.tile(norm_w, nh).reshape(1, db)
    nb = min(SCAN_SEQS, bsz)
    seqs = lambda w: pl.BlockSpec((nb, tt, w), lambda b, c: (b, c, 0))
    const = lambda r, w: pl.BlockSpec((r, w), lambda b, c: (0, 0))
    out = pl.pallas_call(
        _per_sequence(_gdn_one),
        grid=(bsz // nb, nchunk),
        in_specs=[seqs(w3), seqs(db), seqs(LANES), const(1, LANES), const(1, LANES), const(1, db)],
        out_specs=seqs(db),
        out_shape=jax.ShapeDtypeStruct((bsz, seq, db), F32),
        scratch_shapes=[pltpu.VMEM((nb, db, db), F32)],
        compiler_params=_cparams("parallel", "arbitrary"),
        name="gdn",
    )(qkv.reshape(bsz, seq, w3), z.reshape(bsz, seq, db), ba.reshape(bsz, seq, LANES), alog_row, dtb_row, nw_row)
    return out.reshape(n, db)


def _lru_body(p_ref, wax_ref, ba_ref, bx_ref, lam_ref, o_ref, hcar):
    tt = p_ref.shape[0]
    dc = o_ref.shape[-1]

    @pl.when(pl.program_id(1) == 0)
    def _():
        hcar[...] = jnp.zeros_like(hcar)

    conv = p_ref[:, 0:dc]
    gate = p_ref[:, dc:]

    ri = _bdot(conv, wax_ref[...])
    r = _sigmoid(ri[:, 0:dc] + ba_ref[...])
    i = _sigmoid(ri[:, dc:] + bx_ref[...])
    log_a = -LRU_C * r * _softplus(-lam_ref[...])
    a = jnp.exp(log_a)
    mult = jnp.sqrt(jnp.tanh(-log_a) * (a * a + 1.0))
    row = _iota((tt, dc), 0)
    mult = jnp.where((row == 0) & (pl.program_id(1) == 0), 1.0, mult)
    bv = conv * i * mult

    av = a
    d = 1
    while d < tt:
        a_sh = jnp.where(row >= d, pltpu.roll(av, d, 0), 1.0)
        b_sh = jnp.where(row >= d, pltpu.roll(bv, d, 0), 0.0)
        bv = av * b_sh + bv
        av = av * a_sh
        d *= 2
    h = av * hcar[...] + bv
    hcar[...] = h[tt - 1:tt, :]
    gelu = 0.5 * gate * (1.0 + jnp.tanh(0.7978845608028654 * (gate + 0.044715 * gate * gate * gate)))
    o_ref[...] = h * gelu


def _block_diag(w):
    nb, bi, bo = w.shape
    out = jnp.zeros((nb * bi, nb * bo), w.dtype)
    for b in range(nb):
        out = out.at[b * bi:(b + 1) * bi, b * bo:(b + 1) * bo].set(w[b])
    return out


def _lru_group(p_c, bsz, seq, w_a, b_a, w_x, b_x, lam, tt=256):
    n, c_cols = p_c.shape
    dc = c_cols // 2
    tt = min(tt, seq)
    nt = seq // tt
    wax = jnp.concatenate([_block_diag(w_a), _block_diag(w_x)], axis=1).astype(BF16)
    row = lambda t: t.reshape(1, -1).astype(F32)
    const = lambda r, w: pl.BlockSpec((r, w), lambda b, c: (0, 0))
    return pl.pallas_call(
        _lru_body,
        grid=(bsz, nt),
        in_specs=[pl.BlockSpec((tt, c_cols), lambda b, c: (b * nt + c, 0)),
                  const(dc, 2 * dc), const(1, dc), const(1, dc), const(1, dc)],
        out_specs=pl.BlockSpec((tt, dc), lambda b, c: (b * nt + c, 0)),
        out_shape=jax.ShapeDtypeStruct((n, dc), F32),
        scratch_shapes=[pltpu.VMEM((1, dc), F32)],
        compiler_params=_cparams("parallel", "arbitrary"),
        name="rglru",
    )(p_c, wax, row(b_a), row(b_x), row(lam))


def _outproj_body(x_ref, ma_ref, mb_ref, mc_ref, wo_ref, g_ref, wrh_ref, wrl_ref, br_ref,
                  xo_ref, h_ref, ri_ref, rf_ref):
    da = ma_ref.shape[-1]
    db = mb_ref.shape[-1]
    tm = x_ref.shape[0]
    mix = (jnp.dot(ma_ref[...].astype(BF16), wo_ref[0:da, :], preferred_element_type=F32)
           + jnp.dot(mb_ref[...].astype(BF16), wo_ref[da:da + db, :], preferred_element_type=F32)
           + jnp.dot(mc_ref[...].astype(BF16), wo_ref[da + db:, :], preferred_element_type=F32))
    x = x_ref[...] + mix
    xo_ref[...] = x
    h = x * lax.rsqrt(jnp.mean(x * x, axis=-1, keepdims=True) + NORM_EPS) * g_ref[...]
    h_ref[...] = h

    h_hi = h.astype(BF16)
    h_lo = (h - h_hi.astype(F32)).astype(BF16)
    logits = (jnp.dot(h_hi, wrh_ref[...], preferred_element_type=F32)
              + jnp.dot(h_lo, wrh_ref[...], preferred_element_type=F32)
              + jnp.dot(h_hi, wrl_ref[...], preferred_element_type=F32)) + br_ref[...]
    lane = _iota((tm, LANES), 1)
    far = 4 * LANES
    gmask = lane < MOE_GROUPS
    gl = jnp.where(gmask, logits, NEG_BIG)
    gmax = jnp.max(gl, axis=-1, keepdims=True)
    gidx = jnp.min(jnp.where(gl == gmax, lane, far), axis=-1, keepdims=True)
    gsum = jnp.sum(jnp.where(gmask, jnp.exp(gl - gmax), 0.0), axis=-1, keepdims=True)
    grp_w = 1.0 / gsum
    eidx = lane - MOE_GROUPS
    emask = (eidx >= 0) & (eidx < MOE_EXPERTS) & ((eidx // MOE_PER_GROUP) == gidx)
    el = jnp.where(emask, logits, NEG_BIG)
    v1 = jnp.max(el, axis=-1, keepdims=True)
    i1 = jnp.min(jnp.where(el == v1, lane, far), axis=-1, keepdims=True)
    el2 = jnp.where(lane == i1, NEG_BIG, el)
    v2 = jnp.max(el2, axis=-1, keepdims=True)
    i2 = jnp.min(jnp.where(el2 == v2, lane, far), axis=-1, keepdims=True)
    pr = jnp.exp(v2 - v1)
    g0 = grp_w / (1.0 + pr)
    g1 = g0 * pr
    ri_ref[...] = jnp.where(lane == 0, i1 - MOE_GROUPS, jnp.where(lane == 1, i2 - MOE_GROUPS, 0))
    rf_ref[...] = jnp.where(lane == 0, g0, jnp.where(lane == 1, g1, 0.0))


def _out_proj_router(xt, o_a, o_b, o_c, w_out, gain, w_group, b_group, w_expert, b_expert, tm=512):
    n, d = xt.shape
    tm = min(tm, n)
    ng, ne = w_group.shape[1], w_expert.shape[1]
    wr = jnp.zeros((d, LANES), F32).at[:, 0:ng].set(w_group).at[:, ng:ng + ne].set(w_expert)
    wr_hi = wr.astype(BF16)
    wr_lo = (wr - wr_hi.astype(F32)).astype(BF16)
    br = jnp.zeros((1, LANES), F32).at[0, 0:ng].set(b_group).at[0, ng:ng + ne].set(b_expert)
    rows = lambda w: pl.BlockSpec((tm, w), lambda i: (i, 0))
    const = lambda r, w: pl.BlockSpec((r, w), lambda i: (0, 0))
    return pl.pallas_call(
        _outproj_body,
        grid=(n // tm,),
        in_specs=[rows(d), rows(o_a.shape[1]), rows(o_b.shape[1]), rows(o_c.shape[1]),
                  const(d, d), const(1, d), const(d, LANES), const(d, LANES), const(1, LANES)],
        out_specs=[rows(d), rows(d), rows(LANES), rows(LANES)],
        out_shape=[jax.ShapeDtypeStruct((n, d), F32), jax.ShapeDtypeStruct((n, d), F32),
                   jax.ShapeDtypeStruct((n, LANES), jnp.int32), jax.ShapeDtypeStruct((n, LANES), F32)],
        compiler_params=_cparams("parallel"),
        name="out_proj_router",
    )(xt, o_a, o_b, o_c, w_out.astype(BF16), gain.reshape(1, d), wr_hi, wr_lo, br)


def _rank_body(ri_ref, rank_ref, cnt_ref, carry):
    tr = ri_ref.shape[0]

    @pl.when(pl.program_id(0) == 0)
    def _():
        carry[...] = jnp.zeros_like(carry)

    ri = ri_ref[...]
    lane = _iota((tr, LANES), 1)
    oh0 = lane == ri[:, 0:1]
    oh1 = lane == ri[:, 1:2]
    oh = jnp.where(oh0 | oh1, 1.0, 0.0)
    earlier = jnp.where(_iota((tr, tr), 1) < _iota((tr, tr), 0), 1.0, 0.0).astype(BF16)
    cnt = jnp.dot(earlier, oh.astype(BF16), preferred_element_type=F32) + carry[...]
    r0 = jnp.sum(jnp.where(oh0, cnt, 0.0), axis=-1, keepdims=True)
    r1 = jnp.sum(jnp.where(oh1, cnt, 0.0), axis=-1, keepdims=True)
    rank_ref[...] = jnp.where(lane == 0, r0, jnp.where(lane == 1, r1, 0.0)).astype(jnp.int32)
    carry[...] = carry[...] + jnp.sum(oh, axis=0, keepdims=True)
    cnt_ref[...] = carry[...].astype(jnp.int32)


def _expert_ranks(ri, tr=512):
    n = ri.shape[0]
    tr = min(tr, n)
    return pl.pallas_call(
        _rank_body,
        grid=(n // tr,),
        in_specs=[pl.BlockSpec((tr, LANES), lambda i: (i, 0))],
        out_specs=[pl.BlockSpec((tr, LANES), lambda i: (i, 0)), pl.BlockSpec((1, LANES), lambda i: (0, 0))],
        out_shape=[jax.ShapeDtypeStruct((n, LANES), jnp.int32), jax.ShapeDtypeStruct((1, LANES), jnp.int32)],
        scratch_shapes=[pltpu.VMEM((1, LANES), F32)],
        compiler_params=_cparams("arbitrary"),
        name="expert_ranks",
    )(ri)


def _padded_starts(cnt_row):
    cnt = jnp.broadcast_to(cnt_row, (8, LANES))
    padded = (((cnt + (MOE_BLOCK - 1)) // MOE_BLOCK) * MOE_BLOCK).astype(F32)
    before = jnp.where(_iota((LANES, LANES), 0) < _iota((LANES, LANES), 1), 1.0, 0.0).astype(F32)
    start = jnp.dot(padded, before, precision=HIGHEST, preferred_element_type=F32)
    return start, start + padded


def _dest_body(ri_ref, rank_ref, cnt_ref, dest_ref):
    tr = ri_ref.shape[0]
    start, _ = _padded_starts(cnt_ref[...])
    start = start[0:1, :]
    ri = ri_ref[...]
    rank = rank_ref[...]
    lane = _iota((tr, LANES), 1)
    s0 = jnp.sum(jnp.where(lane == ri[:, 0:1], start, 0.0), axis=-1, keepdims=True).astype(jnp.int32)
    s1 = jnp.sum(jnp.where(lane == ri[:, 1:2], start, 0.0), axis=-1, keepdims=True).astype(jnp.int32)
    dest_ref[...] = jnp.where(lane == 0, s0 + rank[:, 0:1], jnp.where(lane == 1, s1 + rank[:, 1:2], 0))


def _destinations(ri, rank, cnt, tr=512):
    n = ri.shape[0]
    tr = min(tr, n)
    rows = pl.BlockSpec((tr, LANES), lambda i: (i, 0))
    return pl.pallas_call(
        _dest_body,
        grid=(n // tr,),
        in_specs=[rows, rows, pl.BlockSpec((1, LANES), lambda i: (0, 0))],
        out_specs=rows,
        out_shape=jax.ShapeDtypeStruct((n, LANES), jnp.int32),
        compiler_params=_cparams("parallel"),
        name="moe_destinations",
    )(ri, rank, cnt)


def _blkexp_body(cnt_ref, be_ref):
    nbp = be_ref.shape[0]
    _, end = _padded_starts(cnt_ref[...])
    end = end[0:1, :]
    lane = _iota((nbp, LANES), 1)
    row_start = (_iota((nbp, LANES), 0) * MOE_BLOCK).astype(F32)
    done = jnp.where((lane < MOE_EXPERTS) & (end <= row_start), 1.0, 0.0)
    be = jnp.minimum(jnp.sum(done, axis=-1, keepdims=True), MOE_EXPERTS - 1.0)
    be_ref[...] = jnp.broadcast_to(be, (nbp, LANES)).astype(jnp.int32)


def _block_experts(cnt, n_blocks):
    nbp = -(-n_blocks // 8) * 8
    out = pl.pallas_call(
        _blkexp_body,
        grid=(1,),
        in_specs=[pl.BlockSpec((1, LANES), lambda i: (0, 0))],
        out_specs=pl.BlockSpec((nbp, LANES), lambda i: (0, 0)),
        out_shape=jax.ShapeDtypeStruct((nbp, LANES), jnp.int32),
        name="moe_block_experts",
    )(cnt)
    return out[:n_blocks, 0]


def _for_each_row(n_rows, fn):
    def lane_row(i, carry):
        for j in range(LANES):
            fn(i * LANES + j, i, j)
        return carry

    lax.fori_loop(0, n_rows // LANES, lane_row, 0)


def _scatter_body(d0_ref, d1_ref, h_ref, xz_hbm, xb_hbm, sem):
    del xz_hbm
    ts = h_ref.shape[0]

    def copies(t, i, j):
        src = h_ref.at[pl.ds(t, 1)]
        return (pltpu.make_async_copy(src, xb_hbm.at[pl.ds(d0_ref[i, j], 1)], sem),
                pltpu.make_async_copy(src, xb_hbm.at[pl.ds(d1_ref[i, j], 1)], sem))

    def start(t, i, j):
        for prio, cp in enumerate(copies(t, i, j)):
            cp.start(priority=prio)

    def wait(t, i, j):
        for cp in copies(t, i, j):
            cp.wait()

    _for_each_row(ts, start)
    _for_each_row(ts, wait)


def _scatter_rows(h2, dest0, dest1, n_rows, ts):
    n, d = h2.shape
    smem = pl.BlockSpec((ts // LANES, LANES), lambda i: (i, 0), memory_space=pltpu.SMEM)
    anyspec = pl.BlockSpec(memory_space=pl.ANY)
    return pl.pallas_call(
        _scatter_body,
        grid=(n // ts,),
        in_specs=[smem, smem, pl.BlockSpec((ts, d), lambda i: (i, 0)), anyspec],
        out_specs=anyspec,
        out_shape=jax.ShapeDtypeStruct((n_rows, d), F32),
        scratch_shapes=[pltpu.SemaphoreType.DMA(())],
        input_output_aliases={3: 0},
        compiler_params=_cparams("arbitrary"),
        name="moe_scatter",
    )(dest0, dest1, h2, jnp.zeros((n_rows, d), F32))


def _expert_body(be_ref, x_ref, wgu_ref, wd_ref, y_ref):
    del be_ref
    de = wd_ref.shape[1]
    gu = jnp.dot(x_ref[...].astype(BF16), wgu_ref[0], preferred_element_type=F32)
    hid = _silu(gu[:, 0:de]) * gu[:, de:]
    y_ref[...] = jnp.dot(hid.astype(BF16), wd_ref[0], preferred_element_type=F32)


def _expert_mlp(xb, blk_expert, w_gu, w_down):
    n_rows, d = xb.shape
    nb = n_rows // MOE_BLOCK
    de = w_down.shape[1]
    grid_spec = pltpu.PrefetchScalarGridSpec(
        num_scalar_prefetch=1,
        grid=(nb,),
        in_specs=[pl.BlockSpec((MOE_BLOCK, d), lambda i, be: (i, 0)),
                  pl.BlockSpec((1, d, 2 * de), lambda i, be: (be[i], 0, 0)),
                  pl.BlockSpec((1, de, d), lambda i, be: (be[i], 0, 0))],
        out_specs=pl.BlockSpec((MOE_BLOCK, d), lambda i, be: (i, 0)),
    )
    return pl.pallas_call(
        _expert_body,
        grid_spec=grid_spec,
        out_shape=jax.ShapeDtypeStruct((n_rows, d), F32),
        compiler_params=_cparams("arbitrary"),
        name="moe_experts",
    )(blk_expert, xb, w_gu, w_down)


def _combine_body(final_norm, d0_ref, d1_ref, rf_ref, x_ref, g_ref, yb_hbm, o_ref, gbuf, sem):
    ts = x_ref.shape[0]

    def copies(t, i, j):
        return (pltpu.make_async_copy(yb_hbm.at[pl.ds(d0_ref[i, j], 1)], gbuf.at[0, pl.ds(t, 1)], sem),
                pltpu.make_async_copy(yb_hbm.at[pl.ds(d1_ref[i, j], 1)], gbuf.at[1, pl.ds(t, 1)], sem))

    def start(t, i, j):
        for prio, cp in enumerate(copies(t, i, j)):
            cp.start(priority=prio)

    def wait(t, i, j):
        for cp in copies(t, i, j):
            cp.wait()

    _for_each_row(ts, start)
    _for_each_row(ts, wait)
    rf = rf_ref[...]
    x = x_ref[...] + (gbuf[0] * rf[:, 0:1] + gbuf[1] * rf[:, 1:2])
    if final_norm:
        x = x * lax.rsqrt(jnp.mean(x * x, axis=-1, keepdims=True) + NORM_EPS) * g_ref[...]
    o_ref[...] = x


def _combine(x_mid, yb, dest0, dest1, rf, final_gain, ts):
    n, d = x_mid.shape
    final_norm = final_gain is not None
    gain = (final_gain if final_norm else jnp.ones((d,), F32)).reshape(1, d)
    smem = pl.BlockSpec((ts // LANES, LANES), lambda i: (i, 0), memory_space=pltpu.SMEM)
    rows = lambda w: pl.BlockSpec((ts, w), lambda i: (i, 0))
    return pl.pallas_call(
        functools.partial(_combine_body, final_norm),
        grid=(n // ts,),
        in_specs=[smem, smem, rows(LANES), rows(d), pl.BlockSpec((1, d), lambda i: (0, 0)),
                  pl.BlockSpec(memory_space=pl.ANY)],
        out_specs=rows(d),
        out_shape=jax.ShapeDtypeStruct((n, d), F32),
        scratch_shapes=[pltpu.VMEM((2, ts, d), F32), pltpu.SemaphoreType.DMA(())],
        compiler_params=_cparams("arbitrary"),
        name="moe_combine",
    )(dest0, dest1, rf, x_mid, gain, yb)


def _moe(x_mid, h2, ri, rf, w_gate, w_up, w_down, final_gain):
    n, d = x_mid.shape
    n_assign = 2 * n
    n_blocks = (n_assign + MOE_EXPERTS * (MOE_BLOCK - 1) + MOE_BLOCK - 1) // MOE_BLOCK
    ts = min(1024, n)
    rank, cnt = _expert_ranks(ri)
    dest = _destinations(ri, rank, cnt)
    blk_expert = _block_experts(cnt, n_blocks)
    dest0 = dest[:, 0].reshape(n // LANES, LANES)
    dest1 = dest[:, 1].reshape(n // LANES, LANES)
    xb = _scatter_rows(h2, dest0, dest1, n_blocks * MOE_BLOCK, ts)
    w_gu = jnp.concatenate([w_gate, w_up], axis=-1).astype(BF16)
    yb = _expert_mlp(xb, blk_expert, w_gu, w_down.astype(BF16))
    return _combine(x_mid, yb, dest0, dest1, rf, final_gain, ts)


def kernel(x, norm_mix, norm_ffn, norm_final, w_in, w_out, rwkv_mu, rwkv_w0, rwkv_w2, rwkv_a0, rwkv_a2, rwkv_g2, rwkv_k_k, rwkv_k_a, rwkv_r_k, rwkv_lnx_w, rwkv_lnx_b, gdn_conv_w, gdn_a_log, gdn_dt_bias, gdn_norm_w, lru_conv_w, lru_conv_b, lru_w_a, lru_b_a, lru_w_x, lru_b_x, lru_lambda, moe_w_group, moe_b_group, moe_w_expert, moe_b_expert, moe_w_gate, moe_w_up, moe_w_down):
    bsz, seq, d = x.shape
    depth = w_in.shape[0]
    n = bsz * seq
    a_cols = rwkv_mu.shape[1]
    db = gdn_norm_w.shape[1] * GDN_HEADS
    dc = lru_conv_b.shape[1]
    n_ba = 2 * GDN_HEADS
    widths = (a_cols, 3 * db, db, 2 * dc, LANES)
    xt = x.reshape(n, d)
    for l in range(depth):
        w = w_in[l]
        b0 = a_cols
        c0 = b0 + 4 * db + n_ba
        w_cat = jnp.concatenate(
            [w[:, 0:a_cols], w[:, b0:b0 + 4 * db], w[:, c0:c0 + 2 * dc],
             w[:, b0 + 4 * db:c0], jnp.zeros((d, LANES - n_ba), F32)], axis=1).astype(BF16)
        p_a, p_qkv, p_z, p_c, p_ba = _in_proj(xt, seq, norm_mix[l], w_cat, widths, rwkv_mu[l], gdn_conv_w[l],
                                              lru_conv_w[l], lru_conv_b[l])
        o_a = _rwkv_group(p_a, bsz, seq, rwkv_w0[l], rwkv_w2[l], rwkv_a0[l], rwkv_a2[l], rwkv_g2[l],
                          rwkv_k_k[l], rwkv_k_a[l], rwkv_r_k[l], rwkv_lnx_w[l], rwkv_lnx_b[l])
        o_b = _gdn_group(p_qkv, p_z, p_ba, bsz, seq, gdn_a_log[l], gdn_dt_bias[l], gdn_norm_w[l])
        o_c = _lru_group(p_c, bsz, seq, lru_w_a[l], lru_b_a[l], lru_w_x[l], lru_b_x[l], lru_lambda[l])
        x_mid, h2, ri, rf = _out_proj_router(xt, o_a, o_b, o_c, w_out[l], norm_ffn[l], moe_w_group[l],
                                             moe_b_group[l], moe_w_expert[l], moe_b_expert[l])
        xt = _moe(x_mid, h2, ri, rf, moe_w_gate[l], moe_w_up[l], moe_w_down[l],
                  norm_final if l == depth - 1 else None)
    return xt.reshape(bsz, seq, d)
```

```python
import functools

import jax
import jax.numpy as jnp
from jax import lax
from jax.experimental import pallas as pl
from jax.experimental.pallas import tpu as pltpu

F32 = jnp.float32
BF16 = jnp.bfloat16
HIGHEST = lax.Precision.HIGHEST

NORM_EPS = 1e-6
CONV_WIDTH = 4
CHUNK = 64
SCAN_TILE = 256
SCAN_SEQS = 1
RWKV_HEADS = 4
RWKV_GN_EPS = 64e-5
GDN_HEADS = 4
LRU_C = 8.0
MOE_GROUPS = 4
MOE_PER_GROUP = 8
MOE_EXPERTS = MOE_GROUPS * MOE_PER_GROUP
MOE_BLOCK = 256
LANES = 128
NEG_BIG = -1e30
VMEM_LIMIT = 56 * 1024 * 1024


def _cparams(*sem):
    return pltpu.CompilerParams(dimension_semantics=sem, vmem_limit_bytes=VMEM_LIMIT)


def _bdot(a, b):
    return jnp.dot(a.astype(BF16), b.astype(BF16), preferred_element_type=F32)


def _bdot_nt(a, b):
    return lax.dot_general(a.astype(BF16), b.astype(BF16), (((1,), (1,)), ((), ())),
                           preferred_element_type=F32)


def _bdot_tn(a, b):
    return lax.dot_general(a.astype(BF16), b.astype(BF16), (((0,), (0,)), ((), ())),
                           preferred_element_type=F32)


def _exact_dot(a, b01):
    hi = a.astype(BF16)
    lo = (a - hi.astype(F32)).astype(BF16)
    return (jnp.dot(hi, b01, preferred_element_type=F32)
            + jnp.dot(lo, b01, preferred_element_type=F32))


def _exact_dot_left(a01, b):
    hi = b.astype(BF16)
    lo = (b - hi.astype(F32)).astype(BF16)
    return (jnp.dot(a01, hi, preferred_element_type=F32)
            + jnp.dot(a01, lo, preferred_element_type=F32))


def _load_token_tiles(ref, rows):
    per = ref.shape[0] // rows
    return jnp.concatenate([ref[pl.ds(c, rows, stride=per), :] for c in range(per)], axis=1)


def _store_token_tiles(ref, value):
    rows, d = value.shape
    per = d // LANES
    for c in range(per):
        ref[pl.ds(c, rows, stride=per), :] = value[:, c * LANES:(c + 1) * LANES]


def _softplus(x):
    return jnp.maximum(x, 0.0) + jnp.log1p(jnp.exp(-jnp.abs(x)))


def _sigmoid(x):
    return 1.0 / (1.0 + jnp.exp(-x))


def _silu(x):
    return x * _sigmoid(x)


def _iota(shape, dim):
    return lax.broadcasted_iota(jnp.int32, shape, dim)


def _head_ones(width, head_dim):
    r = _iota((width, width), 0) // head_dim
    c = _iota((width, width), 1) // head_dim
    return jnp.where(r == c, 1.0, 0.0).astype(BF16)


def _stack_heads(x, n_heads):
    c, w = x.shape
    hd = w // n_heads
    xb = x.astype(BF16)
    lane_head = _iota((c, w), 1) // hd
    return jnp.concatenate([jnp.where(lane_head == h, xb, jnp.zeros_like(xb)) for h in range(n_heads)],
                           axis=0)


def _tri_masks(c, n_heads):
    r = _iota((c, n_heads * c), 0)
    s = _iota((c, n_heads * c), 1) % c
    return s < r, s <= r


def _unit_lower_inverses(l_mats, n_heads):
    shape = l_mats[0].shape
    c = shape[0]
    eye = jnp.where(_iota(shape, 1) % c == _iota(shape, 0), 1.0, 0.0).astype(F32)
    ps = [eye + l for l in l_mats]
    ms = [_bdot(l, _stack_heads(l, n_heads)) for l in l_mats]
    span = 2
    while span < c:
        last = span * 2 >= c
        nxt = [_bdot(p if last else jnp.concatenate([m, p], axis=0), _stack_heads(m, n_heads))
               for m, p in zip(ms, ps)]
        ps = [p + (r if last else r[c:]) for p, r in zip(ps, nxt)]
        ms = [None if last else r[0:c] for r in nxt]
        span *= 2
    return ps


def _head_block_mask(width, head_dim):
    r = _iota((width, width), 0) // head_dim
    c = _iota((width, width), 1) // head_dim
    return r == c


def _shift_rows(y, tail, j):
    rolled = pltpu.roll(y, j, 0)
    head = jnp.where(_iota(tail.shape, 0) < j, pltpu.roll(tail, j, 0), rolled[0:8])
    return jnp.concatenate([head, rolled[8:]], axis=0)


def _causal_conv(y, tail, cw):
    acc = cw[CONV_WIDTH - 1:CONV_WIDTH, :] * y
    for j in range(1, CONV_WIDTH):
        acc = acc + cw[CONV_WIDTH - 1 - j:CONV_WIDTH - j, :] * _shift_rows(y, tail, j)
    return acc


def _inproj_body(tiles_per_seq, x_ref, g_ref, w_ref, mu_ref, gcw_ref, lcw_ref, lcb_ref,
                 a_ref, qkv_ref, z_ref, c_ref, ba_ref, atail, qtail, ctail):
    tm = x_ref.shape[0]
    tails = (atail, qtail, ctail)

    @pl.when(pl.program_id(0) % tiles_per_seq == 0)
    def _():
        for tail in tails:
            tail[...] = jnp.zeros_like(tail)

    x = x_ref[...]
    h = x * lax.rsqrt(jnp.mean(x * x, axis=-1, keepdims=True) + NORM_EPS) * g_ref[...]
    hb = h.astype(BF16)
    wa, wq, wz, wc = a_ref.shape[-1], qkv_ref.shape[-1], z_ref.shape[-1], c_ref.shape[-1]
    dc = ctail.shape[1]

    def proj(off, width):
        return jnp.dot(hb, w_ref[:, off:off + width], preferred_element_type=F32)

    for s in range(3):
        cols = slice(s * (wq // 3), (s + 1) * (wq // 3))
        y = proj(wa + cols.start, wq // 3)
        qkv_ref[:, cols] = _silu(_causal_conv(y, qtail[:, cols], gcw_ref[:, cols]))
        qtail[:, cols] = y[tm - 8:tm, :]
    p = proj(0, wa)
    a_ref[...] = p + (_shift_rows(p, atail[...], 1) - p) * mu_ref[...]
    atail[...] = p[tm - 8:tm, :]
    z_ref[...] = proj(wa + wq, wz)
    pc = proj(wa + wq + wz, wc)
    c_ref[:, 0:dc] = lcb_ref[...] + _causal_conv(pc[:, 0:dc], ctail[...], lcw_ref[...])
    c_ref[:, dc:] = pc[:, dc:]
    ctail[...] = pc[tm - 8:tm, 0:dc]
    ba_ref[...] = proj(wa + wq + wz + wc, ba_ref.shape[-1])


def _in_proj(xt, seq, gain, w_cat, widths, mu, gdn_conv_w, lru_conv_w, lru_conv_b, tm=512):
    n, d = xt.shape
    ntot = w_cat.shape[1]
    tm = min(tm, seq)
    dc = lru_conv_b.shape[0]
    const = lambda r, w: pl.BlockSpec((r, w), lambda i: (0, 0))
    return pl.pallas_call(
        functools.partial(_inproj_body, seq // tm),
        grid=(n // tm,),
        in_specs=[pl.BlockSpec((tm, d), lambda i: (i, 0)), const(1, d), const(d, ntot),
                  const(1, widths[0]), const(CONV_WIDTH, widths[1]), const(CONV_WIDTH, dc), const(1, dc)],
        out_specs=[pl.BlockSpec((tm, w), lambda i: (i, 0)) for w in widths],
        out_shape=[jax.ShapeDtypeStruct((n, w), F32) for w in widths],
        scratch_shapes=[pltpu.VMEM((8, widths[0]), F32), pltpu.VMEM((8, widths[1]), F32),
                        pltpu.VMEM((8, dc), F32)],
        compiler_params=_cparams("arbitrary"),
        name="in_proj",
    )(xt, gain.reshape(1, d), w_cat, mu.reshape(1, -1), gdn_conv_w, lru_conv_w, lru_conv_b.reshape(1, dc))


def _per_sequence(one_sequence):
    def body(*refs):
        n_seq = refs[0].shape[0]
        state = refs[-1]

        @pl.when(pl.program_id(1) == 0)
        def _():
            state[...] = jnp.zeros_like(state)

        for s in range(n_seq):
            one_sequence(*[r.at[s] if len(r.shape) == 3 else r for r in refs])
    return body


def _rwkv_one(xs_ref, w0_ref, a0_ref, kk_ref, ka_ref, rk_ref, lnw_ref, lnb_ref,
              w2_ref, a2_ref, g2_ref, o_ref, state):
    c = CHUNK
    tt, da = o_ref.shape
    nh = RWKV_HEADS
    hd = da // nh

    xs = xs_ref[...]
    r = xs[:, 0:da]
    k = xs[:, da:2 * da]
    v = xs[:, 2 * da:3 * da]
    lo = xs[:, 3 * da:]

    w_log = -_softplus(-(w0_ref[...] + _bdot(jnp.tanh(lo), w2_ref[...]))) - 0.5
    lw = -jnp.exp(w_log)
    a = _sigmoid(a0_ref[...] + _bdot(lo, a2_ref[...]))
    g = _bdot(_sigmoid(lo), g2_ref[...])

    ones_h = _head_ones(da, hd)
    kkr = k * kk_ref[...]
    kk = kkr * lax.rsqrt(_exact_dot(kkr * kkr, ones_h) + 1e-6)
    k2 = k * (1.0 + (a - 1.0) * ka_ref[...])
    alpha = -(kk * a)

    rr = _iota((tt, tt), 0)
    cc = _iota((tt, tt), 1)
    tri = jnp.where((cc <= rr) & (cc // c == rr // c), 1.0, 0.0).astype(BF16)
    cum = _exact_dot_left(tri, lw)
    n_chunks = tt // c
    chunk_rows = [slice(g * c, (g + 1) * c) for g in range(n_chunks)]
    cum_ends = [cum[(g + 1) * c - 1:(g + 1) * c, :] for g in range(n_chunks)]
    cum_last = jnp.concatenate([jnp.broadcast_to(ce, (c, da)) for ce in cum_ends], axis=0)
    e_pos = jnp.exp(cum)
    e_neg = jnp.exp(-cum)
    e_tail = jnp.exp(cum_last - cum)
    r_t = r * e_pos
    b_t = kk * jnp.exp(cum - lw)
    a_t = alpha * e_neg
    k_t = k2 * e_neg
    a_end = alpha * e_tail
    k_end = k2 * e_tail

    nl = nh * c
    strict, incl = _tri_masks(c, nh)
    grams = [_bdot_nt(jnp.concatenate([b_t[rows], r_t[rows]], axis=0),
                      jnp.concatenate([_stack_heads(a_t[rows], nh), _stack_heads(k_t[rows], nh)], axis=0))
             for rows in chunk_rows]
    t_invs = _unit_lower_inverses([jnp.where(strict, gm[0:c, 0:nl], 0.0) for gm in grams], nh)
    vss = [_stack_heads(v[rows], nh) for rows in chunk_rows]
    lkvs = [_bdot(jnp.where(strict, gm[0:c, nl:], 0.0), vs) for gm, vs in zip(grams, vss)]
    wus = [_bdot(t_inv, jnp.concatenate([_stack_heads(b_t[rows], nh), _stack_heads(lkv, nh)], axis=1))
           for t_inv, lkv, rows in zip(t_invs, lkvs, chunk_rows)]
    s_as = [jnp.where(incl, gm[c:, 0:nl], 0.0) for gm in grams]
    s_ks = [jnp.where(incl, gm[c:, nl:], 0.0) for gm in grams]
    block_diag = _head_block_mask(da, hd)
    w_mats = [wu[:, 0:da] for wu in wus]
    u0s = [wu[:, da:] for wu in wus]
    q_effs = [r_t[rows] + _bdot(s_a, _stack_heads(w_mat, nh)) for rows, s_a, w_mat in zip(chunk_rows, s_as, w_mats)]
    o_consts = [_bdot(jnp.concatenate([s_a, s_k], axis=1), jnp.concatenate([_stack_heads(u0, nh), vs], axis=0))
                for s_a, s_k, u0, vs in zip(s_as, s_ks, u0s, vss)]
    s_lins = [jnp.where(block_diag, _bdot_tn(w_mat, a_end[rows]), 0.0) for w_mat, rows in zip(w_mats, chunk_rows)]
    s_consts = [jnp.where(block_diag, _bdot_tn(jnp.concatenate([u0, v[rows]], axis=0),
                                               jnp.concatenate([a_end[rows], k_end[rows]], axis=0)), 0.0)
                for u0, rows in zip(u0s, chunk_rows)]

    s_cur = state[...]
    outs = []
    for ce, q_eff, o_const, s_lin, s_const in zip(cum_ends, q_effs, o_consts, s_lins, s_consts):
        outs.append(_bdot_nt(q_eff, s_cur) + o_const)
        s_cur = s_cur * jnp.exp(ce) + _bdot(s_cur, s_lin) + s_const
    state[...] = s_cur
    o = jnp.concatenate(outs, axis=0)

    inv_hd = 1.0 / hd
    mean = _exact_dot(o, ones_h) * inv_hd
    cen = o - mean
    var = _exact_dot(cen * cen, ones_h) * inv_hd
    o_n = cen * lax.rsqrt(var + RWKV_GN_EPS) * lnw_ref[...] + lnb_ref[...]
    bonus = _exact_dot(r * k2 * rk_ref[...], ones_h) * v
    o_ref[...] = (o_n + bonus) * g


def _rwkv_group(p_a, bsz, seq, w0, w2, a0, a2, g2, k_k, k_a, r_k, lnx_w, lnx_b, tt=SCAN_TILE):
    n, a_cols = p_a.shape
    da = w0.shape[-1]
    n_lora = a_cols - 3 * da
    tt = min(tt, seq)
    nchunk = seq // tt
    d_dec, d_aaa = w2.shape[0], a2.shape[0]
    w2p = jnp.zeros((n_lora, da), F32).at[0:d_dec].set(w2).astype(BF16)
    a2p = jnp.zeros((n_lora, da), F32).at[d_dec:d_dec + d_aaa].set(a2).astype(BF16)
    g2p = jnp.zeros((n_lora, da), F32).at[d_dec + d_aaa:].set(g2).astype(BF16)
    row = lambda t: t.reshape(1, -1).astype(F32)
    vec = lambda w: pl.BlockSpec((1, w), lambda b, c: (0, 0))
    mat = pl.BlockSpec((n_lora, da), lambda b, c: (0, 0))
    nb = min(SCAN_SEQS, bsz)
    seqs = lambda w: pl.BlockSpec((nb, tt, w), lambda b, c: (b, c, 0))
    out = pl.pallas_call(
        _per_sequence(_rwkv_one),
        grid=(bsz // nb, nchunk),
        in_specs=[seqs(a_cols)] + [vec(da)] * 7 + [mat] * 3,
        out_specs=seqs(da),
        out_shape=jax.ShapeDtypeStruct((bsz, seq, da), F32),
        scratch_shapes=[pltpu.VMEM((nb, da, da), F32)],
        compiler_params=_cparams("parallel", "arbitrary"),
        name="rwkv7",
    )(p_a.reshape(bsz, seq, a_cols), row(w0), row(a0), row(k_k), row(k_a), row(r_k), row(lnx_w),
      row(lnx_b), w2p, a2p, g2p)
    return out.reshape(n, da)


def _gdn_one(qkv_ref, z_ref, ba_ref, alog_ref, dtb_ref, nw_ref, o_ref, state):
    c = CHUNK
    tt, db = o_ref.shape
    nh = GDN_HEADS
    hd = db // nh
    nl = nh * c

    q = qkv_ref[:, 0:db]
    k = qkv_ref[:, db:2 * db]
    v = qkv_ref[:, 2 * db:]

    def per_head(col_of_head):
        return jnp.concatenate([jnp.broadcast_to(col_of_head(h), (tt, hd)) for h in range(nh)], axis=1)

    def head_sumsq(x):
        return per_head(lambda h: jnp.sum(x[:, h * hd:(h + 1) * hd] * x[:, h * hd:(h + 1) * hd],
                                          axis=-1, keepdims=True))

    q = q * lax.rsqrt(head_sumsq(q) + 1e-6) * (hd ** -0.5)
    k = k * lax.rsqrt(head_sumsq(k) + 1e-6)

    ba = ba_ref[...]
    beta_l = _sigmoid(ba)
    g_l = -jnp.exp(alog_ref[...]) * _softplus(ba + dtb_ref[...])
    row_in_chunk = _iota((tt, LANES), 0) % c
    gcum_l = g_l
    d = 1
    while d < c:
        gcum_l = gcum_l + jnp.where(row_in_chunk >= d, pltpu.roll(gcum_l, d, 0), 0.0)
        d *= 2
    n_chunks = tt // c
    chunk_rows = [slice(g * c, (g + 1) * c) for g in range(n_chunks)]
    g_last_l = jnp.concatenate([jnp.broadcast_to(gcum_l[(g + 1) * c - 1:(g + 1) * c, :], (c, LANES))
                                for g in range(n_chunks)], axis=0)
    e_g_l = jnp.exp(gcum_l)
    e_tail_l = jnp.exp(g_last_l - gcum_l)
    beta = per_head(lambda h: beta_l[:, h:h + 1])
    e_g = per_head(lambda h: e_g_l[:, nh + h:nh + h + 1])
    e_tail = per_head(lambda h: e_tail_l[:, nh + h:nh + h + 1])
    kb = k * beta
    vb = v * beta
    kbe = kb * e_g
    qe = q * e_g
    k_end = k * e_tail

    strict, incl = _tri_masks(c, nh)
    lane_head = _iota((c, nl), 1) // c
    on_diag = (_iota((c, nl), 1) % c) == _iota((c, nl), 0)
    dmats = []
    for rows in chunk_rows:
        gc = gcum_l[rows]
        gcol = jnp.broadcast_to(gc[:, nh:nh + 1], (c, nl))
        for h in range(1, nh):
            gcol = jnp.where(lane_head == h, jnp.broadcast_to(gc[:, nh + h:nh + h + 1], (c, nl)), gcol)
        grow = jnp.sum(jnp.where(on_diag, gcol, 0.0), axis=0, keepdims=True)
        dmats.append(jnp.exp(jnp.where(incl, gcol - grow, NEG_BIG)))
    grams = [_bdot_nt(jnp.concatenate([kb[rows], q[rows]], axis=0), _stack_heads(k[rows], nh))
             for rows in chunk_rows]
    t_invs = _unit_lower_inverses([jnp.where(strict, -(gm[0:c] * dm), 0.0) for gm, dm in zip(grams, dmats)], nh)
    intras = [jnp.where(incl, gm[c:] * dm, 0.0) for gm, dm in zip(grams, dmats)]
    sols = [_bdot(t_inv, jnp.concatenate([_stack_heads(vb[rows], nh), _stack_heads(kbe[rows], nh)], axis=1))
            for t_inv, rows in zip(t_invs, chunk_rows)]

    s_cur = state[...]
    block_diag = _head_block_mask(db, hd)
    outs = []
    for g, (rows, sol, intra) in enumerate(zip(chunk_rows, sols, intras)):
        u_s, w_s = sol[:, 0:db], sol[:, db:]
        wq = _bdot(jnp.concatenate([w_s, qe[rows]], axis=0), s_cur)
        v_new = u_s - wq[0:c]
        outs.append(wq[c:] + _bdot(intra, _stack_heads(v_new, nh)))
        upd = _bdot_tn(k_end[rows], v_new)
        s_cur = s_cur * e_g[(g + 1) * c - 1:(g + 1) * c, :] + jnp.where(block_diag, upd, 0.0)
    state[...] = s_cur
    o = jnp.concatenate(outs, axis=0)
    o = o * lax.rsqrt(head_sumsq(o) * (1.0 / hd) + NORM_EPS) * nw_ref[...]
    o_ref[...] = o * _silu(z_ref[...])


def _gdn_group(qkv, z, ba, bsz, seq, a_log, dt_bias, norm_w, tt=SCAN_TILE):
    n, w3 = qkv.shape
    db = z.shape[-1]
    nh = GDN_HEADS
    tt = min(tt, seq)
    nchunk = seq // tt
    alog_row = jnp.zeros((1, LANES), F32).at[0, nh:2 * nh].set(a_log)
    dtb_row = jnp.zeros((1, LANES), F32).at[0, nh:2 * nh].set(dt_bias)
    nw_row = jnp.tile(norm_w, nh).reshape(1, db)
    nb = min(SCAN_SEQS, bsz)
    seqs = lambda w: pl.BlockSpec((nb, tt, w), lambda b, c: (b, c, 0))
    const = lambda r, w: pl.BlockSpec((r, w), lambda b, c: (0, 0))
    out = pl.pallas_call(
        _per_sequence(_gdn_one),
        grid=(bsz // nb, nchunk),
        in_specs=[seqs(w3), seqs(db), seqs(LANES), const(1, LANES), const(1, LANES), const(1, db)],
        out_specs=seqs(db),
        out_shape=jax.ShapeDtypeStruct((bsz, seq, db), F32),
        scratch_shapes=[pltpu.VMEM((nb, db, db), F32)],
        compiler_params=_cparams("parallel", "arbitrary"),
        name="gdn",
    )(qkv.reshape(bsz, seq, w3), z.reshape(bsz, seq, db), ba.reshape(bsz, seq, LANES), alog_row, dtb_row, nw_row)
    return out.reshape(n, db)


def _lru_body(p_ref, wax_ref, ba_ref, bx_ref, lam_ref, o_ref, hcar):
    tt = p_ref.shape[0]
    dc = o_ref.shape[-1]

    @pl.when(pl.program_id(1) == 0)
    def _():
        hcar[...] = jnp.zeros_like(hcar)

    conv = p_ref[:, 0:dc]
    gate = p_ref[:, dc:]

    ri = _bdot(conv, wax_ref[...])
    r = _sigmoid(ri[:, 0:dc] + ba_ref[...])
    i = _sigmoid(ri[:, dc:] + bx_ref[...])
    log_a = -LRU_C * r * _softplus(-lam_ref[...])
    a = jnp.exp(log_a)
    mult = jnp.sqrt(jnp.tanh(-log_a) * (a * a + 1.0))
    row = _iota((tt, dc), 0)
    mult = jnp.where((row == 0) & (pl.program_id(1) == 0), 1.0, mult)
    bv = conv * i * mult

    av = a
    d = 1
    while d < tt:
        a_sh = jnp.where(row >= d, pltpu.roll(av, d, 0), 1.0)
        b_sh = jnp.where(row >= d, pltpu.roll(bv, d, 0), 0.0)
        bv = av * b_sh + bv
        av = av * a_sh
        d *= 2
    h = av * hcar[...] + bv
    hcar[...] = h[tt - 1:tt, :]
    gelu = 0.5 * gate * (1.0 + jnp.tanh(0.7978845608028654 * (gate + 0.044715 * gate * gate * gate)))
    o_ref[...] = h * gelu


def _block_diag(w):
    nb, bi, bo = w.shape
    out = jnp.zeros((nb * bi, nb * bo), w.dtype)
    for b in range(nb):
        out = out.at[b * bi:(b + 1) * bi, b * bo:(b + 1) * bo].set(w[b])
    return out


def _lru_group(p_c, bsz, seq, w_a, b_a, w_x, b_x, lam, tt=256):
    n, c_cols = p_c.shape
    dc = c_cols // 2
    tt = min(tt, seq)
    nt = seq // tt
    wax = jnp.concatenate([_block_diag(w_a), _block_diag(w_x)], axis=1).astype(BF16)
    row = lambda t: t.reshape(1, -1).astype(F32)
    const = lambda r, w: pl.BlockSpec((r, w), lambda b, c: (0, 0))
    return pl.pallas_call(
        _lru_body,
        grid=(bsz, nt),
        in_specs=[pl.BlockSpec((tt, c_cols), lambda b, c: (b * nt + c, 0)),
                  const(dc, 2 * dc), const(1, dc), const(1, dc), const(1, dc)],
        out_specs=pl.BlockSpec((tt, dc), lambda b, c: (b * nt + c, 0)),
        out_shape=jax.ShapeDtypeStruct((n, dc), F32),
        scratch_shapes=[pltpu.VMEM((1, dc), F32)],
        compiler_params=_cparams("parallel", "arbitrary"),
        name="rglru",
    )(p_c, wax, row(b_a), row(b_x), row(lam))


def _outproj_body(x_ref, ma_ref, mb_ref, mc_ref, wo_ref, g_ref, wrh_ref, wrl_ref, br_ref,
                  xo_ref, h_ref, ri_ref, rf_ref):
    da = ma_ref.shape[-1]
    db = mb_ref.shape[-1]
    tm = x_ref.shape[0]
    mix = (jnp.dot(ma_ref[...].astype(BF16), wo_ref[0:da, :], preferred_element_type=F32)
           + jnp.dot(mb_ref[...].astype(BF16), wo_ref[da:da + db, :], preferred_element_type=F32)
           + jnp.dot(mc_ref[...].astype(BF16), wo_ref[da + db:, :], preferred_element_type=F32))
    x = x_ref[...] + mix
    xo_ref[...] = x
    h = x * lax.rsqrt(jnp.mean(x * x, axis=-1, keepdims=True) + NORM_EPS) * g_ref[...]
    _store_token_tiles(h_ref, h)

    h_hi = h.astype(BF16)
    h_lo = (h - h_hi.astype(F32)).astype(BF16)
    logits = (jnp.dot(h_hi, wrh_ref[...], preferred_element_type=F32)
              + jnp.dot(h_lo, wrh_ref[...], preferred_element_type=F32)
              + jnp.dot(h_hi, wrl_ref[...], preferred_element_type=F32)) + br_ref[...]
    lane = _iota((tm, LANES), 1)
    far = 4 * LANES
    gmask = lane < MOE_GROUPS
    gl = jnp.where(gmask, logits, NEG_BIG)
    gmax = jnp.max(gl, axis=-1, keepdims=True)
    gidx = jnp.min(jnp.where(gl == gmax, lane, far), axis=-1, keepdims=True)
    gsum = jnp.sum(jnp.where(gmask, jnp.exp(gl - gmax), 0.0), axis=-1, keepdims=True)
    grp_w = 1.0 / gsum
    eidx = lane - MOE_GROUPS
    emask = (eidx >= 0) & (eidx < MOE_EXPERTS) & ((eidx // MOE_PER_GROUP) == gidx)
    el = jnp.where(emask, logits, NEG_BIG)
    v1 = jnp.max(el, axis=-1, keepdims=True)
    i1 = jnp.min(jnp.where(el == v1, lane, far), axis=-1, keepdims=True)
    el2 = jnp.where(lane == i1, NEG_BIG, el)
    v2 = jnp.max(el2, axis=-1, keepdims=True)
    i2 = jnp.min(jnp.where(el2 == v2, lane, far), axis=-1, keepdims=True)
    pr = jnp.exp(v2 - v1)
    g0 = grp_w / (1.0 + pr)
    g1 = g0 * pr
    ri_ref[...] = jnp.where(lane == 0, i1 - MOE_GROUPS, jnp.where(lane == 1, i2 - MOE_GROUPS, 0))
    rf_ref[...] = jnp.where(lane == 0, g0, jnp.where(lane == 1, g1, 0.0))


def _out_proj_router(xt, o_a, o_b, o_c, w_out, gain, w_group, b_group, w_expert, b_expert, tm=512):
    n, d = xt.shape
    tm = min(tm, n)
    ng, ne = w_group.shape[1], w_expert.shape[1]
    wr = jnp.zeros((d, LANES), F32).at[:, 0:ng].set(w_group).at[:, ng:ng + ne].set(w_expert)
    wr_hi = wr.astype(BF16)
    wr_lo = (wr - wr_hi.astype(F32)).astype(BF16)
    br = jnp.zeros((1, LANES), F32).at[0, 0:ng].set(b_group).at[0, ng:ng + ne].set(b_expert)
    rows = lambda w: pl.BlockSpec((tm, w), lambda i: (i, 0))
    const = lambda r, w: pl.BlockSpec((r, w), lambda i: (0, 0))
    return pl.pallas_call(
        _outproj_body,
        grid=(n // tm,),
        in_specs=[rows(d), rows(o_a.shape[1]), rows(o_b.shape[1]), rows(o_c.shape[1]),
                  const(d, d), const(1, d), const(d, LANES), const(d, LANES), const(1, LANES)],
        out_specs=[rows(d), pl.BlockSpec((tm * d // LANES, LANES), lambda i: (i, 0)), rows(LANES), rows(LANES)],
        out_shape=[jax.ShapeDtypeStruct((n, d), F32), jax.ShapeDtypeStruct((n * d // LANES, LANES), F32),
                   jax.ShapeDtypeStruct((n, LANES), jnp.int32), jax.ShapeDtypeStruct((n, LANES), F32)],
        compiler_params=_cparams("parallel"),
        name="out_proj_router",
    )(xt, o_a, o_b, o_c, w_out.astype(BF16), gain.reshape(1, d), wr_hi, wr_lo, br)


def _rank_body(ri_ref, rank_ref, cnt_ref, carry):
    tr = ri_ref.shape[0]

    @pl.when(pl.program_id(0) == 0)
    def _():
        carry[...] = jnp.zeros_like(carry)

    ri = ri_ref[...]
    lane = _iota((tr, LANES), 1)
    oh0 = lane == ri[:, 0:1]
    oh1 = lane == ri[:, 1:2]
    oh = jnp.where(oh0 | oh1, 1.0, 0.0)
    earlier = jnp.where(_iota((tr, tr), 1) < _iota((tr, tr), 0), 1.0, 0.0).astype(BF16)
    cnt = jnp.dot(earlier, oh.astype(BF16), preferred_element_type=F32) + carry[...]
    r0 = jnp.sum(jnp.where(oh0, cnt, 0.0), axis=-1, keepdims=True)
    r1 = jnp.sum(jnp.where(oh1, cnt, 0.0), axis=-1, keepdims=True)
    rank_ref[...] = jnp.where(lane == 0, r0, jnp.where(lane == 1, r1, 0.0)).astype(jnp.int32)
    carry[...] = carry[...] + jnp.sum(oh, axis=0, keepdims=True)
    cnt_ref[...] = carry[...].astype(jnp.int32)


def _expert_ranks(ri, tr=512):
    n = ri.shape[0]
    tr = min(tr, n)
    return pl.pallas_call(
        _rank_body,
        grid=(n // tr,),
        in_specs=[pl.BlockSpec((tr, LANES), lambda i: (i, 0))],
        out_specs=[pl.BlockSpec((tr, LANES), lambda i: (i, 0)), pl.BlockSpec((1, LANES), lambda i: (0, 0))],
        out_shape=[jax.ShapeDtypeStruct((n, LANES), jnp.int32), jax.ShapeDtypeStruct((1, LANES), jnp.int32)],
        scratch_shapes=[pltpu.VMEM((1, LANES), F32)],
        compiler_params=_cparams("arbitrary"),
        name="expert_ranks",
    )(ri)


def _padded_starts(cnt_row):
    cnt = jnp.broadcast_to(cnt_row, (8, LANES))
    padded = (((cnt + (MOE_BLOCK - 1)) // MOE_BLOCK) * MOE_BLOCK).astype(F32)
    before = jnp.where(_iota((LANES, LANES), 0) < _iota((LANES, LANES), 1), 1.0, 0.0).astype(F32)
    start = jnp.dot(padded, before, precision=HIGHEST, preferred_element_type=F32)
    return start, start + padded


def _dest_body(ri_ref, rank_ref, cnt_ref, dest_ref):
    tr = ri_ref.shape[0]
    start, _ = _padded_starts(cnt_ref[...])
    start = start[0:1, :]
    ri = ri_ref[...]
    rank = rank_ref[...]
    lane = _iota((tr, LANES), 1)
    s0 = jnp.sum(jnp.where(lane == ri[:, 0:1], start, 0.0), axis=-1, keepdims=True).astype(jnp.int32)
    s1 = jnp.sum(jnp.where(lane == ri[:, 1:2], start, 0.0), axis=-1, keepdims=True).astype(jnp.int32)
    dest_ref[...] = jnp.where(lane == 0, s0 + rank[:, 0:1], jnp.where(lane == 1, s1 + rank[:, 1:2], 0))


def _destinations(ri, rank, cnt, tr=512):
    n = ri.shape[0]
    tr = min(tr, n)
    rows = pl.BlockSpec((tr, LANES), lambda i: (i, 0))
    return pl.pallas_call(
        _dest_body,
        grid=(n // tr,),
        in_specs=[rows, rows, pl.BlockSpec((1, LANES), lambda i: (0, 0))],
        out_specs=rows,
        out_shape=jax.ShapeDtypeStruct((n, LANES), jnp.int32),
        compiler_params=_cparams("parallel"),
        name="moe_destinations",
    )(ri, rank, cnt)


def _blkexp_body(cnt_ref, be_ref):
    nbp = be_ref.shape[0]
    _, end = _padded_starts(cnt_ref[...])
    end = end[0:1, :]
    lane = _iota((nbp, LANES), 1)
    row_start = (_iota((nbp, LANES), 0) * MOE_BLOCK).astype(F32)
    done = jnp.where((lane < MOE_EXPERTS) & (end <= row_start), 1.0, 0.0)
    be = jnp.minimum(jnp.sum(done, axis=-1, keepdims=True), MOE_EXPERTS - 1.0)
    be_ref[...] = jnp.broadcast_to(be, (nbp, LANES)).astype(jnp.int32)


def _block_experts(cnt, n_blocks):
    nbp = -(-n_blocks // 8) * 8
    out = pl.pallas_call(
        _blkexp_body,
        grid=(1,),
        in_specs=[pl.BlockSpec((1, LANES), lambda i: (0, 0))],
        out_specs=pl.BlockSpec((nbp, LANES), lambda i: (0, 0)),
        out_shape=jax.ShapeDtypeStruct((nbp, LANES), jnp.int32),
        name="moe_block_experts",
    )(cnt)
    return out[:n_blocks, 0]


def _for_each_row(n_rows, fn):
    def lane_row(i, carry):
        for j in range(LANES):
            fn(i * LANES + j, i, j)
        return carry

    lax.fori_loop(0, n_rows // LANES, lane_row, 0)


def _token_tile(ref, row, per):
    return ref.at[pl.ds(pl.multiple_of(row * per, per), per)]


def _scatter_body(per, d0_ref, d1_ref, cnt_ref, h_ref, xb_hbm, zbuf, sem, zsem):
    ts = h_ref.shape[0] // per
    n_rows = xb_hbm.shape[0] // per

    @pl.when(pl.program_id(0) == 0)
    def _():
        zbuf[...] = jnp.zeros_like(zbuf)

        def zero_rows(lo, hi):
            def start(r, carry):
                pltpu.make_async_copy(zbuf, _token_tile(xb_hbm, r, per), zsem).start()
                return carry

            def wait(r, carry):
                pltpu.make_async_copy(zbuf, _token_tile(xb_hbm, r, per), zsem).wait()
                return carry

            lax.fori_loop(lo, hi, start, 0)
            lax.fori_loop(lo, hi, wait, 0)

        def expert(e, slab_start):
            used = cnt_ref[0, e]
            padded = ((used + (MOE_BLOCK - 1)) // MOE_BLOCK) * MOE_BLOCK
            zero_rows(slab_start + used, slab_start + padded)
            return slab_start + padded

        zero_rows(lax.fori_loop(0, MOE_EXPERTS, expert, 0), n_rows)

    def copies(t, i, j):
        src = _token_tile(h_ref, t, per)
        return (pltpu.make_async_copy(src, _token_tile(xb_hbm, d0_ref[i, j], per), sem),
                pltpu.make_async_copy(src, _token_tile(xb_hbm, d1_ref[i, j], per), sem))

    def start(t, i, j):
        for prio, cp in enumerate(copies(t, i, j)):
            cp.start(priority=prio)

    def wait(t, i, j):
        for cp in copies(t, i, j):
            cp.wait()

    _for_each_row(ts, start)
    _for_each_row(ts, wait)


def _scatter_rows(h2, dest0, dest1, cnt, n, n_rows, ts):
    per = h2.shape[0] // n
    smem = pl.BlockSpec((ts // LANES, LANES), lambda i: (i, 0), memory_space=pltpu.SMEM)
    return pl.pallas_call(
        functools.partial(_scatter_body, per),
        grid=(n // ts,),
        in_specs=[smem, smem, pl.BlockSpec((1, LANES), lambda i: (0, 0), memory_space=pltpu.SMEM),
                  pl.BlockSpec((ts * per, LANES), lambda i: (i, 0))],
        out_specs=pl.BlockSpec(memory_space=pl.ANY),
        out_shape=jax.ShapeDtypeStruct((n_rows * per, LANES), F32),
        scratch_shapes=[pltpu.VMEM((per, LANES), F32), pltpu.SemaphoreType.DMA(()), pltpu.SemaphoreType.DMA(())],
        compiler_params=_cparams("arbitrary"),
        name="moe_scatter",
    )(dest0, dest1, cnt, h2)


def _expert_body(be_ref, x_ref, wgu_ref, wd_ref, y_ref):
    del be_ref
    de = wd_ref.shape[1]
    x = _load_token_tiles(x_ref, MOE_BLOCK)
    gu = jnp.dot(x.astype(BF16), wgu_ref[0], preferred_element_type=F32)
    hid = _silu(gu[:, 0:de]) * gu[:, de:]
    _store_token_tiles(y_ref, jnp.dot(hid.astype(BF16), wd_ref[0], preferred_element_type=F32))


def _expert_mlp(xb, blk_expert, w_gu, w_down):
    de, d = w_down.shape[1:]
    per = d // LANES
    nb = xb.shape[0] // (MOE_BLOCK * per)
    tiles = pl.BlockSpec((MOE_BLOCK * per, LANES), lambda i, be: (i, 0))
    grid_spec = pltpu.PrefetchScalarGridSpec(
        num_scalar_prefetch=1,
        grid=(nb,),
        in_specs=[tiles,
                  pl.BlockSpec((1, d, 2 * de), lambda i, be: (be[i], 0, 0)),
                  pl.BlockSpec((1, de, d), lambda i, be: (be[i], 0, 0))],
        out_specs=tiles,
    )
    return pl.pallas_call(
        _expert_body,
        grid_spec=grid_spec,
        out_shape=jax.ShapeDtypeStruct(xb.shape, F32),
        compiler_params=_cparams("arbitrary"),
        name="moe_experts",
    )(blk_expert, xb, w_gu, w_down)


def _combine_body(final_norm, d0_ref, d1_ref, rf_ref, x_ref, g_ref, yb_hbm, o_ref, gbuf, sem):
    ts, d = x_ref.shape
    per = d // LANES

    def copies(t, i, j):
        return (pltpu.make_async_copy(_token_tile(yb_hbm, d0_ref[i, j], per), _token_tile(gbuf.at[0], t, per), sem),
                pltpu.make_async_copy(_token_tile(yb_hbm, d1_ref[i, j], per), _token_tile(gbuf.at[1], t, per), sem))

    def start(t, i, j):
        for prio, cp in enumerate(copies(t, i, j)):
            cp.start(priority=prio)

    def wait(t, i, j):
        for cp in copies(t, i, j):
            cp.wait()

    _for_each_row(ts, start)
    _for_each_row(ts, wait)
    rf = rf_ref[...]
    x = x_ref[...] + (_load_token_tiles(gbuf.at[0], ts) * rf[:, 0:1] + _load_token_tiles(gbuf.at[1], ts) * rf[:, 1:2])
    if final_norm:
        x = x * lax.rsqrt(jnp.mean(x * x, axis=-1, keepdims=True) + NORM_EPS) * g_ref[...]
    o_ref[...] = x


def _combine(x_mid, yb, dest0, dest1, rf, final_gain, ts):
    n, d = x_mid.shape
    final_norm = final_gain is not None
    gain = (final_gain if final_norm else jnp.ones((d,), F32)).reshape(1, d)
    smem = pl.BlockSpec((ts // LANES, LANES), lambda i: (i, 0), memory_space=pltpu.SMEM)
    rows = lambda w: pl.BlockSpec((ts, w), lambda i: (i, 0))
    return pl.pallas_call(
        functools.partial(_combine_body, final_norm),
        grid=(n // ts,),
        in_specs=[smem, smem, rows(LANES), rows(d), pl.BlockSpec((1, d), lambda i: (0, 0)),
                  pl.BlockSpec(memory_space=pl.ANY)],
        out_specs=rows(d),
        out_shape=jax.ShapeDtypeStruct((n, d), F32),
        scratch_shapes=[pltpu.VMEM((2, ts * d // LANES, LANES), F32), pltpu.SemaphoreType.DMA(())],
        compiler_params=_cparams("arbitrary"),
        name="moe_combine",
    )(dest0, dest1, rf, x_mid, gain, yb)


def _moe(x_mid, h2, ri, rf, w_gate, w_up, w_down, final_gain):
    n, d = x_mid.shape
    n_assign = 2 * n
    n_blocks = (n_assign + MOE_EXPERTS * (MOE_BLOCK - 1) + MOE_BLOCK - 1) // MOE_BLOCK
    ts = min(1024, n)
    rank, cnt = _expert_ranks(ri)
    dest = _destinations(ri, rank, cnt)
    blk_expert = _block_experts(cnt, n_blocks)
    dest0 = dest[:, 0].reshape(n // LANES, LANES)
    dest1 = dest[:, 1].reshape(n // LANES, LANES)
    xb = _scatter_rows(h2, dest0, dest1, cnt, n, n_blocks * MOE_BLOCK, ts)
    w_gu = jnp.concatenate([w_gate, w_up], axis=-1).astype(BF16)
    yb = _expert_mlp(xb, blk_expert, w_gu, w_down.astype(BF16))
    return _combine(x_mid, yb, dest0, dest1, rf, final_gain, ts)


def kernel(x, norm_mix, norm_ffn, norm_final, w_in, w_out, rwkv_mu, rwkv_w0, rwkv_w2, rwkv_a0, rwkv_a2, rwkv_g2, rwkv_k_k, rwkv_k_a, rwkv_r_k, rwkv_lnx_w, rwkv_lnx_b, gdn_conv_w, gdn_a_log, gdn_dt_bias, gdn_norm_w, lru_conv_w, lru_conv_b, lru_w_a, lru_b_a, lru_w_x, lru_b_x, lru_lambda, moe_w_group, moe_b_group, moe_w_expert, moe_b_expert, moe_w_gate, moe_w_up, moe_w_down):
    bsz, seq, d = x.shape
    depth = w_in.shape[0]
    n = bsz * seq
    a_cols = rwkv_mu.shape[1]
    db = gdn_norm_w.shape[1] * GDN_HEADS
    dc = lru_conv_b.shape[1]
    n_ba = 2 * GDN_HEADS
    widths = (a_cols, 3 * db, db, 2 * dc, LANES)
    xt = x.reshape(n, d)
    for l in range(depth):
        w = w_in[l]
        b0 = a_cols
        c0 = b0 + 4 * db + n_ba
        w_cat = jnp.concatenate(
            [w[:, 0:a_cols], w[:, b0:b0 + 4 * db], w[:, c0:c0 + 2 * dc],
             w[:, b0 + 4 * db:c0], jnp.zeros((d, LANES - n_ba), F32)], axis=1).astype(BF16)
        p_a, p_qkv, p_z, p_c, p_ba = _in_proj(xt, seq, norm_mix[l], w_cat, widths, rwkv_mu[l], gdn_conv_w[l],
                                              lru_conv_w[l], lru_conv_b[l])
        o_a = _rwkv_group(p_a, bsz, seq, rwkv_w0[l], rwkv_w2[l], rwkv_a0[l], rwkv_a2[l], rwkv_g2[l],
                          rwkv_k_k[l], rwkv_k_a[l], rwkv_r_k[l], rwkv_lnx_w[l], rwkv_lnx_b[l])
        o_b = _gdn_group(p_qkv, p_z, p_ba, bsz, seq, gdn_a_log[l], gdn_dt_bias[l], gdn_norm_w[l])
        o_c = _lru_group(p_c, bsz, seq, lru_w_a[l], lru_b_a[l], lru_w_x[l], lru_b_x[l], lru_lambda[l])
        x_mid, h2, ri, rf = _out_proj_router(xt, o_a, o_b, o_c, w_out[l], norm_ffn[l], moe_w_group[l],
                                             moe_b_group[l], moe_w_expert[l], moe_b_expert[l])
        xt = _moe(x_mid, h2, ri, rf, moe_w_gate[l], moe_w_up[l], moe_w_down[l],
                  norm_final if l == depth - 1 else None)
    return xt.reshape(bsz, seq, d)
```

```python
import functools

import jax
import jax.numpy as jnp
from jax import lax
from jax.experimental import pallas as pl
from jax.experimental.pallas import tpu as pltpu

F32 = jnp.float32
BF16 = jnp.bfloat16
HIGHEST = lax.Precision.HIGHEST

NORM_EPS = 1e-6
CONV_WIDTH = 4
CHUNK = 64
SCAN_TILE = 256
SCAN_SEQS = 2
RWKV_HEADS = 4
RWKV_GN_EPS = 64e-5
GDN_HEADS = 4
LRU_C = 8.0
MOE_GROUPS = 4
MOE_PER_GROUP = 8
MOE_EXPERTS = MOE_GROUPS * MOE_PER_GROUP
MOE_BLOCK = 256
LANES = 128
NEG_BIG = -1e30
VMEM_LIMIT = 56 * 1024 * 1024


def _cparams(*sem):
    return pltpu.CompilerParams(dimension_semantics=sem, vmem_limit_bytes=VMEM_LIMIT)


def _bdot(a, b):
    return jnp.dot(a.astype(BF16), b.astype(BF16), preferred_element_type=F32)


def _bdot_nt(a, b):
    return lax.dot_general(a.astype(BF16), b.astype(BF16), (((1,), (1,)), ((), ())),
                           preferred_element_type=F32)


def _bdot_tn(a, b):
    return lax.dot_general(a.astype(BF16), b.astype(BF16), (((0,), (0,)), ((), ())),
                           preferred_element_type=F32)


def _exact_dot(a, b01):
    hi = a.astype(BF16)
    lo = (a - hi.astype(F32)).astype(BF16)
    return (jnp.dot(hi, b01, preferred_element_type=F32)
            + jnp.dot(lo, b01, preferred_element_type=F32))


def _exact_dot_left(a01, b):
    hi = b.astype(BF16)
    lo = (b - hi.astype(F32)).astype(BF16)
    return (jnp.dot(a01, hi, preferred_element_type=F32)
            + jnp.dot(a01, lo, preferred_element_type=F32))


def _load_token_tiles(ref, rows):
    per = ref.shape[0] // rows
    return jnp.concatenate([ref[pl.ds(c, rows, stride=per), :] for c in range(per)], axis=1)


def _store_token_tiles(ref, value):
    rows, d = value.shape
    per = d // LANES
    for c in range(per):
        ref[pl.ds(c, rows, stride=per), :] = value[:, c * LANES:(c + 1) * LANES]


def _softplus(x):
    return jnp.maximum(x, 0.0) + jnp.log1p(jnp.exp(-jnp.abs(x)))


def _sigmoid(x):
    return 1.0 / (1.0 + jnp.exp(-x))


def _silu(x):
    return x * _sigmoid(x)


def _iota(shape, dim):
    return lax.broadcasted_iota(jnp.int32, shape, dim)


def _head_ones(width, head_dim):
    r = _iota((width, width), 0) // head_dim
    c = _iota((width, width), 1) // head_dim
    return jnp.where(r == c, 1.0, 0.0).astype(BF16)


def _stack_heads(x, n_heads):
    c, w = x.shape
    hd = w // n_heads
    xb = x.astype(BF16)
    lane_head = _iota((c, w), 1) // hd
    return jnp.concatenate([jnp.where(lane_head == h, xb, jnp.zeros_like(xb)) for h in range(n_heads)],
                           axis=0)


def _tri_masks(c, n_heads):
    r = _iota((c, n_heads * c), 0)
    s = _iota((c, n_heads * c), 1) % c
    return s < r, s <= r


def _unit_lower_inverses(l_mats, n_heads):
    shape = l_mats[0].shape
    c = shape[0]
    eye = jnp.where(_iota(shape, 1) % c == _iota(shape, 0), 1.0, 0.0).astype(F32)
    ps = [eye + l for l in l_mats]
    ms = [_bdot(l, _stack_heads(l, n_heads)) for l in l_mats]
    span = 2
    while span < c:
        last = span * 2 >= c
        nxt = [_bdot(p if last else jnp.concatenate([m, p], axis=0), _stack_heads(m, n_heads))
               for m, p in zip(ms, ps)]
        ps = [p + (r if last else r[c:]) for p, r in zip(ps, nxt)]
        ms = [None if last else r[0:c] for r in nxt]
        span *= 2
    return ps


def _head_block_mask(width, head_dim):
    r = _iota((width, width), 0) // head_dim
    c = _iota((width, width), 1) // head_dim
    return r == c


def _shift_rows(y, tail, j):
    rolled = pltpu.roll(y, j, 0)
    head = jnp.where(_iota(tail.shape, 0) < j, pltpu.roll(tail, j, 0), rolled[0:8])
    return jnp.concatenate([head, rolled[8:]], axis=0)


def _causal_conv(y, tail, cw):
    acc = cw[CONV_WIDTH - 1:CONV_WIDTH, :] * y
    for j in range(1, CONV_WIDTH):
        acc = acc + cw[CONV_WIDTH - 1 - j:CONV_WIDTH - j, :] * _shift_rows(y, tail, j)
    return acc


def _inproj_body(tiles_per_seq, x_ref, g_ref, w_ref, mu_ref, gcw_ref, lcw_ref, lcb_ref,
                 a_ref, qkv_ref, z_ref, c_ref, ba_ref, atail, qtail, ctail):
    tm = x_ref.shape[0]
    tails = (atail, qtail, ctail)

    @pl.when(pl.program_id(0) % tiles_per_seq == 0)
    def _():
        for tail in tails:
            tail[...] = jnp.zeros_like(tail)

    x = x_ref[...]
    h = x * lax.rsqrt(jnp.mean(x * x, axis=-1, keepdims=True) + NORM_EPS) * g_ref[...]
    hb = h.astype(BF16)
    wa, wq, wz, wc = a_ref.shape[-1], qkv_ref.shape[-1], z_ref.shape[-1], c_ref.shape[-1]
    dc = ctail.shape[1]

    def proj(off, width):
        return jnp.dot(hb, w_ref[:, off:off + width], preferred_element_type=F32)

    for s in range(3):
        cols = slice(s * (wq // 3), (s + 1) * (wq // 3))
        y = proj(wa + cols.start, wq // 3)
        qkv_ref[:, cols] = _silu(_causal_conv(y, qtail[:, cols], gcw_ref[:, cols]))
        qtail[:, cols] = y[tm - 8:tm, :]
    p = proj(0, wa)
    a_ref[...] = p + (_shift_rows(p, atail[...], 1) - p) * mu_ref[...]
    atail[...] = p[tm - 8:tm, :]
    z_ref[...] = proj(wa + wq, wz)
    pc = proj(wa + wq + wz, wc)
    c_ref[:, 0:dc] = lcb_ref[...] + _causal_conv(pc[:, 0:dc], ctail[...], lcw_ref[...])
    c_ref[:, dc:] = pc[:, dc:]
    ctail[...] = pc[tm - 8:tm, 0:dc]
    ba_ref[...] = proj(wa + wq + wz + wc, ba_ref.shape[-1])


def _in_proj(xt, seq, gain, w_cat, widths, mu, gdn_conv_w, lru_conv_w, lru_conv_b, tm=512):
    n, d = xt.shape
    ntot = w_cat.shape[1]
    tm = min(tm, seq)
    dc = lru_conv_b.shape[0]
    const = lambda r, w: pl.BlockSpec((r, w), lambda i: (0, 0))
    return pl.pallas_call(
        functools.partial(_inproj_body, seq // tm),
        grid=(n // tm,),
        in_specs=[pl.BlockSpec((tm, d), lambda i: (i, 0)), const(1, d), const(d, ntot),
                  const(1, widths[0]), const(CONV_WIDTH, widths[1]), const(CONV_WIDTH, dc), const(1, dc)],
        out_specs=[pl.BlockSpec((tm, w), lambda i: (i, 0)) for w in widths],
        out_shape=[jax.ShapeDtypeStruct((n, w), F32) for w in widths],
        scratch_shapes=[pltpu.VMEM((8, widths[0]), F32), pltpu.VMEM((8, widths[1]), F32),
                        pltpu.VMEM((8, dc), F32)],
        compiler_params=_cparams("arbitrary"),
        name="in_proj",
    )(xt, gain.reshape(1, d), w_cat, mu.reshape(1, -1), gdn_conv_w, lru_conv_w, lru_conv_b.reshape(1, dc))


def _rwkv_body(xs_ref, w0_ref, a0_ref, kk_ref, ka_ref, rk_ref, lnw_ref, lnb_ref,
               w2_ref, a2_ref, g2_ref, o_ref, state):
    c = CHUNK
    n_seq, tt, da = o_ref.shape
    nh = RWKV_HEADS
    hd = da // nh

    @pl.when(pl.program_id(1) == 0)
    def _():
        state[...] = jnp.zeros_like(state)

    xs = jnp.concatenate([xs_ref[s] for s in range(n_seq)], axis=0)
    r = xs[:, 0:da]
    k = xs[:, da:2 * da]
    v = xs[:, 2 * da:3 * da]
    lo = xs[:, 3 * da:]

    w_log = -_softplus(-(w0_ref[...] + _bdot(jnp.tanh(lo), w2_ref[...]))) - 0.5
    lw = -jnp.exp(w_log)
    a = _sigmoid(a0_ref[...] + _bdot(lo, a2_ref[...]))
    g = _bdot(_sigmoid(lo), g2_ref[...])

    ones_h = _head_ones(da, hd)
    kkr = k * kk_ref[...]
    kk = kkr * lax.rsqrt(_exact_dot(kkr * kkr, ones_h) + 1e-6)
    k2 = k * (1.0 + (a - 1.0) * ka_ref[...])
    alpha = -(kk * a)

    rr = _iota((tt, tt), 0)
    cc = _iota((tt, tt), 1)
    tri = jnp.where((cc <= rr) & (cc // c == rr // c), 1.0, 0.0).astype(BF16)
    cum = jnp.concatenate([_exact_dot_left(tri, lw[s * tt:(s + 1) * tt]) for s in range(n_seq)], axis=0)
    per_seq = tt // c
    n_chunks = n_seq * per_seq
    chunk_rows = [slice(g * c, (g + 1) * c) for g in range(n_chunks)]
    cum_ends = [cum[(g + 1) * c - 1:(g + 1) * c, :] for g in range(n_chunks)]
    cum_last = jnp.concatenate([jnp.broadcast_to(ce, (c, da)) for ce in cum_ends], axis=0)
    e_pos = jnp.exp(cum)
    e_neg = jnp.exp(-cum)
    e_tail = jnp.exp(cum_last - cum)
    r_t = r * e_pos
    b_t = kk * jnp.exp(cum - lw)
    a_t = alpha * e_neg
    k_t = k2 * e_neg
    a_end = alpha * e_tail
    k_end = k2 * e_tail

    nl = nh * c
    strict, incl = _tri_masks(c, nh)
    grams = [_bdot_nt(jnp.concatenate([b_t[rows], r_t[rows]], axis=0),
                      jnp.concatenate([_stack_heads(a_t[rows], nh), _stack_heads(k_t[rows], nh)], axis=0))
             for rows in chunk_rows]
    t_invs = _unit_lower_inverses([jnp.where(strict, gm[0:c, 0:nl], 0.0) for gm in grams], nh)
    vss = [_stack_heads(v[rows], nh) for rows in chunk_rows]
    lkvs = [_bdot(jnp.where(strict, gm[0:c, nl:], 0.0), vs) for gm, vs in zip(grams, vss)]
    wus = [_bdot(t_inv, jnp.concatenate([_stack_heads(b_t[rows], nh), _stack_heads(lkv, nh)], axis=1))
           for t_inv, lkv, rows in zip(t_invs, lkvs, chunk_rows)]
    s_as = [jnp.where(incl, gm[c:, 0:nl], 0.0) for gm in grams]
    s_ks = [jnp.where(incl, gm[c:, nl:], 0.0) for gm in grams]
    block_diag = _head_block_mask(da, hd)
    w_mats = [wu[:, 0:da] for wu in wus]
    u0s = [wu[:, da:] for wu in wus]
    q_effs = [r_t[rows] + _bdot(s_a, _stack_heads(w_mat, nh)) for rows, s_a, w_mat in zip(chunk_rows, s_as, w_mats)]
    o_consts = [_bdot(jnp.concatenate([s_a, s_k], axis=1), jnp.concatenate([_stack_heads(u0, nh), vs], axis=0))
                for s_a, s_k, u0, vs in zip(s_as, s_ks, u0s, vss)]
    s_lins = [jnp.where(block_diag, _bdot_tn(w_mat, a_end[rows]), 0.0) for w_mat, rows in zip(w_mats, chunk_rows)]
    s_consts = [jnp.where(block_diag, _bdot_tn(jnp.concatenate([u0, v[rows]], axis=0),
                                               jnp.concatenate([a_end[rows], k_end[rows]], axis=0)), 0.0)
                for u0, rows in zip(u0s, chunk_rows)]

    s_curs = [state[s] for s in range(n_seq)]
    outs = [None] * n_chunks
    for step in range(per_seq):
        for s in range(n_seq):
            i = s * per_seq + step
            outs[i] = _bdot_nt(q_effs[i], s_curs[s]) + o_consts[i]
            s_curs[s] = s_curs[s] * jnp.exp(cum_ends[i]) + _bdot(s_curs[s], s_lins[i]) + s_consts[i]
    for s in range(n_seq):
        state[s] = s_curs[s]
    o = jnp.concatenate(outs, axis=0)

    inv_hd = 1.0 / hd
    mean = _exact_dot(o, ones_h) * inv_hd
    cen = o - mean
    var = _exact_dot(cen * cen, ones_h) * inv_hd
    o_n = cen * lax.rsqrt(var + RWKV_GN_EPS) * lnw_ref[...] + lnb_ref[...]
    bonus = _exact_dot(r * k2 * rk_ref[...], ones_h) * v
    res = (o_n + bonus) * g
    for s in range(n_seq):
        o_ref[s] = res[s * tt:(s + 1) * tt]


def _rwkv_group(p_a, bsz, seq, w0, w2, a0, a2, g2, k_k, k_a, r_k, lnx_w, lnx_b, tt=SCAN_TILE):
    n, a_cols = p_a.shape
    da = w0.shape[-1]
    n_lora = a_cols - 3 * da
    tt = min(tt, seq)
    nchunk = seq // tt
    d_dec, d_aaa = w2.shape[0], a2.shape[0]
    w2p = jnp.zeros((n_lora, da), F32).at[0:d_dec].set(w2).astype(BF16)
    a2p = jnp.zeros((n_lora, da), F32).at[d_dec:d_dec + d_aaa].set(a2).astype(BF16)
    g2p = jnp.zeros((n_lora, da), F32).at[d_dec + d_aaa:].set(g2).astype(BF16)
    row = lambda t: t.reshape(1, -1).astype(F32)
    vec = lambda w: pl.BlockSpec((1, w), lambda b, c: (0, 0))
    mat = pl.BlockSpec((n_lora, da), lambda b, c: (0, 0))
    nb = min(SCAN_SEQS, bsz)
    seqs = lambda w: pl.BlockSpec((nb, tt, w), lambda b, c: (b, c, 0))
    out = pl.pallas_call(
        _rwkv_body,
        grid=(bsz // nb, nchunk),
        in_specs=[seqs(a_cols)] + [vec(da)] * 7 + [mat] * 3,
        out_specs=seqs(da),
        out_shape=jax.ShapeDtypeStruct((bsz, seq, da), F32),
        scratch_shapes=[pltpu.VMEM((nb, da, da), F32)],
        compiler_params=_cparams("parallel", "arbitrary"),
        name="rwkv7",
    )(p_a.reshape(bsz, seq, a_cols), row(w0), row(a0), row(k_k), row(k_a), row(r_k), row(lnx_w),
      row(lnx_b), w2p, a2p, g2p)
    return out.reshape(n, da)


def _gdn_body(qkv_ref, z_ref, ba_ref, alog_ref, dtb_ref, nw_ref, o_ref, state):
    c = CHUNK
    n_seq, tt, db = o_ref.shape
    rows_all = n_seq * tt
    nh = GDN_HEADS
    hd = db // nh
    nl = nh * c

    @pl.when(pl.program_id(1) == 0)
    def _():
        state[...] = jnp.zeros_like(state)

    def all_rows(ref, cols=slice(None)):
        return jnp.concatenate([ref[s, :, cols] for s in range(n_seq)], axis=0)

    q = all_rows(qkv_ref, slice(0, db))
    k = all_rows(qkv_ref, slice(db, 2 * db))
    v = all_rows(qkv_ref, slice(2 * db, 3 * db))

    def per_head(col_of_head):
        return jnp.concatenate([jnp.broadcast_to(col_of_head(h), (rows_all, hd)) for h in range(nh)], axis=1)

    def head_sumsq(x):
        return per_head(lambda h: jnp.sum(x[:, h * hd:(h + 1) * hd] * x[:, h * hd:(h + 1) * hd],
                                          axis=-1, keepdims=True))

    q = q * lax.rsqrt(head_sumsq(q) + 1e-6) * (hd ** -0.5)
    k = k * lax.rsqrt(head_sumsq(k) + 1e-6)

    ba = all_rows(ba_ref)
    beta_l = _sigmoid(ba)
    g_l = -jnp.exp(alog_ref[...]) * _softplus(ba + dtb_ref[...])
    row_in_chunk = _iota((rows_all, LANES), 0) % c
    gcum_l = g_l
    d = 1
    while d < c:
        gcum_l = gcum_l + jnp.where(row_in_chunk >= d, pltpu.roll(gcum_l, d, 0), 0.0)
        d *= 2
    per_seq = tt // c
    n_chunks = n_seq * per_seq
    chunk_rows = [slice(g * c, (g + 1) * c) for g in range(n_chunks)]
    g_last_l = jnp.concatenate([jnp.broadcast_to(gcum_l[(g + 1) * c - 1:(g + 1) * c, :], (c, LANES))
                                for g in range(n_chunks)], axis=0)
    e_g_l = jnp.exp(gcum_l)
    e_tail_l = jnp.exp(g_last_l - gcum_l)
    beta = per_head(lambda h: beta_l[:, h:h + 1])
    e_g = per_head(lambda h: e_g_l[:, nh + h:nh + h + 1])
    e_tail = per_head(lambda h: e_tail_l[:, nh + h:nh + h + 1])
    kb = k * beta
    vb = v * beta
    kbe = kb * e_g
    qe = q * e_g
    k_end = k * e_tail

    strict, incl = _tri_masks(c, nh)
    lane_head = _iota((c, nl), 1) // c
    on_diag = (_iota((c, nl), 1) % c) == _iota((c, nl), 0)
    dmats = []
    for rows in chunk_rows:
        gc = gcum_l[rows]
        gcol = jnp.broadcast_to(gc[:, nh:nh + 1], (c, nl))
        for h in range(1, nh):
            gcol = jnp.where(lane_head == h, jnp.broadcast_to(gc[:, nh + h:nh + h + 1], (c, nl)), gcol)
        grow = jnp.sum(jnp.where(on_diag, gcol, 0.0), axis=0, keepdims=True)
        dmats.append(jnp.exp(jnp.where(incl, gcol - grow, NEG_BIG)))
    grams = [_bdot_nt(jnp.concatenate([kb[rows], q[rows]], axis=0), _stack_heads(k[rows], nh))
             for rows in chunk_rows]
    t_invs = _unit_lower_inverses([jnp.where(strict, -(gm[0:c] * dm), 0.0) for gm, dm in zip(grams, dmats)], nh)
    intras = [jnp.where(incl, gm[c:] * dm, 0.0) for gm, dm in zip(grams, dmats)]
    sols = [_bdot(t_inv, jnp.concatenate([_stack_heads(vb[rows], nh), _stack_heads(kbe[rows], nh)], axis=1))
            for t_inv, rows in zip(t_invs, chunk_rows)]

    s_curs = [state[s] for s in range(n_seq)]
    block_diag = _head_block_mask(db, hd)
    outs = [None] * n_chunks
    for g in range(per_seq):
        for s in range(n_seq):
            i = s * per_seq + g
            rows = chunk_rows[i]
            u_s, w_s = sols[i][:, 0:db], sols[i][:, db:]
            wq = _bdot(jnp.concatenate([w_s, qe[rows]], axis=0), s_curs[s])
            v_new = u_s - wq[0:c]
            outs[i] = wq[c:] + _bdot(intras[i], _stack_heads(v_new, nh))
            upd = _bdot_tn(k_end[rows], v_new)
            s_curs[s] = s_curs[s] * e_g[(i + 1) * c - 1:(i + 1) * c, :] + jnp.where(block_diag, upd, 0.0)
    for s in range(n_seq):
        state[s] = s_curs[s]
    o = jnp.concatenate(outs, axis=0)
    o = o * lax.rsqrt(head_sumsq(o) * (1.0 / hd) + NORM_EPS) * nw_ref[...]
    res = o * _silu(all_rows(z_ref))
    for s in range(n_seq):
        o_ref[s] = res[s * tt:(s + 1) * tt]


def _gdn_group(qkv, z, ba, bsz, seq, a_log, dt_bias, norm_w, tt=SCAN_TILE):
    n, w3 = qkv.shape
    db = z.shape[-1]
    nh = GDN_HEADS
    tt = min(tt, seq)
    nchunk = seq // tt
    alog_row = jnp.zeros((1, LANES), F32).at[0, nh:2 * nh].set(a_log)
    dtb_row = jnp.zeros((1, LANES), F32).at[0, nh:2 * nh].set(dt_bias)
    nw_row = jnp.tile(norm_w, nh).reshape(1, db)
    nb = min(SCAN_SEQS, bsz)
    seqs = lambda w: pl.BlockSpec((nb, tt, w), lambda b, c: (b, c, 0))
    const = lambda r, w: pl.BlockSpec((r, w), lambda b, c: (0, 0))
    out = pl.pallas_call(
        _gdn_body,
        grid=(bsz // nb, nchunk),
        in_specs=[seqs(w3), seqs(db), seqs(LANES), const(1, LANES), const(1, LANES), const(1, db)],
        out_specs=seqs(db),
        out_shape=jax.ShapeDtypeStruct((bsz, seq, db), F32),
        scratch_shapes=[pltpu.VMEM((nb, db, db), F32)],
        compiler_params=_cparams("parallel", "arbitrary"),
        name="gdn",
    )(qkv.reshape(bsz, seq, w3), z.reshape(bsz, seq, db), ba.reshape(bsz, seq, LANES), alog_row, dtb_row, nw_row)
    return out.reshape(n, db)


def _lru_body(p_ref, wax_ref, ba_ref, bx_ref, lam_ref, o_ref, hcar):
    tt = p_ref.shape[0]
    dc = o_ref.shape[-1]

    @pl.when(pl.program_id(1) == 0)
    def _():
        hcar[...] = jnp.zeros_like(hcar)

    conv = p_ref[:, 0:dc]
    gate = p_ref[:, dc:]

    ri = _bdot(conv, wax_ref[...])
    r = _sigmoid(ri[:, 0:dc] + ba_ref[...])
    i = _sigmoid(ri[:, dc:] + bx_ref[...])
    log_a = -LRU_C * r * _softplus(-lam_ref[...])
    a = jnp.exp(log_a)
    mult = jnp.sqrt(jnp.tanh(-log_a) * (a * a + 1.0))
    row = _iota((tt, dc), 0)
    mult = jnp.where((row == 0) & (pl.program_id(1) == 0), 1.0, mult)
    bv = conv * i * mult

    av = a
    d = 1
    while d < tt:
        a_sh = jnp.where(row >= d, pltpu.roll(av, d, 0), 1.0)
        b_sh = jnp.where(row >= d, pltpu.roll(bv, d, 0), 0.0)
        bv = av * b_sh + bv
        av = av * a_sh
        d *= 2
    h = av * hcar[...] + bv
    hcar[...] = h[tt - 1:tt, :]
    gelu = 0.5 * gate * (1.0 + jnp.tanh(0.7978845608028654 * (gate + 0.044715 * gate * gate * gate)))
    o_ref[...] = h * gelu


def _block_diag(w):
    nb, bi, bo = w.shape
    out = jnp.zeros((nb * bi, nb * bo), w.dtype)
    for b in range(nb):
        out = out.at[b * bi:(b + 1) * bi, b * bo:(b + 1) * bo].set(w[b])
    return out


def _lru_group(p_c, bsz, seq, w_a, b_a, w_x, b_x, lam, tt=256):
    n, c_cols = p_c.shape
    dc = c_cols // 2
    tt = min(tt, seq)
    nt = seq // tt
    wax = jnp.concatenate([_block_diag(w_a), _block_diag(w_x)], axis=1).astype(BF16)
    row = lambda t: t.reshape(1, -1).astype(F32)
    const = lambda r, w: pl.BlockSpec((r, w), lambda b, c: (0, 0))
    return pl.pallas_call(
        _lru_body,
        grid=(bsz, nt),
        in_specs=[pl.BlockSpec((tt, c_cols), lambda b, c: (b * nt + c, 0)),
                  const(dc, 2 * dc), const(1, dc), const(1, dc), const(1, dc)],
        out_specs=pl.BlockSpec((tt, dc), lambda b, c: (b * nt + c, 0)),
        out_shape=jax.ShapeDtypeStruct((n, dc), F32),
        scratch_shapes=[pltpu.VMEM((1, dc), F32)],
        compiler_params=_cparams("parallel", "arbitrary"),
        name="rglru",
    )(p_c, wax, row(b_a), row(b_x), row(lam))


def _outproj_body(x_ref, ma_ref, mb_ref, mc_ref, wo_ref, g_ref, wrh_ref, wrl_ref, br_ref,
                  xo_ref, h_ref, ri_ref, rf_ref):
    da = ma_ref.shape[-1]
    db = mb_ref.shape[-1]
    tm = x_ref.shape[0]
    mix = (jnp.dot(ma_ref[...].astype(BF16), wo_ref[0:da, :], preferred_element_type=F32)
           + jnp.dot(mb_ref[...].astype(BF16), wo_ref[da:da + db, :], preferred_element_type=F32)
           + jnp.dot(mc_ref[...].astype(BF16), wo_ref[da + db:, :], preferred_element_type=F32))
    x = x_ref[...] + mix
    xo_ref[...] = x
    h = x * lax.rsqrt(jnp.mean(x * x, axis=-1, keepdims=True) + NORM_EPS) * g_ref[...]
    _store_token_tiles(h_ref, h)

    h_hi = h.astype(BF16)
    h_lo = (h - h_hi.astype(F32)).astype(BF16)
    logits = (jnp.dot(h_hi, wrh_ref[...], preferred_element_type=F32)
              + jnp.dot(h_lo, wrh_ref[...], preferred_element_type=F32)
              + jnp.dot(h_hi, wrl_ref[...], preferred_element_type=F32)) + br_ref[...]
    lane = _iota((tm, LANES), 1)
    far = 4 * LANES
    gmask = lane < MOE_GROUPS
    gl = jnp.where(gmask, logits, NEG_BIG)
    gmax = jnp.max(gl, axis=-1, keepdims=True)
    gidx = jnp.min(jnp.where(gl == gmax, lane, far), axis=-1, keepdims=True)
    gsum = jnp.sum(jnp.where(gmask, jnp.exp(gl - gmax), 0.0), axis=-1, keepdims=True)
    grp_w = 1.0 / gsum
    eidx = lane - MOE_GROUPS
    emask = (eidx >= 0) & (eidx < MOE_EXPERTS) & ((eidx // MOE_PER_GROUP) == gidx)
    el = jnp.where(emask, logits, NEG_BIG)
    v1 = jnp.max(el, axis=-1, keepdims=True)
    i1 = jnp.min(jnp.where(el == v1, lane, far), axis=-1, keepdims=True)
    el2 = jnp.where(lane == i1, NEG_BIG, el)
    v2 = jnp.max(el2, axis=-1, keepdims=True)
    i2 = jnp.min(jnp.where(el2 == v2, lane, far), axis=-1, keepdims=True)
    pr = jnp.exp(v2 - v1)
    g0 = grp_w / (1.0 + pr)
    g1 = g0 * pr
    ri_ref[...] = jnp.where(lane == 0, i1 - MOE_GROUPS, jnp.where(lane == 1, i2 - MOE_GROUPS, 0))
    rf_ref[...] = jnp.where(lane == 0, g0, jnp.where(lane == 1, g1, 0.0))


def _out_proj_router(xt, o_a, o_b, o_c, w_out, gain, w_group, b_group, w_expert, b_expert, tm=512):
    n, d = xt.shape
    tm = min(tm, n)
    ng, ne = w_group.shape[1], w_expert.shape[1]
    wr = jnp.zeros((d, LANES), F32).at[:, 0:ng].set(w_group).at[:, ng:ng + ne].set(w_expert)
    wr_hi = wr.astype(BF16)
    wr_lo = (wr - wr_hi.astype(F32)).astype(BF16)
    br = jnp.zeros((1, LANES), F32).at[0, 0:ng].set(b_group).at[0, ng:ng + ne].set(b_expert)
    rows = lambda w: pl.BlockSpec((tm, w), lambda i: (i, 0))
    const = lambda r, w: pl.BlockSpec((r, w), lambda i: (0, 0))
    return pl.pallas_call(
        _outproj_body,
        grid=(n // tm,),
        in_specs=[rows(d), rows(o_a.shape[1]), rows(o_b.shape[1]), rows(o_c.shape[1]),
                  const(d, d), const(1, d), const(d, LANES), const(d, LANES), const(1, LANES)],
        out_specs=[rows(d), pl.BlockSpec((tm * d // LANES, LANES), lambda i: (i, 0)), rows(LANES), rows(LANES)],
        out_shape=[jax.ShapeDtypeStruct((n, d), F32), jax.ShapeDtypeStruct((n * d // LANES, LANES), F32),
                   jax.ShapeDtypeStruct((n, LANES), jnp.int32), jax.ShapeDtypeStruct((n, LANES), F32)],
        compiler_params=_cparams("parallel"),
        name="out_proj_router",
    )(xt, o_a, o_b, o_c, w_out.astype(BF16), gain.reshape(1, d), wr_hi, wr_lo, br)


def _rank_body(ri_ref, rank_ref, cnt_ref, carry):
    tr = ri_ref.shape[0]

    @pl.when(pl.program_id(0) == 0)
    def _():
        carry[...] = jnp.zeros_like(carry)

    ri = ri_ref[...]
    lane = _iota((tr, LANES), 1)
    oh0 = lane == ri[:, 0:1]
    oh1 = lane == ri[:, 1:2]
    oh = jnp.where(oh0 | oh1, 1.0, 0.0)
    earlier = jnp.where(_iota((tr, tr), 1) < _iota((tr, tr), 0), 1.0, 0.0).astype(BF16)
    cnt = jnp.dot(earlier, oh.astype(BF16), preferred_element_type=F32) + carry[...]
    r0 = jnp.sum(jnp.where(oh0, cnt, 0.0), axis=-1, keepdims=True)
    r1 = jnp.sum(jnp.where(oh1, cnt, 0.0), axis=-1, keepdims=True)
    rank_ref[...] = jnp.where(lane == 0, r0, jnp.where(lane == 1, r1, 0.0)).astype(jnp.int32)
    carry[...] = carry[...] + jnp.sum(oh, axis=0, keepdims=True)
    cnt_ref[...] = carry[...].astype(jnp.int32)


def _expert_ranks(ri, tr=512):
    n = ri.shape[0]
    tr = min(tr, n)
    return pl.pallas_call(
        _rank_body,
        grid=(n // tr,),
        in_specs=[pl.BlockSpec((tr, LANES), lambda i: (i, 0))],
        out_specs=[pl.BlockSpec((tr, LANES), lambda i: (i, 0)), pl.BlockSpec((1, LANES), lambda i: (0, 0))],
        out_shape=[jax.ShapeDtypeStruct((n, LANES), jnp.int32), jax.ShapeDtypeStruct((1, LANES), jnp.int32)],
        scratch_shapes=[pltpu.VMEM((1, LANES), F32)],
        compiler_params=_cparams("arbitrary"),
        name="expert_ranks",
    )(ri)


def _padded_starts(cnt_row):
    cnt = jnp.broadcast_to(cnt_row, (8, LANES))
    padded = (((cnt + (MOE_BLOCK - 1)) // MOE_BLOCK) * MOE_BLOCK).astype(F32)
    before = jnp.where(_iota((LANES, LANES), 0) < _iota((LANES, LANES), 1), 1.0, 0.0).astype(F32)
    start = jnp.dot(padded, before, precision=HIGHEST, preferred_element_type=F32)
    return start, start + padded


def _dest_body(ri_ref, rank_ref, cnt_ref, dest_ref):
    tr = ri_ref.shape[0]
    start, _ = _padded_starts(cnt_ref[...])
    start = start[0:1, :]
    ri = ri_ref[...]
    rank = rank_ref[...]
    lane = _iota((tr, LANES), 1)
    s0 = jnp.sum(jnp.where(lane == ri[:, 0:1], start, 0.0), axis=-1, keepdims=True).astype(jnp.int32)
    s1 = jnp.sum(jnp.where(lane == ri[:, 1:2], start, 0.0), axis=-1, keepdims=True).astype(jnp.int32)
    dest_ref[...] = jnp.where(lane == 0, s0 + rank[:, 0:1], jnp.where(lane == 1, s1 + rank[:, 1:2], 0))


def _destinations(ri, rank, cnt, tr=512):
    n = ri.shape[0]
    tr = min(tr, n)
    rows = pl.BlockSpec((tr, LANES), lambda i: (i, 0))
    return pl.pallas_call(
        _dest_body,
        grid=(n // tr,),
        in_specs=[rows, rows, pl.BlockSpec((1, LANES), lambda i: (0, 0))],
        out_specs=rows,
        out_shape=jax.ShapeDtypeStruct((n, LANES), jnp.int32),
        compiler_params=_cparams("parallel"),
        name="moe_destinations",
    )(ri, rank, cnt)


def _blkexp_body(cnt_ref, be_ref):
    nbp = be_ref.shape[0]
    _, end = _padded_starts(cnt_ref[...])
    end = end[0:1, :]
    lane = _iota((nbp, LANES), 1)
    row_start = (_iota((nbp, LANES), 0) * MOE_BLOCK).astype(F32)
    done = jnp.where((lane < MOE_EXPERTS) & (end <= row_start), 1.0, 0.0)
    be = jnp.minimum(jnp.sum(done, axis=-1, keepdims=True), MOE_EXPERTS - 1.0)
    be_ref[...] = jnp.broadcast_to(be, (nbp, LANES)).astype(jnp.int32)


def _block_experts(cnt, n_blocks):
    nbp = -(-n_blocks // 8) * 8
    out = pl.pallas_call(
        _blkexp_body,
        grid=(1,),
        in_specs=[pl.BlockSpec((1, LANES), lambda i: (0, 0))],
        out_specs=pl.BlockSpec((nbp, LANES), lambda i: (0, 0)),
        out_shape=jax.ShapeDtypeStruct((nbp, LANES), jnp.int32),
        name="moe_block_experts",
    )(cnt)
    return out[:n_blocks, 0]


def _for_each_row(n_rows, fn):
    def lane_row(i, carry):
        for j in range(LANES):
            fn(i * LANES + j, i, j)
        return carry

    lax.fori_loop(0, n_rows // LANES, lane_row, 0)


def _token_tile(ref, row, per):
    return ref.at[pl.ds(pl.multiple_of(row * per, per), per)]


def _scatter_body(per, d0_ref, d1_ref, cnt_ref, h_ref, xb_hbm, zbuf, sem, zsem):
    ts = h_ref.shape[0] // per
    n_rows = xb_hbm.shape[0] // per

    @pl.when(pl.program_id(0) == 0)
    def _():
        zbuf[...] = jnp.zeros_like(zbuf)

        def zero_rows(lo, hi):
            def start(r, carry):
                pltpu.make_async_copy(zbuf, _token_tile(xb_hbm, r, per), zsem).start()
                return carry

            def wait(r, carry):
                pltpu.make_async_copy(zbuf, _token_tile(xb_hbm, r, per), zsem).wait()
                return carry

            lax.fori_loop(lo, hi, start, 0)
            lax.fori_loop(lo, hi, wait, 0)

        def expert(e, slab_start):
            used = cnt_ref[0, e]
            padded = ((used + (MOE_BLOCK - 1)) // MOE_BLOCK) * MOE_BLOCK
            zero_rows(slab_start + used, slab_start + padded)
            return slab_start + padded

        zero_rows(lax.fori_loop(0, MOE_EXPERTS, expert, 0), n_rows)

    def copies(t, i, j):
        src = _token_tile(h_ref, t, per)
        return (pltpu.make_async_copy(src, _token_tile(xb_hbm, d0_ref[i, j], per), sem),
                pltpu.make_async_copy(src, _token_tile(xb_hbm, d1_ref[i, j], per), sem))

    def start(t, i, j):
        for prio, cp in enumerate(copies(t, i, j)):
            cp.start(priority=prio)

    def wait(t, i, j):
        for cp in copies(t, i, j):
            cp.wait()

    _for_each_row(ts, start)
    _for_each_row(ts, wait)


def _scatter_rows(h2, dest0, dest1, cnt, n, n_rows, ts):
    per = h2.shape[0] // n
    smem = pl.BlockSpec((ts // LANES, LANES), lambda i: (i, 0), memory_space=pltpu.SMEM)
    return pl.pallas_call(
        functools.partial(_scatter_body, per),
        grid=(n // ts,),
        in_specs=[smem, smem, pl.BlockSpec((1, LANES), lambda i: (0, 0), memory_space=pltpu.SMEM),
                  pl.BlockSpec((ts * per, LANES), lambda i: (i, 0))],
        out_specs=pl.BlockSpec(memory_space=pl.ANY),
        out_shape=jax.ShapeDtypeStruct((n_rows * per, LANES), F32),
        scratch_shapes=[pltpu.VMEM((per, LANES), F32), pltpu.SemaphoreType.DMA(()), pltpu.SemaphoreType.DMA(())],
        compiler_params=_cparams("arbitrary"),
        name="moe_scatter",
    )(dest0, dest1, cnt, h2)


def _expert_body(be_ref, x_ref, wg_ref, wu_ref, wd_ref, y_ref):
    del be_ref
    x = _load_token_tiles(x_ref, MOE_BLOCK)
    hid = _silu(_bdot(x, wg_ref[0])) * _bdot(x, wu_ref[0])
    _store_token_tiles(y_ref, _bdot(hid, wd_ref[0]))


def _expert_mlp(xb, blk_expert, w_gate, w_up, w_down):
    de, d = w_down.shape[1:]
    per = d // LANES
    nb = xb.shape[0] // (MOE_BLOCK * per)
    tiles = pl.BlockSpec((MOE_BLOCK * per, LANES), lambda i, be: (i, 0))
    grid_spec = pltpu.PrefetchScalarGridSpec(
        num_scalar_prefetch=1,
        grid=(nb,),
        in_specs=[tiles,
                  pl.BlockSpec((1, d, de), lambda i, be: (be[i], 0, 0)),
                  pl.BlockSpec((1, d, de), lambda i, be: (be[i], 0, 0)),
                  pl.BlockSpec((1, de, d), lambda i, be: (be[i], 0, 0))],
        out_specs=tiles,
    )
    return pl.pallas_call(
        _expert_body,
        grid_spec=grid_spec,
        out_shape=jax.ShapeDtypeStruct(xb.shape, F32),
        compiler_params=_cparams("arbitrary"),
        name="moe_experts",
    )(blk_expert, xb, w_gate, w_up, w_down)


def _combine_body(final_norm, d0_ref, d1_ref, rf_ref, x_ref, g_ref, yb_hbm, o_ref, gbuf, sem):
    ts, d = x_ref.shape
    per = d // LANES

    def copies(t, i, j):
        return (pltpu.make_async_copy(_token_tile(yb_hbm, d0_ref[i, j], per), _token_tile(gbuf.at[0], t, per), sem),
                pltpu.make_async_copy(_token_tile(yb_hbm, d1_ref[i, j], per), _token_tile(gbuf.at[1], t, per), sem))

    def start(t, i, j):
        for prio, cp in enumerate(copies(t, i, j)):
            cp.start(priority=prio)

    def wait(t, i, j):
        for cp in copies(t, i, j):
            cp.wait()

    _for_each_row(ts, start)
    _for_each_row(ts, wait)
    rf = rf_ref[...]
    x = x_ref[...] + (_load_token_tiles(gbuf.at[0], ts) * rf[:, 0:1] + _load_token_tiles(gbuf.at[1], ts) * rf[:, 1:2])
    if final_norm:
        x = x * lax.rsqrt(jnp.mean(x * x, axis=-1, keepdims=True) + NORM_EPS) * g_ref[...]
    o_ref[...] = x


def _combine(x_mid, yb, dest0, dest1, rf, final_gain, ts):
    n, d = x_mid.shape
    final_norm = final_gain is not None
    gain = (final_gain if final_norm else jnp.ones((d,), F32)).reshape(1, d)
    smem = pl.BlockSpec((ts // LANES, LANES), lambda i: (i, 0), memory_space=pltpu.SMEM)
    rows = lambda w: pl.BlockSpec((ts, w), lambda i: (i, 0))
    return pl.pallas_call(
        functools.partial(_combine_body, final_norm),
        grid=(n // ts,),
        in_specs=[smem, smem, rows(LANES), rows(d), pl.BlockSpec((1, d), lambda i: (0, 0)),
                  pl.BlockSpec(memory_space=pl.ANY)],
        out_specs=rows(d),
        out_shape=jax.ShapeDtypeStruct((n, d), F32),
        scratch_shapes=[pltpu.VMEM((2, ts * d // LANES, LANES), F32), pltpu.SemaphoreType.DMA(())],
        compiler_params=_cparams("arbitrary"),
        name="moe_combine",
    )(dest0, dest1, rf, x_mid, gain, yb)


def _moe(x_mid, h2, ri, rf, w_gate, w_up, w_down, final_gain):
    n, d = x_mid.shape
    n_assign = 2 * n
    n_blocks = (n_assign + MOE_EXPERTS * (MOE_BLOCK - 1) + MOE_BLOCK - 1) // MOE_BLOCK
    ts = min(1024, n)
    rank, cnt = _expert_ranks(ri)
    dest = _destinations(ri, rank, cnt)
    blk_expert = _block_experts(cnt, n_blocks)
    dest0 = dest[:, 0].reshape(n // LANES, LANES)
    dest1 = dest[:, 1].reshape(n // LANES, LANES)
    xb = _scatter_rows(h2, dest0, dest1, cnt, n, n_blocks * MOE_BLOCK, ts)
    yb = _expert_mlp(xb, blk_expert, w_gate, w_up, w_down)
    return _combine(x_mid, yb, dest0, dest1, rf, final_gain, ts)


def kernel(x, norm_mix, norm_ffn, norm_final, w_in, w_out, rwkv_mu, rwkv_w0, rwkv_w2, rwkv_a0, rwkv_a2, rwkv_g2, rwkv_k_k, rwkv_k_a, rwkv_r_k, rwkv_lnx_w, rwkv_lnx_b, gdn_conv_w, gdn_a_log, gdn_dt_bias, gdn_norm_w, lru_conv_w, lru_conv_b, lru_w_a, lru_b_a, lru_w_x, lru_b_x, lru_lambda, moe_w_group, moe_b_group, moe_w_expert, moe_b_expert, moe_w_gate, moe_w_up, moe_w_down):
    bsz, seq, d = x.shape
    depth = w_in.shape[0]
    n = bsz * seq
    a_cols = rwkv_mu.shape[1]
    db = gdn_norm_w.shape[1] * GDN_HEADS
    dc = lru_conv_b.shape[1]
    n_ba = 2 * GDN_HEADS
    widths = (a_cols, 3 * db, db, 2 * dc, LANES)
    xt = x.reshape(n, d)
    for l in range(depth):
        w = w_in[l]
        b0 = a_cols
        c0 = b0 + 4 * db + n_ba
        w_cat = jnp.concatenate(
            [w[:, 0:a_cols], w[:, b0:b0 + 4 * db], w[:, c0:c0 + 2 * dc],
             w[:, b0 + 4 * db:c0], jnp.zeros((d, LANES - n_ba), F32)], axis=1).astype(BF16)
        p_a, p_qkv, p_z, p_c, p_ba = _in_proj(xt, seq, norm_mix[l], w_cat, widths, rwkv_mu[l], gdn_conv_w[l],
                                              lru_conv_w[l], lru_conv_b[l])
        o_a = _rwkv_group(p_a, bsz, seq, rwkv_w0[l], rwkv_w2[l], rwkv_a0[l], rwkv_a2[l], rwkv_g2[l],
                          rwkv_k_k[l], rwkv_k_a[l], rwkv_r_k[l], rwkv_lnx_w[l], rwkv_lnx_b[l])
        o_b = _gdn_group(p_qkv, p_z, p_ba, bsz, seq, gdn_a_log[l], gdn_dt_bias[l], gdn_norm_w[l])
        o_c = _lru_group(p_c, bsz, seq, lru_w_a[l], lru_b_a[l], lru_w_x[l], lru_b_x[l], lru_lambda[l])
        x_mid, h2, ri, rf = _out_proj_router(xt, o_a, o_b, o_c, w_out[l], norm_ffn[l], moe_w_group[l],
                                             moe_b_group[l], moe_w_expert[l], moe_b_expert[l])
        xt = _moe(x_mid, h2, ri, rf, moe_w_gate[l], moe_w_up[l], moe_w_down[l],
                  norm_final if l == depth - 1 else None)
    return xt.reshape(bsz, seq, d)
```

```python
import functools

import jax
import jax.numpy as jnp
from jax import lax
from jax.experimental import pallas as pl
from jax.experimental.pallas import tpu as pltpu

F32 = jnp.float32
BF16 = jnp.bfloat16
HIGHEST = lax.Precision.HIGHEST

NORM_EPS = 1e-6
CONV_WIDTH = 4
CHUNK = 64
SCAN_TILE = 256
SCAN_SEQS = 2
RWKV_HEADS = 4
RWKV_GN_EPS = 64e-5
GDN_HEADS = 4
LRU_C = 8.0
MOE_GROUPS = 4
MOE_PER_GROUP = 8
MOE_EXPERTS = MOE_GROUPS * MOE_PER_GROUP
MOE_BLOCK = 256
LANES = 128
NEG_BIG = -1e30
VMEM_LIMIT = 56 * 1024 * 1024


def _cparams(*sem):
    return pltpu.CompilerParams(dimension_semantics=sem, vmem_limit_bytes=VMEM_LIMIT)


def _bdot(a, b):
    return jnp.dot(a.astype(BF16), b.astype(BF16), preferred_element_type=F32)


def _bdot_nt(a, b):
    return lax.dot_general(a.astype(BF16), b.astype(BF16), (((1,), (1,)), ((), ())),
                           preferred_element_type=F32)


def _bdot_tn(a, b):
    return lax.dot_general(a.astype(BF16), b.astype(BF16), (((0,), (0,)), ((), ())),
                           preferred_element_type=F32)


def _exact_dot(a, b01):
    hi = a.astype(BF16)
    lo = (a - hi.astype(F32)).astype(BF16)
    return (jnp.dot(hi, b01, preferred_element_type=F32)
            + jnp.dot(lo, b01, preferred_element_type=F32))


def _exact_dot_left(a01, b):
    hi = b.astype(BF16)
    lo = (b - hi.astype(F32)).astype(BF16)
    return (jnp.dot(a01, hi, preferred_element_type=F32)
            + jnp.dot(a01, lo, preferred_element_type=F32))


def _load_token_tiles(ref, rows):
    per = ref.shape[0] // rows
    return jnp.concatenate([ref[pl.ds(c, rows, stride=per), :] for c in range(per)], axis=1)


def _store_token_tiles(ref, value):
    rows, d = value.shape
    per = d // LANES
    for c in range(per):
        ref[pl.ds(c, rows, stride=per), :] = value[:, c * LANES:(c + 1) * LANES]


def _softplus(x):
    return jnp.maximum(x, 0.0) + jnp.log1p(jnp.exp(-jnp.abs(x)))


def _sigmoid(x):
    return 1.0 / (1.0 + jnp.exp(-x))


def _silu(x):
    return x * _sigmoid(x)


def _iota(shape, dim):
    return lax.broadcasted_iota(jnp.int32, shape, dim)


def _head_ones(width, head_dim):
    r = _iota((width, width), 0) // head_dim
    c = _iota((width, width), 1) // head_dim
    return jnp.where(r == c, 1.0, 0.0).astype(BF16)


def _stack_heads(x, n_heads):
    c, w = x.shape
    hd = w // n_heads
    xb = x.astype(BF16)
    lane_head = _iota((c, w), 1) // hd
    return jnp.concatenate([jnp.where(lane_head == h, xb, jnp.zeros_like(xb)) for h in range(n_heads)],
                           axis=0)


def _tri_masks(c, n_heads):
    r = _iota((c, n_heads * c), 0)
    s = _iota((c, n_heads * c), 1) % c
    return s < r, s <= r


def _unit_lower_inverses(l_mats, n_heads):
    shape = l_mats[0].shape
    c = shape[0]
    eye = jnp.where(_iota(shape, 1) % c == _iota(shape, 0), 1.0, 0.0).astype(F32)
    ps = [eye + l for l in l_mats]
    ms = [_bdot(l, _stack_heads(l, n_heads)) for l in l_mats]
    span = 2
    while span < c:
        last = span * 2 >= c
        nxt = [_bdot(p if last else jnp.concatenate([m, p], axis=0), _stack_heads(m, n_heads))
               for m, p in zip(ms, ps)]
        ps = [p + (r if last else r[c:]) for p, r in zip(ps, nxt)]
        ms = [None if last else r[0:c] for r in nxt]
        span *= 2
    return ps


def _head_block_mask(width, head_dim):
    r = _iota((width, width), 0) // head_dim
    c = _iota((width, width), 1) // head_dim
    return r == c


def _shift_rows(y, tail, j):
    rolled = pltpu.roll(y, j, 0)
    head = jnp.where(_iota(tail.shape, 0) < j, pltpu.roll(tail, j, 0), rolled[0:8])
    return jnp.concatenate([head, rolled[8:]], axis=0)


def _causal_conv(y, tail, cw):
    acc = cw[CONV_WIDTH - 1:CONV_WIDTH, :] * y
    for j in range(1, CONV_WIDTH):
        acc = acc + cw[CONV_WIDTH - 1 - j:CONV_WIDTH - j, :] * _shift_rows(y, tail, j)
    return acc


def _inproj_body(tiles_per_seq, x_ref, g_ref, w_ref, mu_ref, gcw_ref, lcw_ref, lcb_ref,
                 a_ref, qkv_ref, z_ref, c_ref, ba_ref, atail, qtail, ctail):
    tm = x_ref.shape[0]
    tails = (atail, qtail, ctail)

    @pl.when(pl.program_id(0) % tiles_per_seq == 0)
    def _():
        for tail in tails:
            tail[...] = jnp.zeros_like(tail)

    x = x_ref[...]
    h = x * lax.rsqrt(jnp.mean(x * x, axis=-1, keepdims=True) + NORM_EPS) * g_ref[...]
    hb = h.astype(BF16)
    wa, wq, wz, wc = a_ref.shape[-1], qkv_ref.shape[-1], z_ref.shape[-1], c_ref.shape[-1]
    dc = ctail.shape[1]

    def proj(off, width):
        return jnp.dot(hb, w_ref[:, off:off + width], preferred_element_type=F32)

    for s in range(3):
        cols = slice(s * (wq // 3), (s + 1) * (wq // 3))
        y = proj(wa + cols.start, wq // 3)
        qkv_ref[:, cols] = _silu(_causal_conv(y, qtail[:, cols], gcw_ref[:, cols]))
        qtail[:, cols] = y[tm - 8:tm, :]
    p = proj(0, wa)
    a_ref[...] = p + (_shift_rows(p, atail[...], 1) - p) * mu_ref[...]
    atail[...] = p[tm - 8:tm, :]
    z_ref[...] = proj(wa + wq, wz)
    pc = proj(wa + wq + wz, wc)
    c_ref[:, 0:dc] = lcb_ref[...] + _causal_conv(pc[:, 0:dc], ctail[...], lcw_ref[...])
    c_ref[:, dc:] = pc[:, dc:]
    ctail[...] = pc[tm - 8:tm, 0:dc]
    ba_ref[...] = proj(wa + wq + wz + wc, ba_ref.shape[-1])


def _in_proj(xt, seq, gain, w_cat, widths, mu, gdn_conv_w, lru_conv_w, lru_conv_b, tm=512):
    n, d = xt.shape
    ntot = w_cat.shape[1]
    tm = min(tm, seq)
    dc = lru_conv_b.shape[0]
    const = lambda r, w: pl.BlockSpec((r, w), lambda i: (0, 0))
    return pl.pallas_call(
        functools.partial(_inproj_body, seq // tm),
        grid=(n // tm,),
        in_specs=[pl.BlockSpec((tm, d), lambda i: (i, 0)), const(1, d), const(d, ntot),
                  const(1, widths[0]), const(CONV_WIDTH, widths[1]), const(CONV_WIDTH, dc), const(1, dc)],
        out_specs=[pl.BlockSpec((tm, w), lambda i: (i, 0)) for w in widths],
        out_shape=[jax.ShapeDtypeStruct((n, w), F32) for w in widths],
        scratch_shapes=[pltpu.VMEM((8, widths[0]), F32), pltpu.VMEM((8, widths[1]), F32),
                        pltpu.VMEM((8, dc), F32)],
        compiler_params=_cparams("arbitrary"),
        name="in_proj",
    )(xt, gain.reshape(1, d), w_cat, mu.reshape(1, -1), gdn_conv_w, lru_conv_w, lru_conv_b.reshape(1, dc))


def _rwkv_body(xs_ref, w0_ref, a0_ref, kk_ref, ka_ref, rk_ref, lnw_ref, lnb_ref,
               w2_ref, a2_ref, g2_ref, o_ref, state):
    c = CHUNK
    n_seq, tt, da = o_ref.shape
    nh = RWKV_HEADS
    hd = da // nh

    @pl.when(pl.program_id(1) == 0)
    def _():
        state[...] = jnp.zeros_like(state)

    xs = jnp.concatenate([xs_ref[s] for s in range(n_seq)], axis=0)
    r = xs[:, 0:da]
    k = xs[:, da:2 * da]
    v = xs[:, 2 * da:3 * da]
    lo = xs[:, 3 * da:]

    w_log = -_softplus(-(w0_ref[...] + _bdot(jnp.tanh(lo), w2_ref[...]))) - 0.5
    lw = -jnp.exp(w_log)
    a = _sigmoid(a0_ref[...] + _bdot(lo, a2_ref[...]))
    g = _bdot(_sigmoid(lo), g2_ref[...])

    ones_h = _head_ones(da, hd)
    kkr = k * kk_ref[...]
    kk = kkr * lax.rsqrt(_exact_dot(kkr * kkr, ones_h) + 1e-6)
    k2 = k * (1.0 + (a - 1.0) * ka_ref[...])
    alpha = -(kk * a)

    rr = _iota((tt, tt), 0)
    cc = _iota((tt, tt), 1)
    tri = jnp.where((cc <= rr) & (cc // c == rr // c), 1.0, 0.0).astype(BF16)
    cum = jnp.concatenate([_exact_dot_left(tri, lw[s * tt:(s + 1) * tt]) for s in range(n_seq)], axis=0)
    per_seq = tt // c
    n_chunks = n_seq * per_seq
    chunk_rows = [slice(g * c, (g + 1) * c) for g in range(n_chunks)]
    cum_ends = [cum[(g + 1) * c - 1:(g + 1) * c, :] for g in range(n_chunks)]
    cum_last = jnp.concatenate([jnp.broadcast_to(ce, (c, da)) for ce in cum_ends], axis=0)
    e_pos = jnp.exp(cum)
    e_neg = jnp.exp(-cum)
    e_tail = jnp.exp(cum_last - cum)
    r_t = r * e_pos
    b_t = kk * jnp.exp(cum - lw)
    a_t = alpha * e_neg
    k_t = k2 * e_neg
    a_end = alpha * e_tail
    k_end = k2 * e_tail

    nl = nh * c
    strict, incl = _tri_masks(c, nh)
    grams = [_bdot_nt(jnp.concatenate([b_t[rows], r_t[rows]], axis=0),
                      jnp.concatenate([_stack_heads(a_t[rows], nh), _stack_heads(k_t[rows], nh)], axis=0))
             for rows in chunk_rows]
    t_invs = _unit_lower_inverses([jnp.where(strict, gm[0:c, 0:nl], 0.0) for gm in grams], nh)
    vss = [_stack_heads(v[rows], nh) for rows in chunk_rows]
    lkvs = [_bdot(jnp.where(strict, gm[0:c, nl:], 0.0), vs) for gm, vs in zip(grams, vss)]
    wus = [_bdot(t_inv, jnp.concatenate([_stack_heads(b_t[rows], nh), _stack_heads(lkv, nh)], axis=1))
           for t_inv, lkv, rows in zip(t_invs, lkvs, chunk_rows)]
    s_as = [jnp.where(incl, gm[c:, 0:nl], 0.0) for gm in grams]
    s_ks = [jnp.where(incl, gm[c:, nl:], 0.0) for gm in grams]
    block_diag = _head_block_mask(da, hd)
    w_mats = [wu[:, 0:da] for wu in wus]
    u0s = [wu[:, da:] for wu in wus]
    q_effs = [r_t[rows] + _bdot(s_a, _stack_heads(w_mat, nh)) for rows, s_a, w_mat in zip(chunk_rows, s_as, w_mats)]
    o_consts = [_bdot(jnp.concatenate([s_a, s_k], axis=1), jnp.concatenate([_stack_heads(u0, nh), vs], axis=0))
                for s_a, s_k, u0, vs in zip(s_as, s_ks, u0s, vss)]
    s_lins = [jnp.where(block_diag, _bdot_tn(w_mat, a_end[rows]), 0.0) for w_mat, rows in zip(w_mats, chunk_rows)]
    s_consts = [jnp.where(block_diag, _bdot_tn(jnp.concatenate([u0, v[rows]], axis=0),
                                               jnp.concatenate([a_end[rows], k_end[rows]], axis=0)), 0.0)
                for u0, rows in zip(u0s, chunk_rows)]

    s_curs = [state[s] for s in range(n_seq)]
    outs = [None] * n_chunks
    for step in range(per_seq):
        for s in range(n_seq):
            i = s * per_seq + step
            outs[i] = _bdot_nt(q_effs[i], s_curs[s]) + o_consts[i]
            s_curs[s] = s_curs[s] * jnp.exp(cum_ends[i]) + _bdot(s_curs[s], s_lins[i]) + s_consts[i]
    for s in range(n_seq):
        state[s] = s_curs[s]
    o = jnp.concatenate(outs, axis=0)

    inv_hd = 1.0 / hd
    mean = _exact_dot(o, ones_h) * inv_hd
    cen = o - mean
    var = _exact_dot(cen * cen, ones_h) * inv_hd
    o_n = cen * lax.rsqrt(var + RWKV_GN_EPS) * lnw_ref[...] + lnb_ref[...]
    bonus = _exact_dot(r * k2 * rk_ref[...], ones_h) * v
    res = (o_n + bonus) * g
    for s in range(n_seq):
        o_ref[s] = res[s * tt:(s + 1) * tt]


def _rwkv_group(p_a, bsz, seq, w0, w2, a0, a2, g2, k_k, k_a, r_k, lnx_w, lnx_b, tt=SCAN_TILE):
    n, a_cols = p_a.shape
    da = w0.shape[-1]
    n_lora = a_cols - 3 * da
    tt = min(tt, seq)
    nchunk = seq // tt
    d_dec, d_aaa = w2.shape[0], a2.shape[0]
    w2p = jnp.zeros((n_lora, da), F32).at[0:d_dec].set(w2).astype(BF16)
    a2p = jnp.zeros((n_lora, da), F32).at[d_dec:d_dec + d_aaa].set(a2).astype(BF16)
    g2p = jnp.zeros((n_lora, da), F32).at[d_dec + d_aaa:].set(g2).astype(BF16)
    row = lambda t: t.reshape(1, -1).astype(F32)
    vec = lambda w: pl.BlockSpec((1, w), lambda b, c: (0, 0))
    mat = pl.BlockSpec((n_lora, da), lambda b, c: (0, 0))
    nb = min(SCAN_SEQS, bsz)
    seqs = lambda w: pl.BlockSpec((nb, tt, w), lambda b, c: (b, c, 0))
    out = pl.pallas_call(
        _rwkv_body,
        grid=(bsz // nb, nchunk),
        in_specs=[seqs(a_cols)] + [vec(da)] * 7 + [mat] * 3,
        out_specs=seqs(da),
        out_shape=jax.ShapeDtypeStruct((bsz, seq, da), F32),
        scratch_shapes=[pltpu.VMEM((nb, da, da), F32)],
        compiler_params=_cparams("parallel", "arbitrary"),
        name="rwkv7",
    )(p_a.reshape(bsz, seq, a_cols), row(w0), row(a0), row(k_k), row(k_a), row(r_k), row(lnx_w),
      row(lnx_b), w2p, a2p, g2p)
    return out.reshape(n, da)


def _gdn_body(qkv_ref, z_ref, ba_ref, alog_ref, dtb_ref, nw_ref, o_ref, state):
    c = CHUNK
    n_seq, tt, db = o_ref.shape
    rows_all = n_seq * tt
    nh = GDN_HEADS
    hd = db // nh
    nl = nh * c

    @pl.when(pl.program_id(1) == 0)
    def _():
        state[...] = jnp.zeros_like(state)

    def all_rows(ref, cols=slice(None)):
        return jnp.concatenate([ref[s, :, cols] for s in range(n_seq)], axis=0)

    q = all_rows(qkv_ref, slice(0, db))
    k = all_rows(qkv_ref, slice(db, 2 * db))
    v = all_rows(qkv_ref, slice(2 * db, 3 * db))

    def per_head(col_of_head):
        return jnp.concatenate([jnp.broadcast_to(col_of_head(h), (rows_all, hd)) for h in range(nh)], axis=1)

    def head_sumsq(x):
        return per_head(lambda h: jnp.sum(x[:, h * hd:(h + 1) * hd] * x[:, h * hd:(h + 1) * hd],
                                          axis=-1, keepdims=True))

    q = q * lax.rsqrt(head_sumsq(q) + 1e-6) * (hd ** -0.5)
    k = k * lax.rsqrt(head_sumsq(k) + 1e-6)

    ba = all_rows(ba_ref)
    beta_l = _sigmoid(ba)
    g_l = -jnp.exp(alog_ref[...]) * _softplus(ba + dtb_ref[...])
    row_in_chunk = _iota((rows_all, LANES), 0) % c
    gcum_l = g_l
    d = 1
    while d < c:
        gcum_l = gcum_l + jnp.where(row_in_chunk >= d, pltpu.roll(gcum_l, d, 0), 0.0)
        d *= 2
    per_seq = tt // c
    n_chunks = n_seq * per_seq
    chunk_rows = [slice(g * c, (g + 1) * c) for g in range(n_chunks)]
    g_last_l = jnp.concatenate([jnp.broadcast_to(gcum_l[(g + 1) * c - 1:(g + 1) * c, :], (c, LANES))
                                for g in range(n_chunks)], axis=0)
    e_g_l = jnp.exp(gcum_l)
    e_tail_l = jnp.exp(g_last_l - gcum_l)
    beta = per_head(lambda h: beta_l[:, h:h + 1])
    e_g = per_head(lambda h: e_g_l[:, nh + h:nh + h + 1])
    e_tail = per_head(lambda h: e_tail_l[:, nh + h:nh + h + 1])
    kb = k * beta
    vb = v * beta
    kbe = kb * e_g
    qe = q * e_g
    k_end = k * e_tail

    strict, incl = _tri_masks(c, nh)
    lane_head = _iota((c, nl), 1) // c
    on_diag = (_iota((c, nl), 1) % c) == _iota((c, nl), 0)
    dmats = []
    for rows in chunk_rows:
        gc = gcum_l[rows]
        gcol = jnp.broadcast_to(gc[:, nh:nh + 1], (c, nl))
        for h in range(1, nh):
            gcol = jnp.where(lane_head == h, jnp.broadcast_to(gc[:, nh + h:nh + h + 1], (c, nl)), gcol)
        grow = jnp.sum(jnp.where(on_diag, gcol, 0.0), axis=0, keepdims=True)
        dmats.append(jnp.exp(jnp.where(incl, gcol - grow, NEG_BIG)))
    grams = [_bdot_nt(jnp.concatenate([kb[rows], q[rows]], axis=0), _stack_heads(k[rows], nh))
             for rows in chunk_rows]
    t_invs = _unit_lower_inverses([jnp.where(strict, -(gm[0:c] * dm), 0.0) for gm, dm in zip(grams, dmats)], nh)
    intras = [jnp.where(incl, gm[c:] * dm, 0.0) for gm, dm in zip(grams, dmats)]
    sols = [_bdot(t_inv, jnp.concatenate([_stack_heads(vb[rows], nh), _stack_heads(kbe[rows], nh)], axis=1))
            for t_inv, rows in zip(t_invs, chunk_rows)]

    s_curs = [state[s] for s in range(n_seq)]
    block_diag = _head_block_mask(db, hd)
    outs = [None] * n_chunks
    for g in range(per_seq):
        for s in range(n_seq):
            i = s * per_seq + g
            rows = chunk_rows[i]
            u_s, w_s = sols[i][:, 0:db], sols[i][:, db:]
            wq = _bdot(jnp.concatenate([w_s, qe[rows]], axis=0), s_curs[s])
            v_new = u_s - wq[0:c]
            outs[i] = wq[c:] + _bdot(intras[i], _stack_heads(v_new, nh))
            upd = _bdot_tn(k_end[rows], v_new)
            s_curs[s] = s_curs[s] * e_g[(i + 1) * c - 1:(i + 1) * c, :] + jnp.where(block_diag, upd, 0.0)
    for s in range(n_seq):
        state[s] = s_curs[s]
    o = jnp.concatenate(outs, axis=0)
    o = o * lax.rsqrt(head_sumsq(o) * (1.0 / hd) + NORM_EPS) * nw_ref[...]
    res = o * _silu(all_rows(z_ref))
    for s in range(n_seq):
        o_ref[s] = res[s * tt:(s + 1) * tt]


def _gdn_group(qkv, z, ba, bsz, seq, a_log, dt_bias, norm_w, tt=SCAN_TILE):
    n, w3 = qkv.shape
    db = z.shape[-1]
    nh = GDN_HEADS
    tt = min(tt, seq)
    nchunk = seq // tt
    alog_row = jnp.zeros((1, LANES), F32).at[0, nh:2 * nh].set(a_log)
    dtb_row = jnp.zeros((1, LANES), F32).at[0, nh:2 * nh].set(dt_bias)
    nw_row = jnp.tile(norm_w, nh).reshape(1, db)
    nb = min(SCAN_SEQS, bsz)
    seqs = lambda w: pl.BlockSpec((nb, tt, w), lambda b, c: (b, c, 0))
    const = lambda r, w: pl.BlockSpec((r, w), lambda b, c: (0, 0))
    out = pl.pallas_call(
        _gdn_body,
        grid=(bsz // nb, nchunk),
        in_specs=[seqs(w3), seqs(db), seqs(LANES), const(1, LANES), const(1, LANES), const(1, db)],
        out_specs=seqs(db),
        out_shape=jax.ShapeDtypeStruct((bsz, seq, db), F32),
        scratch_shapes=[pltpu.VMEM((nb, db, db), F32)],
        compiler_params=_cparams("parallel", "arbitrary"),
        name="gdn",
    )(qkv.reshape(bsz, seq, w3), z.reshape(bsz, seq, db), ba.reshape(bsz, seq, LANES), alog_row, dtb_row, nw_row)
    return out.reshape(n, db)


def _lru_body(p_ref, wax_ref, ba_ref, bx_ref, lam_ref, o_ref, hcar):
    tt = p_ref.shape[0]
    dc = o_ref.shape[-1]

    @pl.when(pl.program_id(1) == 0)
    def _():
        hcar[...] = jnp.zeros_like(hcar)

    conv = p_ref[:, 0:dc]
    gate = p_ref[:, dc:]

    ri = _bdot(conv, wax_ref[...])
    r = _sigmoid(ri[:, 0:dc] + ba_ref[...])
    i = _sigmoid(ri[:, dc:] + bx_ref[...])
    log_a = -LRU_C * r * _softplus(-lam_ref[...])
    a = jnp.exp(log_a)
    mult = jnp.sqrt(jnp.tanh(-log_a) * (a * a + 1.0))
    row = _iota((tt, dc), 0)
    mult = jnp.where((row == 0) & (pl.program_id(1) == 0), 1.0, mult)
    bv = conv * i * mult

    av = a
    d = 1
    while d < tt:
        a_sh = jnp.where(row >= d, pltpu.roll(av, d, 0), 1.0)
        b_sh = jnp.where(row >= d, pltpu.roll(bv, d, 0), 0.0)
        bv = av * b_sh + bv
        av = av * a_sh
        d *= 2
    h = av * hcar[...] + bv
    hcar[...] = h[tt - 1:tt, :]
    gelu = 0.5 * gate * (1.0 + jnp.tanh(0.7978845608028654 * (gate + 0.044715 * gate * gate * gate)))
    o_ref[...] = h * gelu


def _block_diag(w):
    nb, bi, bo = w.shape
    out = jnp.zeros((nb * bi, nb * bo), w.dtype)
    for b in range(nb):
        out = out.at[b * bi:(b + 1) * bi, b * bo:(b + 1) * bo].set(w[b])
    return out


def _lru_group(p_c, bsz, seq, w_a, b_a, w_x, b_x, lam, tt=256):
    n, c_cols = p_c.shape
    dc = c_cols // 2
    tt = min(tt, seq)
    nt = seq // tt
    wax = jnp.concatenate([_block_diag(w_a), _block_diag(w_x)], axis=1).astype(BF16)
    row = lambda t: t.reshape(1, -1).astype(F32)
    const = lambda r, w: pl.BlockSpec((r, w), lambda b, c: (0, 0))
    return pl.pallas_call(
        _lru_body,
        grid=(bsz, nt),
        in_specs=[pl.BlockSpec((tt, c_cols), lambda b, c: (b * nt + c, 0)),
                  const(dc, 2 * dc), const(1, dc), const(1, dc), const(1, dc)],
        out_specs=pl.BlockSpec((tt, dc), lambda b, c: (b * nt + c, 0)),
        out_shape=jax.ShapeDtypeStruct((n, dc), F32),
        scratch_shapes=[pltpu.VMEM((1, dc), F32)],
        compiler_params=_cparams("parallel", "arbitrary"),
        name="rglru",
    )(p_c, wax, row(b_a), row(b_x), row(lam))


def _outproj_body(x_ref, ma_ref, mb_ref, mc_ref, wo_ref, g_ref, wrh_ref, wrl_ref, br_ref,
                  xo_ref, h_ref, ri_ref, rf_ref, rank_ref, cnt_ref, carry):
    da = ma_ref.shape[-1]
    db = mb_ref.shape[-1]
    tm = x_ref.shape[0]

    @pl.when(pl.program_id(0) == 0)
    def _():
        carry[...] = jnp.zeros_like(carry)

    mix = (jnp.dot(ma_ref[...].astype(BF16), wo_ref[0:da, :], preferred_element_type=F32)
           + jnp.dot(mb_ref[...].astype(BF16), wo_ref[da:da + db, :], preferred_element_type=F32)
           + jnp.dot(mc_ref[...].astype(BF16), wo_ref[da + db:, :], preferred_element_type=F32))
    x = x_ref[...] + mix
    xo_ref[...] = x
    h = x * lax.rsqrt(jnp.mean(x * x, axis=-1, keepdims=True) + NORM_EPS) * g_ref[...]
    _store_token_tiles(h_ref, h)

    h_hi = h.astype(BF16)
    h_lo = (h - h_hi.astype(F32)).astype(BF16)
    logits = (jnp.dot(h_hi, wrh_ref[...], preferred_element_type=F32)
              + jnp.dot(h_lo, wrh_ref[...], preferred_element_type=F32)
              + jnp.dot(h_hi, wrl_ref[...], preferred_element_type=F32)) + br_ref[...]
    lane = _iota((tm, LANES), 1)
    far = 4 * LANES
    gmask = lane < MOE_GROUPS
    gl = jnp.where(gmask, logits, NEG_BIG)
    gmax = jnp.max(gl, axis=-1, keepdims=True)
    gidx = jnp.min(jnp.where(gl == gmax, lane, far), axis=-1, keepdims=True)
    gsum = jnp.sum(jnp.where(gmask, jnp.exp(gl - gmax), 0.0), axis=-1, keepdims=True)
    grp_w = 1.0 / gsum
    eidx = lane - MOE_GROUPS
    emask = (eidx >= 0) & (eidx < MOE_EXPERTS) & ((eidx // MOE_PER_GROUP) == gidx)
    el = jnp.where(emask, logits, NEG_BIG)
    v1 = jnp.max(el, axis=-1, keepdims=True)
    i1 = jnp.min(jnp.where(el == v1, lane, far), axis=-1, keepdims=True)
    el2 = jnp.where(lane == i1, NEG_BIG, el)
    v2 = jnp.max(el2, axis=-1, keepdims=True)
    i2 = jnp.min(jnp.where(el2 == v2, lane, far), axis=-1, keepdims=True)
    pr = jnp.exp(v2 - v1)
    g0 = grp_w / (1.0 + pr)
    g1 = g0 * pr
    ri_ref[...] = jnp.where(lane == 0, i1 - MOE_GROUPS, jnp.where(lane == 1, i2 - MOE_GROUPS, 0))
    rf_ref[...] = jnp.where(lane == 0, g0, jnp.where(lane == 1, g1, 0.0))

    oh0 = lane == (i1 - MOE_GROUPS)
    oh1 = lane == (i2 - MOE_GROUPS)
    oh = jnp.where(oh0 | oh1, 1.0, 0.0)
    earlier = jnp.where(_iota((tm, tm), 1) < _iota((tm, tm), 0), 1.0, 0.0).astype(BF16)
    seen = jnp.dot(earlier, oh.astype(BF16), preferred_element_type=F32) + carry[...]
    r0 = jnp.sum(jnp.where(oh0, seen, 0.0), axis=-1, keepdims=True)
    r1 = jnp.sum(jnp.where(oh1, seen, 0.0), axis=-1, keepdims=True)
    rank_ref[...] = jnp.where(lane == 0, r0, jnp.where(lane == 1, r1, 0.0)).astype(jnp.int32)
    carry[...] = carry[...] + jnp.sum(oh, axis=0, keepdims=True)
    cnt_ref[...] = carry[...].astype(jnp.int32)


def _out_proj_router(xt, o_a, o_b, o_c, w_out, gain, w_group, b_group, w_expert, b_expert, tm=512):
    n, d = xt.shape
    tm = min(tm, n)
    ng, ne = w_group.shape[1], w_expert.shape[1]
    wr = jnp.zeros((d, LANES), F32).at[:, 0:ng].set(w_group).at[:, ng:ng + ne].set(w_expert)
    wr_hi = wr.astype(BF16)
    wr_lo = (wr - wr_hi.astype(F32)).astype(BF16)
    br = jnp.zeros((1, LANES), F32).at[0, 0:ng].set(b_group).at[0, ng:ng + ne].set(b_expert)
    rows = lambda w: pl.BlockSpec((tm, w), lambda i: (i, 0))
    const = lambda r, w: pl.BlockSpec((r, w), lambda i: (0, 0))
    return pl.pallas_call(
        _outproj_body,
        grid=(n // tm,),
        in_specs=[rows(d), rows(o_a.shape[1]), rows(o_b.shape[1]), rows(o_c.shape[1]),
                  const(d, d), const(1, d), const(d, LANES), const(d, LANES), const(1, LANES)],
        out_specs=[rows(d), pl.BlockSpec((tm * d // LANES, LANES), lambda i: (i, 0)), rows(LANES), rows(LANES),
                   rows(LANES), const(1, LANES)],
        out_shape=[jax.ShapeDtypeStruct((n, d), F32), jax.ShapeDtypeStruct((n * d // LANES, LANES), F32),
                   jax.ShapeDtypeStruct((n, LANES), jnp.int32), jax.ShapeDtypeStruct((n, LANES), F32),
                   jax.ShapeDtypeStruct((n, LANES), jnp.int32), jax.ShapeDtypeStruct((1, LANES), jnp.int32)],
        scratch_shapes=[pltpu.VMEM((1, LANES), F32)],
        compiler_params=_cparams("arbitrary"),
        name="out_proj_router",
    )(xt, o_a, o_b, o_c, w_out.astype(BF16), gain.reshape(1, d), wr_hi, wr_lo, br)


def _padded_starts(cnt_row):
    cnt = jnp.broadcast_to(cnt_row, (8, LANES))
    padded = (((cnt + (MOE_BLOCK - 1)) // MOE_BLOCK) * MOE_BLOCK).astype(F32)
    before = jnp.where(_iota((LANES, LANES), 0) < _iota((LANES, LANES), 1), 1.0, 0.0).astype(F32)
    start = jnp.dot(padded, before, precision=HIGHEST, preferred_element_type=F32)
    return start, start + padded


def _dest_body(ri_ref, rank_ref, cnt_ref, d0_ref, d1_ref):
    tr = ri_ref.shape[0]
    nr = tr // LANES
    start, _ = _padded_starts(cnt_ref[...])
    start = start[0:1, :]
    ri = ri_ref[...]
    rank = rank_ref[...].astype(F32)
    lane = _iota((tr, LANES), 1)
    own_lane = (_iota((tr, LANES), 0) % LANES) == lane
    fold = jnp.where(_iota((nr, tr), 1) // LANES == _iota((nr, tr), 0), 1.0, 0.0).astype(F32)
    for slot, d_ref in enumerate((d0_ref, d1_ref)):
        begin = jnp.sum(jnp.where(lane == ri[:, slot:slot + 1], start, 0.0), axis=-1, keepdims=True)
        dest = begin + rank[:, slot:slot + 1]
        dense = jnp.dot(fold, jnp.where(own_lane, dest, 0.0), precision=HIGHEST, preferred_element_type=F32)
        d_ref[...] = dense.astype(jnp.int32)


def _destinations(ri, rank, cnt, tr=2048):
    n = ri.shape[0]
    tr = min(tr, n)
    rows = pl.BlockSpec((tr, LANES), lambda i: (i, 0))
    dense = pl.BlockSpec((tr // LANES, LANES), lambda i: (i, 0))
    return pl.pallas_call(
        _dest_body,
        grid=(n // tr,),
        in_specs=[rows, rows, pl.BlockSpec((1, LANES), lambda i: (0, 0))],
        out_specs=[dense, dense],
        out_shape=[jax.ShapeDtypeStruct((n // LANES, LANES), jnp.int32)] * 2,
        compiler_params=_cparams("parallel"),
        name="moe_destinations",
    )(ri, rank, cnt)


def _blkexp_body(cnt_ref, be_ref):
    nbp = be_ref.shape[0]
    _, end = _padded_starts(cnt_ref[...])
    end = end[0:1, :]
    lane = _iota((nbp, LANES), 1)
    row_start = (_iota((nbp, LANES), 0) * MOE_BLOCK).astype(F32)
    done = jnp.where((lane < MOE_EXPERTS) & (end <= row_start), 1.0, 0.0)
    be = jnp.minimum(jnp.sum(done, axis=-1, keepdims=True), MOE_EXPERTS - 1.0)
    be_ref[...] = jnp.broadcast_to(be, (nbp, LANES)).astype(jnp.int32)


def _block_experts(cnt, n_blocks):
    nbp = -(-n_blocks // 8) * 8
    out = pl.pallas_call(
        _blkexp_body,
        grid=(1,),
        in_specs=[pl.BlockSpec((1, LANES), lambda i: (0, 0))],
        out_specs=pl.BlockSpec((nbp, LANES), lambda i: (0, 0)),
        out_shape=jax.ShapeDtypeStruct((nbp, LANES), jnp.int32),
        name="moe_block_experts",
    )(cnt)
    return out[:n_blocks, 0]


def _for_each_row(n_rows, fn):
    def lane_row(i, carry):
        for j in range(LANES):
            fn(i * LANES + j, i, j)
        return carry

    lax.fori_loop(0, n_rows // LANES, lane_row, 0)


def _token_tile(ref, row, per):
    return ref.at[pl.ds(pl.multiple_of(row * per, per), per)]


def _scatter_body(per, d0_ref, d1_ref, cnt_ref, h_ref, xb_hbm, zbuf, sem, zsem):
    ts = h_ref.shape[0] // per
    n_rows = xb_hbm.shape[0] // per

    @pl.when(pl.program_id(0) == 0)
    def _():
        zbuf[...] = jnp.zeros_like(zbuf)

        def zero_rows(lo, hi):
            def start(r, carry):
                pltpu.make_async_copy(zbuf, _token_tile(xb_hbm, r, per), zsem).start()
                return carry

            def wait(r, carry):
                pltpu.make_async_copy(zbuf, _token_tile(xb_hbm, r, per), zsem).wait()
                return carry

            lax.fori_loop(lo, hi, start, 0)
            lax.fori_loop(lo, hi, wait, 0)

        def expert(e, slab_start):
            used = cnt_ref[0, e]
            padded = ((used + (MOE_BLOCK - 1)) // MOE_BLOCK) * MOE_BLOCK
            zero_rows(slab_start + used, slab_start + padded)
            return slab_start + padded

        zero_rows(lax.fori_loop(0, MOE_EXPERTS, expert, 0), n_rows)

    def copies(t, i, j):
        src = _token_tile(h_ref, t, per)
        return (pltpu.make_async_copy(src, _token_tile(xb_hbm, d0_ref[i, j], per), sem),
                pltpu.make_async_copy(src, _token_tile(xb_hbm, d1_ref[i, j], per), sem))

    def start(t, i, j):
        for prio, cp in enumerate(copies(t, i, j)):
            cp.start(priority=prio)

    def wait(t, i, j):
        for cp in copies(t, i, j):
            cp.wait()

    _for_each_row(ts, start)
    _for_each_row(ts, wait)


def _scatter_rows(h2, dest0, dest1, cnt, n, n_rows, ts):
    per = h2.shape[0] // n
    smem = pl.BlockSpec((ts // LANES, LANES), lambda i: (i, 0), memory_space=pltpu.SMEM)
    return pl.pallas_call(
        functools.partial(_scatter_body, per),
        grid=(n // ts,),
        in_specs=[smem, smem, pl.BlockSpec((1, LANES), lambda i: (0, 0), memory_space=pltpu.SMEM),
                  pl.BlockSpec((ts * per, LANES), lambda i: (i, 0))],
        out_specs=pl.BlockSpec(memory_space=pl.ANY),
        out_shape=jax.ShapeDtypeStruct((n_rows * per, LANES), F32),
        scratch_shapes=[pltpu.VMEM((per, LANES), F32), pltpu.SemaphoreType.DMA(()), pltpu.SemaphoreType.DMA(())],
        compiler_params=_cparams("arbitrary"),
        name="moe_scatter",
    )(dest0, dest1, cnt, h2)


def _expert_body(be_ref, x_ref, wg_ref, wu_ref, wd_ref, y_ref):
    del be_ref
    x = _load_token_tiles(x_ref, MOE_BLOCK)
    hid = _silu(_bdot(x, wg_ref[0, 0])) * _bdot(x, wu_ref[0, 0])
    _store_token_tiles(y_ref, _bdot(hid, wd_ref[0, 0]))


def _expert_mlp(xb, blk_expert, w_gate, w_up, w_down, layer):
    de, d = w_down.shape[2:]
    per = d // LANES
    nb = xb.shape[0] // (MOE_BLOCK * per)
    tiles = pl.BlockSpec((MOE_BLOCK * per, LANES), lambda i, be: (i, 0))
    grid_spec = pltpu.PrefetchScalarGridSpec(
        num_scalar_prefetch=1,
        grid=(nb,),
        in_specs=[tiles,
                  pl.BlockSpec((1, 1, d, de), lambda i, be: (layer, be[i], 0, 0)),
                  pl.BlockSpec((1, 1, d, de), lambda i, be: (layer, be[i], 0, 0)),
                  pl.BlockSpec((1, 1, de, d), lambda i, be: (layer, be[i], 0, 0))],
        out_specs=tiles,
    )
    return pl.pallas_call(
        _expert_body,
        grid_spec=grid_spec,
        out_shape=jax.ShapeDtypeStruct(xb.shape, F32),
        compiler_params=_cparams("arbitrary"),
        name="moe_experts",
    )(blk_expert, xb, w_gate, w_up, w_down)


def _combine_body(final_norm, d0_ref, d1_ref, rf_ref, x_ref, g_ref, yb_hbm, o_ref, gbuf, sem):
    ts, d = x_ref.shape
    per = d // LANES

    def copies(t, i, j):
        return (pltpu.make_async_copy(_token_tile(yb_hbm, d0_ref[i, j], per), _token_tile(gbuf.at[0], t, per), sem),
                pltpu.make_async_copy(_token_tile(yb_hbm, d1_ref[i, j], per), _token_tile(gbuf.at[1], t, per), sem))

    def start(t, i, j):
        for prio, cp in enumerate(copies(t, i, j)):
            cp.start(priority=prio)

    def wait(t, i, j):
        for cp in copies(t, i, j):
            cp.wait()

    _for_each_row(ts, start)
    _for_each_row(ts, wait)
    rf = rf_ref[...]
    x = x_ref[...] + (_load_token_tiles(gbuf.at[0], ts) * rf[:, 0:1] + _load_token_tiles(gbuf.at[1], ts) * rf[:, 1:2])
    if final_norm:
        x = x * lax.rsqrt(jnp.mean(x * x, axis=-1, keepdims=True) + NORM_EPS) * g_ref[...]
    o_ref[...] = x


def _combine(x_mid, yb, dest0, dest1, rf, final_gain, ts):
    n, d = x_mid.shape
    final_norm = final_gain is not None
    gain = (final_gain if final_norm else jnp.ones((d,), F32)).reshape(1, d)
    smem = pl.BlockSpec((ts // LANES, LANES), lambda i: (i, 0), memory_space=pltpu.SMEM)
    rows = lambda w: pl.BlockSpec((ts, w), lambda i: (i, 0))
    return pl.pallas_call(
        functools.partial(_combine_body, final_norm),
        grid=(n // ts,),
        in_specs=[smem, smem, rows(LANES), rows(d), pl.BlockSpec((1, d), lambda i: (0, 0)),
                  pl.BlockSpec(memory_space=pl.ANY)],
        out_specs=rows(d),
        out_shape=jax.ShapeDtypeStruct((n, d), F32),
        scratch_shapes=[pltpu.VMEM((2, ts * d // LANES, LANES), F32), pltpu.SemaphoreType.DMA(())],
        compiler_params=_cparams("arbitrary"),
        name="moe_combine",
    )(dest0, dest1, rf, x_mid, gain, yb)


def _moe(x_mid, h2, ri, rf, rank, cnt, w_gate, w_up, w_down, layer, final_gain):
    n, d = x_mid.shape
    n_assign = 2 * n
    n_blocks = (n_assign + MOE_EXPERTS * (MOE_BLOCK - 1) + MOE_BLOCK - 1) // MOE_BLOCK
    ts = min(1024, n)
    dest0, dest1 = _destinations(ri, rank, cnt)
    blk_expert = _block_experts(cnt, n_blocks)
    xb = _scatter_rows(h2, dest0, dest1, cnt, n, n_blocks * MOE_BLOCK, ts)
    yb = _expert_mlp(xb, blk_expert, w_gate, w_up, w_down, layer)
    return _combine(x_mid, yb, dest0, dest1, rf, final_gain, ts)


def kernel(x, norm_mix, norm_ffn, norm_final, w_in, w_out, rwkv_mu, rwkv_w0, rwkv_w2, rwkv_a0, rwkv_a2, rwkv_g2, rwkv_k_k, rwkv_k_a, rwkv_r_k, rwkv_lnx_w, rwkv_lnx_b, gdn_conv_w, gdn_a_log, gdn_dt_bias, gdn_norm_w, lru_conv_w, lru_conv_b, lru_w_a, lru_b_a, lru_w_x, lru_b_x, lru_lambda, moe_w_group, moe_b_group, moe_w_expert, moe_b_expert, moe_w_gate, moe_w_up, moe_w_down):
    bsz, seq, d = x.shape
    depth = w_in.shape[0]
    n = bsz * seq
    a_cols = rwkv_mu.shape[1]
    db = gdn_norm_w.shape[1] * GDN_HEADS
    dc = lru_conv_b.shape[1]
    n_ba = 2 * GDN_HEADS
    widths = (a_cols, 3 * db, db, 2 * dc, LANES)
    xt = x.reshape(n, d)
    for l in range(depth):
        w = w_in[l]
        b0 = a_cols
        c0 = b0 + 4 * db + n_ba
        w_cat = jnp.concatenate(
            [w[:, 0:a_cols], w[:, b0:b0 + 4 * db], w[:, c0:c0 + 2 * dc],
             w[:, b0 + 4 * db:c0], jnp.zeros((d, LANES - n_ba), F32)], axis=1).astype(BF16)
        p_a, p_qkv, p_z, p_c, p_ba = _in_proj(xt, seq, norm_mix[l], w_cat, widths, rwkv_mu[l], gdn_conv_w[l],
                                              lru_conv_w[l], lru_conv_b[l])
        o_a = _rwkv_group(p_a, bsz, seq, rwkv_w0[l], rwkv_w2[l], rwkv_a0[l], rwkv_a2[l], rwkv_g2[l],
                          rwkv_k_k[l], rwkv_k_a[l], rwkv_r_k[l], rwkv_lnx_w[l], rwkv_lnx_b[l])
        o_b = _gdn_group(p_qkv, p_z, p_ba, bsz, seq, gdn_a_log[l], gdn_dt_bias[l], gdn_norm_w[l])
        o_c = _lru_group(p_c, bsz, seq, lru_w_a[l], lru_b_a[l], lru_w_x[l], lru_b_x[l], lru_lambda[l])
        x_mid, h2, ri, rf, rank, cnt = _out_proj_router(xt, o_a, o_b, o_c, w_out[l], norm_ffn[l], moe_w_group[l],
                                                        moe_b_group[l], moe_w_expert[l], moe_b_expert[l])
        xt = _moe(x_mid, h2, ri, rf, rank, cnt, moe_w_gate, moe_w_up, moe_w_down, l,
                  norm_final if l == depth - 1 else None)
    return xt.reshape(bsz, seq, d)
```

```python
import functools

import jax
import jax.numpy as jnp
from jax import lax
from jax.experimental import pallas as pl
from jax.experimental.pallas import tpu as pltpu

F32 = jnp.float32
BF16 = jnp.bfloat16
HIGHEST = lax.Precision.HIGHEST

NORM_EPS = 1e-6
CONV_WIDTH = 4
CHUNK = 64
SCAN_TILE = 256
SCAN_SEQS = 2
RWKV_HEADS = 4
RWKV_GN_EPS = 64e-5
GDN_HEADS = 4
LRU_C = 8.0
MOE_GROUPS = 4
MOE_PER_GROUP = 8
MOE_EXPERTS = MOE_GROUPS * MOE_PER_GROUP
MOE_BLOCK = 512
LANES = 128
SUBLANES = 8
NEG_BIG = -1e30
VMEM_LIMIT = 56 * 1024 * 1024


def _cparams(*sem):
    return pltpu.CompilerParams(dimension_semantics=sem, vmem_limit_bytes=VMEM_LIMIT)


def _bdot(a, b):
    return jnp.dot(a.astype(BF16), b.astype(BF16), preferred_element_type=F32)


def _bdot_nt(a, b):
    return lax.dot_general(a.astype(BF16), b.astype(BF16), (((1,), (1,)), ((), ())),
                           preferred_element_type=F32)


def _bdot_tn(a, b):
    return lax.dot_general(a.astype(BF16), b.astype(BF16), (((0,), (0,)), ((), ())),
                           preferred_element_type=F32)


def _exact_dot(a, b01):
    hi = a.astype(BF16)
    lo = (a - hi.astype(F32)).astype(BF16)
    return (jnp.dot(hi, b01, preferred_element_type=F32)
            + jnp.dot(lo, b01, preferred_element_type=F32))


def _exact_dot_left(a01, b):
    hi = b.astype(BF16)
    lo = (b - hi.astype(F32)).astype(BF16)
    return (jnp.dot(a01, hi, preferred_element_type=F32)
            + jnp.dot(a01, lo, preferred_element_type=F32))


def _load_token_tiles(ref, rows):
    per = ref.shape[0] // rows
    return jnp.concatenate([ref[pl.ds(c, rows, stride=per), :] for c in range(per)], axis=1)


def _store_token_tiles(ref, value):
    rows, d = value.shape
    per = d // LANES
    for c in range(per):
        ref[pl.ds(c, rows, stride=per), :] = value[:, c * LANES:(c + 1) * LANES]


def _softplus(x):
    return jnp.maximum(x, 0.0) + jnp.log1p(jnp.exp(-jnp.abs(x)))


def _sigmoid(x):
    return 1.0 / (1.0 + jnp.exp(-x))


def _silu(x):
    return x * _sigmoid(x)


def _iota(shape, dim):
    return lax.broadcasted_iota(jnp.int32, shape, dim)


def _head_ones(width, head_dim):
    r = _iota((width, width), 0) // head_dim
    c = _iota((width, width), 1) // head_dim
    return jnp.where(r == c, 1.0, 0.0).astype(BF16)


def _stack_heads(x, n_heads):
    c, w = x.shape
    hd = w // n_heads
    xb = x.astype(BF16)
    lane_head = _iota((c, w), 1) // hd
    return jnp.concatenate([jnp.where(lane_head == h, xb, jnp.zeros_like(xb)) for h in range(n_heads)],
                           axis=0)


def _tri_masks(c, n_heads):
    r = _iota((c, n_heads * c), 0)
    s = _iota((c, n_heads * c), 1) % c
    return s < r, s <= r


def _unit_lower_inverses(l_mats, n_heads):
    shape = l_mats[0].shape
    c = shape[0]
    eye = jnp.where(_iota(shape, 1) % c == _iota(shape, 0), 1.0, 0.0).astype(F32)
    ps = [eye + l for l in l_mats]
    ms = [_bdot(l, _stack_heads(l, n_heads)) for l in l_mats]
    span = 2
    while span < c:
        last = span * 2 >= c
        nxt = [_bdot(p if last else jnp.concatenate([m, p], axis=0), _stack_heads(m, n_heads))
               for m, p in zip(ms, ps)]
        ps = [p + (r if last else r[c:]) for p, r in zip(ps, nxt)]
        ms = [None if last else r[0:c] for r in nxt]
        span *= 2
    return ps


def _head_block_mask(width, head_dim):
    r = _iota((width, width), 0) // head_dim
    c = _iota((width, width), 1) // head_dim
    return r == c


def _shift_rows(y, tail, j):
    rolled = pltpu.roll(y, j, 0)
    head = jnp.where(_iota(tail.shape, 0) < j, pltpu.roll(tail, j, 0), rolled[0:8])
    return jnp.concatenate([head, rolled[8:]], axis=0)


def _causal_conv(y, tail, cw):
    acc = cw[CONV_WIDTH - 1:CONV_WIDTH, :] * y
    for j in range(1, CONV_WIDTH):
        acc = acc + cw[CONV_WIDTH - 1 - j:CONV_WIDTH - j, :] * _shift_rows(y, tail, j)
    return acc


def _inproj_body(tiles_per_seq, x_ref, g_ref, w_ref, mu_ref, gcw_ref, lcw_ref, lcb_ref,
                 a_ref, qkv_ref, z_ref, c_ref, ba_ref, atail, qtail, ctail):
    tm = x_ref.shape[0]
    tails = (atail, qtail, ctail)

    @pl.when(pl.program_id(0) % tiles_per_seq == 0)
    def _():
        for tail in tails:
            tail[...] = jnp.zeros_like(tail)

    x = x_ref[...]
    h = x * lax.rsqrt(jnp.mean(x * x, axis=-1, keepdims=True) + NORM_EPS) * g_ref[...]
    hb = h.astype(BF16)
    wa, wq, wz, wc = a_ref.shape[-1], qkv_ref.shape[-1], z_ref.shape[-1], c_ref.shape[-1]
    dc = ctail.shape[1]

    def proj(off, width):
        return jnp.dot(hb, w_ref[:, off:off + width], preferred_element_type=F32)

    for s in range(3):
        cols = slice(s * (wq // 3), (s + 1) * (wq // 3))
        y = proj(wa + cols.start, wq // 3)
        qkv_ref[:, cols] = _silu(_causal_conv(y, qtail[:, cols], gcw_ref[:, cols]))
        qtail[:, cols] = y[tm - 8:tm, :]
    p = proj(0, wa)
    a_ref[...] = p + (_shift_rows(p, atail[...], 1) - p) * mu_ref[...]
    atail[...] = p[tm - 8:tm, :]
    z_ref[...] = proj(wa + wq, wz)
    pc = proj(wa + wq + wz, wc)
    c_ref[:, 0:dc] = lcb_ref[...] + _causal_conv(pc[:, 0:dc], ctail[...], lcw_ref[...])
    c_ref[:, dc:] = pc[:, dc:]
    ctail[...] = pc[tm - 8:tm, 0:dc]
    ba_ref[...] = proj(wa + wq + wz + wc, ba_ref.shape[-1])


def _in_proj(xt, seq, gain, w_cat, widths, mu, gdn_conv_w, lru_conv_w, lru_conv_b, tm=512):
    n, d = xt.shape
    ntot = w_cat.shape[1]
    tm = min(tm, seq)
    dc = lru_conv_b.shape[0]
    const = lambda r, w: pl.BlockSpec((r, w), lambda i: (0, 0))
    return pl.pallas_call(
        functools.partial(_inproj_body, seq // tm),
        grid=(n // tm,),
        in_specs=[pl.BlockSpec((tm, d), lambda i: (i, 0)), const(1, d), const(d, ntot),
                  const(1, widths[0]), const(CONV_WIDTH, widths[1]), const(CONV_WIDTH, dc), const(1, dc)],
        out_specs=[pl.BlockSpec((tm, w), lambda i: (i, 0)) for w in widths],
        out_shape=[jax.ShapeDtypeStruct((n, w), F32) for w in widths],
        scratch_shapes=[pltpu.VMEM((8, widths[0]), F32), pltpu.VMEM((8, widths[1]), F32),
                        pltpu.VMEM((8, dc), F32)],
        compiler_params=_cparams("arbitrary"),
        name="in_proj",
    )(xt, gain.reshape(1, d), w_cat, mu.reshape(1, -1), gdn_conv_w, lru_conv_w, lru_conv_b.reshape(1, dc))


def _rwkv_body(xs_ref, w0_ref, a0_ref, kk_ref, ka_ref, rk_ref, lnw_ref, lnb_ref,
               w2_ref, a2_ref, g2_ref, o_ref, state):
    c = CHUNK
    n_seq, tt, da = o_ref.shape
    nh = RWKV_HEADS
    hd = da // nh

    @pl.when(pl.program_id(1) == 0)
    def _():
        state[...] = jnp.zeros_like(state)

    xs = jnp.concatenate([xs_ref[s] for s in range(n_seq)], axis=0)
    r = xs[:, 0:da]
    k = xs[:, da:2 * da]
    v = xs[:, 2 * da:3 * da]
    lo = xs[:, 3 * da:]

    w_log = -_softplus(-(w0_ref[...] + _bdot(jnp.tanh(lo), w2_ref[...]))) - 0.5
    lw = -jnp.exp(w_log)
    a = _sigmoid(a0_ref[...] + _bdot(lo, a2_ref[...]))
    g = _bdot(_sigmoid(lo), g2_ref[...])

    ones_h = _head_ones(da, hd)
    kkr = k * kk_ref[...]
    kk = kkr * lax.rsqrt(_exact_dot(kkr * kkr, ones_h) + 1e-6)
    k2 = k * (1.0 + (a - 1.0) * ka_ref[...])
    alpha = -(kk * a)

    rr = _iota((tt, tt), 0)
    cc = _iota((tt, tt), 1)
    tri = jnp.where((cc <= rr) & (cc // c == rr // c), 1.0, 0.0).astype(BF16)
    cum = jnp.concatenate([_exact_dot_left(tri, lw[s * tt:(s + 1) * tt]) for s in range(n_seq)], axis=0)
    per_seq = tt // c
    n_chunks = n_seq * per_seq
    chunk_rows = [slice(g * c, (g + 1) * c) for g in range(n_chunks)]
    cum_ends = [cum[(g + 1) * c - 1:(g + 1) * c, :] for g in range(n_chunks)]
    cum_last = jnp.concatenate([jnp.broadcast_to(ce, (c, da)) for ce in cum_ends], axis=0)
    e_pos = jnp.exp(cum)
    e_neg = jnp.exp(-cum)
    e_tail = jnp.exp(cum_last - cum)
    r_t = r * e_pos
    b_t = kk * jnp.exp(cum - lw)
    a_t = alpha * e_neg
    k_t = k2 * e_neg
    a_end = alpha * e_tail
    k_end = k2 * e_tail

    nl = nh * c
    strict, incl = _tri_masks(c, nh)
    grams = [_bdot_nt(jnp.concatenate([b_t[rows], r_t[rows]], axis=0),
                      jnp.concatenate([_stack_heads(a_t[rows], nh), _stack_heads(k_t[rows], nh)], axis=0))
             for rows in chunk_rows]
    t_invs = _unit_lower_inverses([jnp.where(strict, gm[0:c, 0:nl], 0.0) for gm in grams], nh)
    vss = [_stack_heads(v[rows], nh) for rows in chunk_rows]
    lkvs = [_bdot(jnp.where(strict, gm[0:c, nl:], 0.0), vs) for gm, vs in zip(grams, vss)]
    wus = [_bdot(t_inv, jnp.concatenate([_stack_heads(b_t[rows], nh), _stack_heads(lkv, nh)], axis=1))
           for t_inv, lkv, rows in zip(t_invs, lkvs, chunk_rows)]
    s_as = [jnp.where(incl, gm[c:, 0:nl], 0.0) for gm in grams]
    s_ks = [jnp.where(incl, gm[c:, nl:], 0.0) for gm in grams]
    block_diag = _head_block_mask(da, hd)
    w_mats = [wu[:, 0:da] for wu in wus]
    u0s = [wu[:, da:] for wu in wus]
    q_effs = [r_t[rows] + _bdot(s_a, _stack_heads(w_mat, nh)) for rows, s_a, w_mat in zip(chunk_rows, s_as, w_mats)]
    o_consts = [_bdot(jnp.concatenate([s_a, s_k], axis=1), jnp.concatenate([_stack_heads(u0, nh), vs], axis=0))
                for s_a, s_k, u0, vs in zip(s_as, s_ks, u0s, vss)]
    s_lins = [jnp.where(block_diag, _bdot_tn(w_mat, a_end[rows]), 0.0) for w_mat, rows in zip(w_mats, chunk_rows)]
    s_consts = [jnp.where(block_diag, _bdot_tn(jnp.concatenate([u0, v[rows]], axis=0),
                                               jnp.concatenate([a_end[rows], k_end[rows]], axis=0)), 0.0)
                for u0, rows in zip(u0s, chunk_rows)]

    s_curs = [state[s] for s in range(n_seq)]
    outs = [None] * n_chunks
    for step in range(per_seq):
        for s in range(n_seq):
            i = s * per_seq + step
            outs[i] = _bdot_nt(q_effs[i], s_curs[s]) + o_consts[i]
            s_curs[s] = s_curs[s] * jnp.exp(cum_ends[i]) + _bdot(s_curs[s], s_lins[i]) + s_consts[i]
    for s in range(n_seq):
        state[s] = s_curs[s]
    o = jnp.concatenate(outs, axis=0)

    inv_hd = 1.0 / hd
    mean = _exact_dot(o, ones_h) * inv_hd
    cen = o - mean
    var = _exact_dot(cen * cen, ones_h) * inv_hd
    o_n = cen * lax.rsqrt(var + RWKV_GN_EPS) * lnw_ref[...] + lnb_ref[...]
    bonus = _exact_dot(r * k2 * rk_ref[...], ones_h) * v
    res = (o_n + bonus) * g
    for s in range(n_seq):
        o_ref[s] = res[s * tt:(s + 1) * tt]


def _rwkv_group(p_a, bsz, seq, w0, w2, a0, a2, g2, k_k, k_a, r_k, lnx_w, lnx_b, tt=SCAN_TILE):
    n, a_cols = p_a.shape
    da = w0.shape[-1]
    n_lora = a_cols - 3 * da
    tt = min(tt, seq)
    nchunk = seq // tt
    d_dec, d_aaa = w2.shape[0], a2.shape[0]
    w2p = jnp.zeros((n_lora, da), F32).at[0:d_dec].set(w2).astype(BF16)
    a2p = jnp.zeros((n_lora, da), F32).at[d_dec:d_dec + d_aaa].set(a2).astype(BF16)
    g2p = jnp.zeros((n_lora, da), F32).at[d_dec + d_aaa:].set(g2).astype(BF16)
    row = lambda t: t.reshape(1, -1).astype(F32)
    vec = lambda w: pl.BlockSpec((1, w), lambda b, c: (0, 0))
    mat = pl.BlockSpec((n_lora, da), lambda b, c: (0, 0))
    nb = min(SCAN_SEQS, bsz)
    seqs = lambda w: pl.BlockSpec((nb, tt, w), lambda b, c: (b, c, 0))
    out = pl.pallas_call(
        _rwkv_body,
        grid=(bsz // nb, nchunk),
        in_specs=[seqs(a_cols)] + [vec(da)] * 7 + [mat] * 3,
        out_specs=seqs(da),
        out_shape=jax.ShapeDtypeStruct((bsz, seq, da), F32),
        scratch_shapes=[pltpu.VMEM((nb, da, da), F32)],
        compiler_params=_cparams("parallel", "arbitrary"),
        name="rwkv7",
    )(p_a.reshape(bsz, seq, a_cols), row(w0), row(a0), row(k_k), row(k_a), row(r_k), row(lnx_w),
      row(lnx_b), w2p, a2p, g2p)
    return out.reshape(n, da)


def _gdn_body(qkv_ref, z_ref, ba_ref, alog_ref, dtb_ref, nw_ref, o_ref, state):
    c = CHUNK
    n_seq, tt, db = o_ref.shape
    rows_all = n_seq * tt
    nh = GDN_HEADS
    hd = db // nh
    nl = nh * c

    @pl.when(pl.program_id(1) == 0)
    def _():
        state[...] = jnp.zeros_like(state)

    def all_rows(ref, cols=slice(None)):
        return jnp.concatenate([ref[s, :, cols] for s in range(n_seq)], axis=0)

    q = all_rows(qkv_ref, slice(0, db))
    k = all_rows(qkv_ref, slice(db, 2 * db))
    v = all_rows(qkv_ref, slice(2 * db, 3 * db))

    def per_head(col_of_head):
        return jnp.concatenate([jnp.broadcast_to(col_of_head(h), (rows_all, hd)) for h in range(nh)], axis=1)

    def head_sumsq(x):
        return per_head(lambda h: jnp.sum(x[:, h * hd:(h + 1) * hd] * x[:, h * hd:(h + 1) * hd],
                                          axis=-1, keepdims=True))

    q = q * lax.rsqrt(head_sumsq(q) + 1e-6) * (hd ** -0.5)
    k = k * lax.rsqrt(head_sumsq(k) + 1e-6)

    ba = all_rows(ba_ref)
    beta_l = _sigmoid(ba)
    g_l = -jnp.exp(alog_ref[...]) * _softplus(ba + dtb_ref[...])
    row_in_chunk = _iota((rows_all, LANES), 0) % c
    gcum_l = g_l
    d = 1
    while d < c:
        gcum_l = gcum_l + jnp.where(row_in_chunk >= d, pltpu.roll(gcum_l, d, 0), 0.0)
        d *= 2
    per_seq = tt // c
    n_chunks = n_seq * per_seq
    chunk_rows = [slice(g * c, (g + 1) * c) for g in range(n_chunks)]
    g_last_l = jnp.concatenate([jnp.broadcast_to(gcum_l[(g + 1) * c - 1:(g + 1) * c, :], (c, LANES))
                                for g in range(n_chunks)], axis=0)
    e_g_l = jnp.exp(gcum_l)
    e_tail_l = jnp.exp(g_last_l - gcum_l)
    beta = per_head(lambda h: beta_l[:, h:h + 1])
    e_g = per_head(lambda h: e_g_l[:, nh + h:nh + h + 1])
    e_tail = per_head(lambda h: e_tail_l[:, nh + h:nh + h + 1])
    kb = k * beta
    vb = v * beta
    kbe = kb * e_g
    qe = q * e_g
    k_end = k * e_tail

    strict, incl = _tri_masks(c, nh)
    lane_head = _iota((c, nl), 1) // c
    on_diag = (_iota((c, nl), 1) % c) == _iota((c, nl), 0)
    dmats = []
    for rows in chunk_rows:
        gc = gcum_l[rows]
        gcol = jnp.broadcast_to(gc[:, nh:nh + 1], (c, nl))
        for h in range(1, nh):
            gcol = jnp.where(lane_head == h, jnp.broadcast_to(gc[:, nh + h:nh + h + 1], (c, nl)), gcol)
        grow = jnp.sum(jnp.where(on_diag, gcol, 0.0), axis=0, keepdims=True)
        dmats.append(jnp.exp(jnp.where(incl, gcol - grow, NEG_BIG)))
    grams = [_bdot_nt(jnp.concatenate([kb[rows], q[rows]], axis=0), _stack_heads(k[rows], nh))
             for rows in chunk_rows]
    intras = [jnp.where(incl, gm[c:] * dm, 0.0) for gm, dm in zip(grams, dmats)]
    t_invs = _unit_lower_inverses([jnp.where(strict, -(gm[0:c] * dm), 0.0) for gm, dm in zip(grams, dmats)], nh)
    sols = [_bdot(t_inv, jnp.concatenate([_stack_heads(vb[rows], nh), _stack_heads(kbe[rows], nh)], axis=1))
            for t_inv, rows in zip(t_invs, chunk_rows)]

    s_curs = [state[s] for s in range(n_seq)]
    block_diag = _head_block_mask(db, hd)
    outs = [None] * n_chunks
    for g in range(per_seq):
        for s in range(n_seq):
            i = s * per_seq + g
            rows = chunk_rows[i]
            u_s, w_s = sols[i][:, 0:db], sols[i][:, db:]
            wq = _bdot(jnp.concatenate([w_s, qe[rows]], axis=0), s_curs[s])
            v_new = u_s - wq[0:c]
            outs[i] = wq[c:] + _bdot(intras[i], _stack_heads(v_new, nh))
            upd = _bdot_tn(k_end[rows], v_new)
            s_curs[s] = s_curs[s] * e_g[(i + 1) * c - 1:(i + 1) * c, :] + jnp.where(block_diag, upd, 0.0)
    for s in range(n_seq):
        state[s] = s_curs[s]
    o = jnp.concatenate(outs, axis=0)
    o = o * lax.rsqrt(head_sumsq(o) * (1.0 / hd) + NORM_EPS) * nw_ref[...]
    res = o * _silu(all_rows(z_ref))
    for s in range(n_seq):
        o_ref[s] = res[s * tt:(s + 1) * tt]


def _gdn_group(qkv, z, ba, bsz, seq, a_log, dt_bias, norm_w, tt=SCAN_TILE):
    n, w3 = qkv.shape
    db = z.shape[-1]
    nh = GDN_HEADS
    tt = min(tt, seq)
    nchunk = seq // tt
    alog_row = jnp.zeros((1, LANES), F32).at[0, nh:2 * nh].set(a_log)
    dtb_row = jnp.zeros((1, LANES), F32).at[0, nh:2 * nh].set(dt_bias)
    nw_row = jnp.tile(norm_w, nh).reshape(1, db)
    nb = min(SCAN_SEQS, bsz)
    seqs = lambda w: pl.BlockSpec((nb, tt, w), lambda b, c: (b, c, 0))
    const = lambda r, w: pl.BlockSpec((r, w), lambda b, c: (0, 0))
    out = pl.pallas_call(
        _gdn_body,
        grid=(bsz // nb, nchunk),
        in_specs=[seqs(w3), seqs(db), seqs(LANES), const(1, LANES), const(1, LANES), const(1, db)],
        out_specs=seqs(db),
        out_shape=jax.ShapeDtypeStruct((bsz, seq, db), F32),
        scratch_shapes=[pltpu.VMEM((nb, db, db), F32)],
        compiler_params=_cparams("parallel", "arbitrary"),
        name="gdn",
    )(qkv.reshape(bsz, seq, w3), z.reshape(bsz, seq, db), ba.reshape(bsz, seq, LANES), alog_row, dtb_row, nw_row)
    return out.reshape(n, db)


def _lru_body(p_ref, wax_ref, ba_ref, bx_ref, lam_ref, o_ref, hcar):
    tt = p_ref.shape[0]
    dc = o_ref.shape[-1]

    @pl.when(pl.program_id(1) == 0)
    def _():
        hcar[...] = jnp.zeros_like(hcar)

    conv = p_ref[:, 0:dc]
    gate = p_ref[:, dc:]

    ri = _bdot(conv, wax_ref[...])
    r = _sigmoid(ri[:, 0:dc] + ba_ref[...])
    i = _sigmoid(ri[:, dc:] + bx_ref[...])
    log_a = -LRU_C * r * _softplus(-lam_ref[...])
    a = jnp.exp(log_a)
    mult = jnp.sqrt(jnp.tanh(-log_a) * (a * a + 1.0))
    row = _iota((tt, dc), 0)
    mult = jnp.where((row == 0) & (pl.program_id(1) == 0), 1.0, mult)
    bv = conv * i * mult

    av = a
    row_in_group = row % SUBLANES
    d = 1
    while d < SUBLANES:
        a_sh = jnp.where(row_in_group >= d, pltpu.roll(av, d, 0), 1.0)
        b_sh = jnp.where(row_in_group >= d, pltpu.roll(bv, d, 0), 0.0)
        bv = av * b_sh + bv
        av = av * a_sh
        d *= 2
    carry = hcar[...]
    groups = []
    for i in range(tt // SUBLANES):
        rows = slice(i * SUBLANES, (i + 1) * SUBLANES)
        hg = av[rows] * carry + bv[rows]
        groups.append(hg)
        carry = hg[SUBLANES - 1:SUBLANES, :]
    h = jnp.concatenate(groups, axis=0)
    hcar[...] = carry
    gelu = 0.5 * gate * (1.0 + jnp.tanh(0.7978845608028654 * (gate + 0.044715 * gate * gate * gate)))
    o_ref[...] = h * gelu


def _block_diag(w):
    nb, bi, bo = w.shape
    out = jnp.zeros((nb * bi, nb * bo), w.dtype)
    for b in range(nb):
        out = out.at[b * bi:(b + 1) * bi, b * bo:(b + 1) * bo].set(w[b])
    return out


def _lru_group(p_c, bsz, seq, w_a, b_a, w_x, b_x, lam, tt=256):
    n, c_cols = p_c.shape
    dc = c_cols // 2
    tt = min(tt, seq)
    nt = seq // tt
    wax = jnp.concatenate([_block_diag(w_a), _block_diag(w_x)], axis=1).astype(BF16)
    row = lambda t: t.reshape(1, -1).astype(F32)
    const = lambda r, w: pl.BlockSpec((r, w), lambda b, c: (0, 0))
    return pl.pallas_call(
        _lru_body,
        grid=(bsz, nt),
        in_specs=[pl.BlockSpec((tt, c_cols), lambda b, c: (b * nt + c, 0)),
                  const(dc, 2 * dc), const(1, dc), const(1, dc), const(1, dc)],
        out_specs=pl.BlockSpec((tt, dc), lambda b, c: (b * nt + c, 0)),
        out_shape=jax.ShapeDtypeStruct((n, dc), F32),
        scratch_shapes=[pltpu.VMEM((1, dc), F32)],
        compiler_params=_cparams("parallel", "arbitrary"),
        name="rglru",
    )(p_c, wax, row(b_a), row(b_x), row(lam))


def _outproj_body(x_ref, ma_ref, mb_ref, mc_ref, wo_ref, g_ref, wrh_ref, wrl_ref, br_ref,
                  xo_ref, h_ref, ri_ref, rf_ref, rank_ref, cnt_ref, carry):
    da = ma_ref.shape[-1]
    db = mb_ref.shape[-1]
    tm = x_ref.shape[0]

    @pl.when(pl.program_id(0) == 0)
    def _():
        carry[...] = jnp.zeros_like(carry)

    mix = (jnp.dot(ma_ref[...].astype(BF16), wo_ref[0:da, :], preferred_element_type=F32)
           + jnp.dot(mb_ref[...].astype(BF16), wo_ref[da:da + db, :], preferred_element_type=F32)
           + jnp.dot(mc_ref[...].astype(BF16), wo_ref[da + db:, :], preferred_element_type=F32))
    x = x_ref[...] + mix
    xo_ref[...] = x
    h = x * lax.rsqrt(jnp.mean(x * x, axis=-1, keepdims=True) + NORM_EPS) * g_ref[...]
    _store_token_tiles(h_ref, h)

    h_hi = h.astype(BF16)
    h_lo = (h - h_hi.astype(F32)).astype(BF16)
    logits = (jnp.dot(h_hi, wrh_ref[...], preferred_element_type=F32)
              + jnp.dot(h_lo, wrh_ref[...], preferred_element_type=F32)
              + jnp.dot(h_hi, wrl_ref[...], preferred_element_type=F32)) + br_ref[...]
    lane = _iota((tm, LANES), 1)
    far = 4 * LANES
    gmask = lane < MOE_GROUPS
    gl = jnp.where(gmask, logits, NEG_BIG)
    gmax = jnp.max(gl, axis=-1, keepdims=True)
    gidx = jnp.min(jnp.where(gl == gmax, lane, far), axis=-1, keepdims=True)
    gsum = jnp.sum(jnp.where(gmask, jnp.exp(gl - gmax), 0.0), axis=-1, keepdims=True)
    grp_w = 1.0 / gsum
    eidx = lane - MOE_GROUPS
    emask = (eidx >= 0) & (eidx < MOE_EXPERTS) & ((eidx // MOE_PER_GROUP) == gidx)
    el = jnp.where(emask, logits, NEG_BIG)
    v1 = jnp.max(el, axis=-1, keepdims=True)
    i1 = jnp.min(jnp.where(el == v1, lane, far), axis=-1, keepdims=True)
    el2 = jnp.where(lane == i1, NEG_BIG, el)
    v2 = jnp.max(el2, axis=-1, keepdims=True)
    i2 = jnp.min(jnp.where(el2 == v2, lane, far), axis=-1, keepdims=True)
    pr = jnp.exp(v2 - v1)
    g0 = grp_w / (1.0 + pr)
    g1 = g0 * pr
    ri_ref[...] = jnp.where(lane == 0, i1 - MOE_GROUPS, jnp.where(lane == 1, i2 - MOE_GROUPS, 0))
    rf_ref[...] = jnp.where(lane == 0, g0, jnp.where(lane == 1, g1, 0.0))

    oh0 = lane == (i1 - MOE_GROUPS)
    oh1 = lane == (i2 - MOE_GROUPS)
    oh = jnp.where(oh0 | oh1, 1.0, 0.0)
    earlier = jnp.where(_iota((tm, tm), 1) < _iota((tm, tm), 0), 1.0, 0.0).astype(BF16)
    seen = jnp.dot(earlier, oh.astype(BF16), preferred_element_type=F32) + carry[...]
    r0 = jnp.sum(jnp.where(oh0, seen, 0.0), axis=-1, keepdims=True)
    r1 = jnp.sum(jnp.where(oh1, seen, 0.0), axis=-1, keepdims=True)
    rank_ref[...] = jnp.where(lane == 0, r0, jnp.where(lane == 1, r1, 0.0)).astype(jnp.int32)
    carry[...] = carry[...] + jnp.sum(oh, axis=0, keepdims=True)
    cnt_ref[...] = carry[...].astype(jnp.int32)


def _out_proj_router(xt, o_a, o_b, o_c, w_out, gain, w_group, b_group, w_expert, b_expert, tm=512):
    n, d = xt.shape
    tm = min(tm, n)
    ng, ne = w_group.shape[1], w_expert.shape[1]
    wr = jnp.zeros((d, LANES), F32).at[:, 0:ng].set(w_group).at[:, ng:ng + ne].set(w_expert)
    wr_hi = wr.astype(BF16)
    wr_lo = (wr - wr_hi.astype(F32)).astype(BF16)
    br = jnp.zeros((1, LANES), F32).at[0, 0:ng].set(b_group).at[0, ng:ng + ne].set(b_expert)
    rows = lambda w: pl.BlockSpec((tm, w), lambda i: (i, 0))
    const = lambda r, w: pl.BlockSpec((r, w), lambda i: (0, 0))
    return pl.pallas_call(
        _outproj_body,
        grid=(n // tm,),
        in_specs=[rows(d), rows(o_a.shape[1]), rows(o_b.shape[1]), rows(o_c.shape[1]),
                  const(d, d), const(1, d), const(d, LANES), const(d, LANES), const(1, LANES)],
        out_specs=[rows(d), pl.BlockSpec((tm * d // LANES, LANES), lambda i: (i, 0)), rows(LANES), rows(LANES),
                   rows(LANES), const(1, LANES)],
        out_shape=[jax.ShapeDtypeStruct((n, d), F32), jax.ShapeDtypeStruct((n * d // LANES, LANES), F32),
                   jax.ShapeDtypeStruct((n, LANES), jnp.int32), jax.ShapeDtypeStruct((n, LANES), F32),
                   jax.ShapeDtypeStruct((n, LANES), jnp.int32), jax.ShapeDtypeStruct((1, LANES), jnp.int32)],
        scratch_shapes=[pltpu.VMEM((1, LANES), F32)],
        compiler_params=_cparams("arbitrary"),
        name="out_proj_router",
    )(xt, o_a, o_b, o_c, w_out.astype(BF16), gain.reshape(1, d), wr_hi, wr_lo, br)


def _padded_starts(cnt_row):
    cnt = jnp.broadcast_to(cnt_row, (8, LANES))
    padded = (((cnt + (MOE_BLOCK - 1)) // MOE_BLOCK) * MOE_BLOCK).astype(F32)
    before = jnp.where(_iota((LANES, LANES), 0) < _iota((LANES, LANES), 1), 1.0, 0.0).astype(F32)
    start = jnp.dot(padded, before, precision=HIGHEST, preferred_element_type=F32)
    return start, start + padded


def _dest_body(ri_ref, rank_ref, cnt_ref, d0_ref, d1_ref):
    tr = ri_ref.shape[0]
    nr = tr // LANES
    start, _ = _padded_starts(cnt_ref[...])
    start = start[0:1, :]
    ri = ri_ref[...]
    rank = rank_ref[...].astype(F32)
    lane = _iota((tr, LANES), 1)
    own_lane = (_iota((tr, LANES), 0) % LANES) == lane
    fold = jnp.where(_iota((nr, tr), 1) // LANES == _iota((nr, tr), 0), 1.0, 0.0).astype(F32)
    for slot, d_ref in enumerate((d0_ref, d1_ref)):
        begin = jnp.sum(jnp.where(lane == ri[:, slot:slot + 1], start, 0.0), axis=-1, keepdims=True)
        dest = begin + rank[:, slot:slot + 1]
        dense = jnp.dot(fold, jnp.where(own_lane, dest, 0.0), precision=HIGHEST, preferred_element_type=F32)
        d_ref[...] = dense.astype(jnp.int32)


def _destinations(ri, rank, cnt, tr=2048):
    n = ri.shape[0]
    tr = min(tr, n)
    rows = pl.BlockSpec((tr, LANES), lambda i: (i, 0))
    dense = pl.BlockSpec((tr // LANES, LANES), lambda i: (i, 0))
    return pl.pallas_call(
        _dest_body,
        grid=(n // tr,),
        in_specs=[rows, rows, pl.BlockSpec((1, LANES), lambda i: (0, 0))],
        out_specs=[dense, dense],
        out_shape=[jax.ShapeDtypeStruct((n // LANES, LANES), jnp.int32)] * 2,
        compiler_params=_cparams("parallel"),
        name="moe_destinations",
    )(ri, rank, cnt)


def _blkexp_body(cnt_ref, be_ref):
    nbp = be_ref.shape[0]
    _, end = _padded_starts(cnt_ref[...])
    end = end[0:1, :]
    lane = _iota((nbp, LANES), 1)
    row_start = (_iota((nbp, LANES), 0) * MOE_BLOCK).astype(F32)
    done = jnp.where((lane < MOE_EXPERTS) & (end <= row_start), 1.0, 0.0)
    be = jnp.minimum(jnp.sum(done, axis=-1, keepdims=True), MOE_EXPERTS - 1.0)
    be_ref[...] = jnp.broadcast_to(be, (nbp, LANES)).astype(jnp.int32)


def _block_experts(cnt, n_blocks):
    nbp = -(-n_blocks // 8) * 8
    out = pl.pallas_call(
        _blkexp_body,
        grid=(1,),
        in_specs=[pl.BlockSpec((1, LANES), lambda i: (0, 0))],
        out_specs=pl.BlockSpec((nbp, LANES), lambda i: (0, 0)),
        out_shape=jax.ShapeDtypeStruct((nbp, LANES), jnp.int32),
        name="moe_block_experts",
    )(cnt)
    return out[:n_blocks, 0]


def _for_each_row(n_rows, fn):
    def lane_row(i, carry):
        for j in range(LANES):
            fn(i * LANES + j, i, j)
        return carry

    lax.fori_loop(0, n_rows // LANES, lane_row, 0)


def _token_tile(ref, row, per):
    return ref.at[pl.ds(pl.multiple_of(row * per, per), per)]


def _scatter_body(per, d0_ref, d1_ref, cnt_ref, h_ref, xb_hbm, zbuf, sem, zsem):
    ts = h_ref.shape[0] // per
    n_rows = xb_hbm.shape[0] // per

    @pl.when(pl.program_id(0) == 0)
    def _():
        zbuf[...] = jnp.zeros_like(zbuf)

        def zero_rows(lo, hi):
            def start(r, carry):
                pltpu.make_async_copy(zbuf, _token_tile(xb_hbm, r, per), zsem).start()
                return carry

            def wait(r, carry):
                pltpu.make_async_copy(zbuf, _token_tile(xb_hbm, r, per), zsem).wait()
                return carry

            lax.fori_loop(lo, hi, start, 0)
            lax.fori_loop(lo, hi, wait, 0)

        def expert(e, slab_start):
            used = cnt_ref[0, e]
            padded = ((used + (MOE_BLOCK - 1)) // MOE_BLOCK) * MOE_BLOCK
            zero_rows(slab_start + used, slab_start + padded)
            return slab_start + padded

        zero_rows(lax.fori_loop(0, MOE_EXPERTS, expert, 0), n_rows)

    def copies(t, i, j):
        src = _token_tile(h_ref, t, per)
        return (pltpu.make_async_copy(src, _token_tile(xb_hbm, d0_ref[i, j], per), sem),
                pltpu.make_async_copy(src, _token_tile(xb_hbm, d1_ref[i, j], per), sem))

    def start(t, i, j):
        for prio, cp in enumerate(copies(t, i, j)):
            cp.start(priority=prio)

    def wait(t, i, j):
        for cp in copies(t, i, j):
            cp.wait()

    _for_each_row(ts, start)
    _for_each_row(ts, wait)


def _scatter_rows(h2, dest0, dest1, cnt, n, n_rows, ts):
    per = h2.shape[0] // n
    smem = pl.BlockSpec((ts // LANES, LANES), lambda i: (i, 0), memory_space=pltpu.SMEM)
    return pl.pallas_call(
        functools.partial(_scatter_body, per),
        grid=(n // ts,),
        in_specs=[smem, smem, pl.BlockSpec((1, LANES), lambda i: (0, 0), memory_space=pltpu.SMEM),
                  pl.BlockSpec((ts * per, LANES), lambda i: (i, 0))],
        out_specs=pl.BlockSpec(memory_space=pl.ANY),
        out_shape=jax.ShapeDtypeStruct((n_rows * per, LANES), F32),
        scratch_shapes=[pltpu.VMEM((per, LANES), F32), pltpu.SemaphoreType.DMA(()), pltpu.SemaphoreType.DMA(())],
        compiler_params=_cparams("arbitrary"),
        name="moe_scatter",
    )(dest0, dest1, cnt, h2)


def _expert_body(be_ref, x_ref, wg_ref, wu_ref, wd_ref, y_ref):
    del be_ref
    x = _load_token_tiles(x_ref, MOE_BLOCK)
    hid = _silu(_bdot(x, wg_ref[0, 0])) * _bdot(x, wu_ref[0, 0])
    _store_token_tiles(y_ref, _bdot(hid, wd_ref[0, 0]))


def _expert_mlp(xb, blk_expert, w_gate, w_up, w_down, layer):
    de, d = w_down.shape[2:]
    per = d // LANES
    nb = xb.shape[0] // (MOE_BLOCK * per)
    tiles = pl.BlockSpec((MOE_BLOCK * per, LANES), lambda i, be: (i, 0))
    grid_spec = pltpu.PrefetchScalarGridSpec(
        num_scalar_prefetch=1,
        grid=(nb,),
        in_specs=[tiles,
                  pl.BlockSpec((1, 1, d, de), lambda i, be: (layer, be[i], 0, 0)),
                  pl.BlockSpec((1, 1, d, de), lambda i, be: (layer, be[i], 0, 0)),
                  pl.BlockSpec((1, 1, de, d), lambda i, be: (layer, be[i], 0, 0))],
        out_specs=tiles,
    )
    return pl.pallas_call(
        _expert_body,
        grid_spec=grid_spec,
        out_shape=jax.ShapeDtypeStruct(xb.shape, F32),
        compiler_params=_cparams("arbitrary"),
        name="moe_experts",
    )(blk_expert, xb, w_gate, w_up, w_down)


def _combine_body(final_norm, d0_ref, d1_ref, rf_ref, x_ref, g_ref, yb_hbm, o_ref, gbuf, sem):
    ts, d = x_ref.shape
    per = d // LANES

    def copies(t, i, j):
        return (pltpu.make_async_copy(_token_tile(yb_hbm, d0_ref[i, j], per), _token_tile(gbuf.at[0], t, per), sem),
                pltpu.make_async_copy(_token_tile(yb_hbm, d1_ref[i, j], per), _token_tile(gbuf.at[1], t, per), sem))

    def start(t, i, j):
        for prio, cp in enumerate(copies(t, i, j)):
            cp.start(priority=prio)

    def wait(t, i, j):
        for cp in copies(t, i, j):
            cp.wait()

    _for_each_row(ts, start)
    _for_each_row(ts, wait)
    rf = rf_ref[...]
    x = x_ref[...] + (_load_token_tiles(gbuf.at[0], ts) * rf[:, 0:1] + _load_token_tiles(gbuf.at[1], ts) * rf[:, 1:2])
    if final_norm:
        x = x * lax.rsqrt(jnp.mean(x * x, axis=-1, keepdims=True) + NORM_EPS) * g_ref[...]
    o_ref[...] = x


def _combine(x_mid, yb, dest0, dest1, rf, final_gain, ts):
    n, d = x_mid.shape
    final_norm = final_gain is not None
    gain = (final_gain if final_norm else jnp.ones((d,), F32)).reshape(1, d)
    smem = pl.BlockSpec((ts // LANES, LANES), lambda i: (i, 0), memory_space=pltpu.SMEM)
    rows = lambda w: pl.BlockSpec((ts, w), lambda i: (i, 0))
    return pl.pallas_call(
        functools.partial(_combine_body, final_norm),
        grid=(n // ts,),
        in_specs=[smem, smem, rows(LANES), rows(d), pl.BlockSpec((1, d), lambda i: (0, 0)),
                  pl.BlockSpec(memory_space=pl.ANY)],
        out_specs=rows(d),
        out_shape=jax.ShapeDtypeStruct((n, d), F32),
        scratch_shapes=[pltpu.VMEM((2, ts * d // LANES, LANES), F32), pltpu.SemaphoreType.DMA(())],
        compiler_params=_cparams("arbitrary"),
        name="moe_combine",
    )(dest0, dest1, rf, x_mid, gain, yb)


def _moe(x_mid, h2, ri, rf, rank, cnt, w_gate, w_up, w_down, layer, final_gain):
    n, d = x_mid.shape
    n_assign = 2 * n
    n_blocks = (n_assign + MOE_EXPERTS * (MOE_BLOCK - 1) + MOE_BLOCK - 1) // MOE_BLOCK
    ts = min(1024, n)
    dest0, dest1 = _destinations(ri, rank, cnt)
    blk_expert = _block_experts(cnt, n_blocks)
    xb = _scatter_rows(h2, dest0, dest1, cnt, n, n_blocks * MOE_BLOCK, ts)
    yb = _expert_mlp(xb, blk_expert, w_gate, w_up, w_down, layer)
    return _combine(x_mid, yb, dest0, dest1, rf, final_gain, ts)


def kernel(x, norm_mix, norm_ffn, norm_final, w_in, w_out, rwkv_mu, rwkv_w0, rwkv_w2, rwkv_a0, rwkv_a2, rwkv_g2, rwkv_k_k, rwkv_k_a, rwkv_r_k, rwkv_lnx_w, rwkv_lnx_b, gdn_conv_w, gdn_a_log, gdn_dt_bias, gdn_norm_w, lru_conv_w, lru_conv_b, lru_w_a, lru_b_a, lru_w_x, lru_b_x, lru_lambda, moe_w_group, moe_b_group, moe_w_expert, moe_b_expert, moe_w_gate, moe_w_up, moe_w_down):
    bsz, seq, d = x.shape
    depth = w_in.shape[0]
    n = bsz * seq
    a_cols = rwkv_mu.shape[1]
    db = gdn_norm_w.shape[1] * GDN_HEADS
    dc = lru_conv_b.shape[1]
    n_ba = 2 * GDN_HEADS
    widths = (a_cols, 3 * db, db, 2 * dc, LANES)
    xt = x.reshape(n, d)
    for l in range(depth):
        w = w_in[l]
        b0 = a_cols
        c0 = b0 + 4 * db + n_ba
        w_cat = jnp.concatenate(
            [w[:, 0:a_cols], w[:, b0:b0 + 4 * db], w[:, c0:c0 + 2 * dc],
             w[:, b0 + 4 * db:c0], jnp.zeros((d, LANES - n_ba), F32)], axis=1).astype(BF16)
        p_a, p_qkv, p_z, p_c, p_ba = _in_proj(xt, seq, norm_mix[l], w_cat, widths, rwkv_mu[l], gdn_conv_w[l],
                                              lru_conv_w[l], lru_conv_b[l])
        o_a = _rwkv_group(p_a, bsz, seq, rwkv_w0[l], rwkv_w2[l], rwkv_a0[l], rwkv_a2[l], rwkv_g2[l],
                          rwkv_k_k[l], rwkv_k_a[l], rwkv_r_k[l], rwkv_lnx_w[l], rwkv_lnx_b[l])
        o_b = _gdn_group(p_qkv, p_z, p_ba, bsz, seq, gdn_a_log[l], gdn_dt_bias[l], gdn_norm_w[l])
        o_c = _lru_group(p_c, bsz, seq, lru_w_a[l], lru_b_a[l], lru_w_x[l], lru_b_x[l], lru_lambda[l])
        x_mid, h2, ri, rf, rank, cnt = _out_proj_router(xt, o_a, o_b, o_c, w_out[l], norm_ffn[l], moe_w_group[l],
                                                        moe_b_group[l], moe_w_expert[l], moe_b_expert[l])
        xt = _moe(x_mid, h2, ri, rf, rank, cnt, moe_w_gate, moe_w_up, moe_w_down, l,
                  norm_final if l == depth - 1 else None)
    return xt.reshape(bsz, seq, d)
```

```python
import functools

import jax
import jax.numpy as jnp
from jax import lax
from jax.experimental import pallas as pl
from jax.experimental.pallas import tpu as pltpu

F32 = jnp.float32
BF16 = jnp.bfloat16
HIGHEST = lax.Precision.HIGHEST

NORM_EPS = 1e-6
CONV_WIDTH = 4
CHUNK = 64
SCAN_TILE = 256
SCAN_SEQS = 2
RWKV_HEADS = 4
RWKV_GN_EPS = 64e-5
GDN_HEADS = 4
LRU_C = 8.0
MOE_GROUPS = 4
MOE_PER_GROUP = 8
MOE_EXPERTS = MOE_GROUPS * MOE_PER_GROUP
MOE_BLOCK = 512
ZERO_ROWS = 256
LANES = 128
SUBLANES = 8
NEG_BIG = -1e30
VMEM_LIMIT = 56 * 1024 * 1024


def _cparams(*sem):
    return pltpu.CompilerParams(dimension_semantics=sem, vmem_limit_bytes=VMEM_LIMIT)


def _bdot(a, b):
    return jnp.dot(a.astype(BF16), b.astype(BF16), preferred_element_type=F32)


def _bdot_nt(a, b):
    return lax.dot_general(a.astype(BF16), b.astype(BF16), (((1,), (1,)), ((), ())),
                           preferred_element_type=F32)


def _bdot_tn(a, b):
    return lax.dot_general(a.astype(BF16), b.astype(BF16), (((0,), (0,)), ((), ())),
                           preferred_element_type=F32)


def _exact_dot(a, b01):
    hi = a.astype(BF16)
    lo = (a - hi.astype(F32)).astype(BF16)
    return (jnp.dot(hi, b01, preferred_element_type=F32)
            + jnp.dot(lo, b01, preferred_element_type=F32))


def _exact_dot_left(a01, b):
    hi = b.astype(BF16)
    lo = (b - hi.astype(F32)).astype(BF16)
    return (jnp.dot(a01, hi, preferred_element_type=F32)
            + jnp.dot(a01, lo, preferred_element_type=F32))


def _load_token_tiles(ref, rows):
    per = ref.shape[0] // rows
    return jnp.concatenate([ref[pl.ds(c, rows, stride=per), :] for c in range(per)], axis=1)


def _store_token_tiles(ref, value):
    rows, d = value.shape
    per = d // LANES
    for c in range(per):
        ref[pl.ds(c, rows, stride=per), :] = value[:, c * LANES:(c + 1) * LANES]


def _softplus(x):
    return jnp.maximum(x, 0.0) + jnp.log1p(jnp.exp(-jnp.abs(x)))


def _sigmoid(x):
    return 1.0 / (1.0 + jnp.exp(-x))


def _silu(x):
    return x * _sigmoid(x)


def _iota(shape, dim):
    return lax.broadcasted_iota(jnp.int32, shape, dim)


def _head_ones(width, head_dim):
    r = _iota((width, width), 0) // head_dim
    c = _iota((width, width), 1) // head_dim
    return jnp.where(r == c, 1.0, 0.0).astype(BF16)


def _stack_heads(x, n_heads):
    c, w = x.shape
    hd = w // n_heads
    xb = x.astype(BF16)
    lane_head = _iota((c, w), 1) // hd
    return jnp.concatenate([jnp.where(lane_head == h, xb, jnp.zeros_like(xb)) for h in range(n_heads)],
                           axis=0)


def _tri_masks(c, n_heads):
    r = _iota((c, n_heads * c), 0)
    s = _iota((c, n_heads * c), 1) % c
    return s < r, s <= r


def _unit_lower_inverses(l_mats, n_heads):
    shape = l_mats[0].shape
    c = shape[0]
    eye = jnp.where(_iota(shape, 1) % c == _iota(shape, 0), 1.0, 0.0).astype(F32)
    ps = [eye + l for l in l_mats]
    ms = [_bdot(l, _stack_heads(l, n_heads)) for l in l_mats]
    span = 2
    while span < c:
        last = span * 2 >= c
        nxt = [_bdot(p if last else jnp.concatenate([m, p], axis=0), _stack_heads(m, n_heads))
               for m, p in zip(ms, ps)]
        ps = [p + (r if last else r[c:]) for p, r in zip(ps, nxt)]
        ms = [None if last else r[0:c] for r in nxt]
        span *= 2
    return ps


def _head_block_mask(width, head_dim):
    r = _iota((width, width), 0) // head_dim
    c = _iota((width, width), 1) // head_dim
    return r == c


def _shift_rows(y, tail, j):
    rolled = pltpu.roll(y, j, 0)
    head = jnp.where(_iota(tail.shape, 0) < j, pltpu.roll(tail, j, 0), rolled[0:8])
    return jnp.concatenate([head, rolled[8:]], axis=0)


def _causal_conv(y, tail, cw):
    acc = cw[CONV_WIDTH - 1:CONV_WIDTH, :] * y
    for j in range(1, CONV_WIDTH):
        acc = acc + cw[CONV_WIDTH - 1 - j:CONV_WIDTH - j, :] * _shift_rows(y, tail, j)
    return acc


def _inproj_body(tiles_per_seq, x_ref, g_ref, w_ref, mu_ref, gcw_ref, lcw_ref, lcb_ref,
                 a_ref, qkv_ref, z_ref, c_ref, ba_ref, atail, qtail, ctail):
    tm = x_ref.shape[0]
    tails = (atail, qtail, ctail)

    @pl.when(pl.program_id(0) % tiles_per_seq == 0)
    def _():
        for tail in tails:
            tail[...] = jnp.zeros_like(tail)

    x = x_ref[...]
    h = x * lax.rsqrt(jnp.mean(x * x, axis=-1, keepdims=True) + NORM_EPS) * g_ref[...]
    hb = h.astype(BF16)
    wa, wq, wz, wc = a_ref.shape[-1], qkv_ref.shape[-1], z_ref.shape[-1], c_ref.shape[-1]
    dc = ctail.shape[1]

    def proj(off, width):
        return jnp.dot(hb, w_ref[:, off:off + width], preferred_element_type=F32)

    for s in range(3):
        cols = slice(s * (wq // 3), (s + 1) * (wq // 3))
        y = proj(wa + cols.start, wq // 3)
        qkv_ref[:, cols] = _silu(_causal_conv(y, qtail[:, cols], gcw_ref[:, cols]))
        qtail[:, cols] = y[tm - 8:tm, :]
    p = proj(0, wa)
    a_ref[...] = p + (_shift_rows(p, atail[...], 1) - p) * mu_ref[...]
    atail[...] = p[tm - 8:tm, :]
    z_ref[...] = proj(wa + wq, wz)
    pc = proj(wa + wq + wz, wc)
    c_ref[:, 0:dc] = lcb_ref[...] + _causal_conv(pc[:, 0:dc], ctail[...], lcw_ref[...])
    c_ref[:, dc:] = pc[:, dc:]
    ctail[...] = pc[tm - 8:tm, 0:dc]
    ba_ref[...] = proj(wa + wq + wz + wc, ba_ref.shape[-1])


def _in_proj(xt, seq, gain, w_cat, widths, mu, gdn_conv_w, lru_conv_w, lru_conv_b, tm=512):
    n, d = xt.shape
    ntot = w_cat.shape[1]
    tm = min(tm, seq)
    dc = lru_conv_b.shape[0]
    const = lambda r, w: pl.BlockSpec((r, w), lambda i: (0, 0))
    return pl.pallas_call(
        functools.partial(_inproj_body, seq // tm),
        grid=(n // tm,),
        in_specs=[pl.BlockSpec((tm, d), lambda i: (i, 0)), const(1, d), const(d, ntot),
                  const(1, widths[0]), const(CONV_WIDTH, widths[1]), const(CONV_WIDTH, dc), const(1, dc)],
        out_specs=[pl.BlockSpec((tm, w), lambda i: (i, 0)) for w in widths],
        out_shape=[jax.ShapeDtypeStruct((n, w), F32) for w in widths],
        scratch_shapes=[pltpu.VMEM((8, widths[0]), F32), pltpu.VMEM((8, widths[1]), F32),
                        pltpu.VMEM((8, dc), F32)],
        compiler_params=_cparams("arbitrary"),
        name="in_proj",
    )(xt, gain.reshape(1, d), w_cat, mu.reshape(1, -1), gdn_conv_w, lru_conv_w, lru_conv_b.reshape(1, dc))


def _rwkv_body(xs_ref, w0_ref, a0_ref, kk_ref, ka_ref, rk_ref, lnw_ref, lnb_ref,
               w2_ref, a2_ref, g2_ref, o_ref, state):
    c = CHUNK
    n_seq, tt, da = o_ref.shape
    nh = RWKV_HEADS
    hd = da // nh

    @pl.when(pl.program_id(1) == 0)
    def _():
        state[...] = jnp.zeros_like(state)

    xs = jnp.concatenate([xs_ref[s] for s in range(n_seq)], axis=0)
    r = xs[:, 0:da]
    k = xs[:, da:2 * da]
    v = xs[:, 2 * da:3 * da]
    lo = xs[:, 3 * da:]

    w_log = -_softplus(-(w0_ref[...] + _bdot(jnp.tanh(lo), w2_ref[...]))) - 0.5
    lw = -jnp.exp(w_log)
    a = _sigmoid(a0_ref[...] + _bdot(lo, a2_ref[...]))
    g = _bdot(_sigmoid(lo), g2_ref[...])

    ones_h = _head_ones(da, hd)
    kkr = k * kk_ref[...]
    kk = kkr * lax.rsqrt(_exact_dot(kkr * kkr, ones_h) + 1e-6)
    k2 = k * (1.0 + (a - 1.0) * ka_ref[...])
    alpha = -(kk * a)

    rr = _iota((tt, tt), 0)
    cc = _iota((tt, tt), 1)
    tri = jnp.where((cc <= rr) & (cc // c == rr // c), 1.0, 0.0).astype(BF16)
    cum = jnp.concatenate([_exact_dot_left(tri, lw[s * tt:(s + 1) * tt]) for s in range(n_seq)], axis=0)
    per_seq = tt // c
    n_chunks = n_seq * per_seq
    chunk_rows = [slice(g * c, (g + 1) * c) for g in range(n_chunks)]
    cum_ends = [cum[(g + 1) * c - 1:(g + 1) * c, :] for g in range(n_chunks)]
    cum_last = jnp.concatenate([jnp.broadcast_to(ce, (c, da)) for ce in cum_ends], axis=0)
    e_pos = jnp.exp(cum)
    e_neg = jnp.exp(-cum)
    e_tail = jnp.exp(cum_last - cum)
    r_t = r * e_pos
    b_t = kk * jnp.exp(cum - lw)
    a_t = alpha * e_neg
    k_t = k2 * e_neg
    a_end = alpha * e_tail
    k_end = k2 * e_tail

    nl = nh * c
    strict, incl = _tri_masks(c, nh)
    grams = [_bdot_nt(jnp.concatenate([b_t[rows], r_t[rows]], axis=0),
                      jnp.concatenate([_stack_heads(a_t[rows], nh), _stack_heads(k_t[rows], nh)], axis=0))
             for rows in chunk_rows]
    t_invs = _unit_lower_inverses([jnp.where(strict, gm[0:c, 0:nl], 0.0) for gm in grams], nh)
    vss = [_stack_heads(v[rows], nh) for rows in chunk_rows]
    lkvs = [_bdot(jnp.where(strict, gm[0:c, nl:], 0.0), vs) for gm, vs in zip(grams, vss)]
    wus = [_bdot(t_inv, jnp.concatenate([_stack_heads(b_t[rows], nh), _stack_heads(lkv, nh)], axis=1))
           for t_inv, lkv, rows in zip(t_invs, lkvs, chunk_rows)]
    s_as = [jnp.where(incl, gm[c:, 0:nl], 0.0) for gm in grams]
    s_ks = [jnp.where(incl, gm[c:, nl:], 0.0) for gm in grams]
    block_diag = _head_block_mask(da, hd)
    w_mats = [wu[:, 0:da] for wu in wus]
    u0s = [wu[:, da:] for wu in wus]
    q_effs = [r_t[rows] + _bdot(s_a, _stack_heads(w_mat, nh)) for rows, s_a, w_mat in zip(chunk_rows, s_as, w_mats)]
    o_consts = [_bdot(jnp.concatenate([s_a, s_k], axis=1), jnp.concatenate([_stack_heads(u0, nh), vs], axis=0))
                for s_a, s_k, u0, vs in zip(s_as, s_ks, u0s, vss)]
    s_lins = [jnp.where(block_diag, _bdot_tn(w_mat, a_end[rows]), 0.0) for w_mat, rows in zip(w_mats, chunk_rows)]
    s_consts = [jnp.where(block_diag, _bdot_tn(jnp.concatenate([u0, v[rows]], axis=0),
                                               jnp.concatenate([a_end[rows], k_end[rows]], axis=0)), 0.0)
                for u0, rows in zip(u0s, chunk_rows)]

    s_curs = [state[s] for s in range(n_seq)]
    outs = [None] * n_chunks
    for step in range(per_seq):
        for s in range(n_seq):
            i = s * per_seq + step
            outs[i] = _bdot_nt(q_effs[i], s_curs[s]) + o_consts[i]
            s_curs[s] = s_curs[s] * jnp.exp(cum_ends[i]) + _bdot(s_curs[s], s_lins[i]) + s_consts[i]
    for s in range(n_seq):
        state[s] = s_curs[s]
    o = jnp.concatenate(outs, axis=0)

    inv_hd = 1.0 / hd
    mean = _exact_dot(o, ones_h) * inv_hd
    cen = o - mean
    var = _exact_dot(cen * cen, ones_h) * inv_hd
    o_n = cen * lax.rsqrt(var + RWKV_GN_EPS) * lnw_ref[...] + lnb_ref[...]
    bonus = _exact_dot(r * k2 * rk_ref[...], ones_h) * v
    res = (o_n + bonus) * g
    for s in range(n_seq):
        o_ref[s] = res[s * tt:(s + 1) * tt]


def _rwkv_group(p_a, bsz, seq, w0, w2, a0, a2, g2, k_k, k_a, r_k, lnx_w, lnx_b, tt=SCAN_TILE):
    n, a_cols = p_a.shape
    da = w0.shape[-1]
    n_lora = a_cols - 3 * da
    tt = min(tt, seq)
    nchunk = seq // tt
    d_dec, d_aaa = w2.shape[0], a2.shape[0]
    w2p = jnp.zeros((n_lora, da), F32).at[0:d_dec].set(w2).astype(BF16)
    a2p = jnp.zeros((n_lora, da), F32).at[d_dec:d_dec + d_aaa].set(a2).astype(BF16)
    g2p = jnp.zeros((n_lora, da), F32).at[d_dec + d_aaa:].set(g2).astype(BF16)
    row = lambda t: t.reshape(1, -1).astype(F32)
    vec = lambda w: pl.BlockSpec((1, w), lambda b, c: (0, 0))
    mat = pl.BlockSpec((n_lora, da), lambda b, c: (0, 0))
    nb = min(SCAN_SEQS, bsz)
    seqs = lambda w: pl.BlockSpec((nb, tt, w), lambda b, c: (b, c, 0))
    out = pl.pallas_call(
        _rwkv_body,
        grid=(bsz // nb, nchunk),
        in_specs=[seqs(a_cols)] + [vec(da)] * 7 + [mat] * 3,
        out_specs=seqs(da),
        out_shape=jax.ShapeDtypeStruct((bsz, seq, da), F32),
        scratch_shapes=[pltpu.VMEM((nb, da, da), F32)],
        compiler_params=_cparams("parallel", "arbitrary"),
        name="rwkv7",
    )(p_a.reshape(bsz, seq, a_cols), row(w0), row(a0), row(k_k), row(k_a), row(r_k), row(lnx_w),
      row(lnx_b), w2p, a2p, g2p)
    return out.reshape(n, da)


def _gdn_body(qkv_ref, z_ref, ba_ref, alog_ref, dtb_ref, nw_ref, o_ref, state):
    c = CHUNK
    n_seq, tt, db = o_ref.shape
    rows_all = n_seq * tt
    nh = GDN_HEADS
    hd = db // nh
    nl = nh * c

    @pl.when(pl.program_id(1) == 0)
    def _():
        state[...] = jnp.zeros_like(state)

    def all_rows(ref, cols=slice(None)):
        return jnp.concatenate([ref[s, :, cols] for s in range(n_seq)], axis=0)

    q = all_rows(qkv_ref, slice(0, db))
    k = all_rows(qkv_ref, slice(db, 2 * db))
    v = all_rows(qkv_ref, slice(2 * db, 3 * db))

    def per_head(col_of_head):
        return jnp.concatenate([jnp.broadcast_to(col_of_head(h), (rows_all, hd)) for h in range(nh)], axis=1)

    def head_sumsq(x):
        return per_head(lambda h: jnp.sum(x[:, h * hd:(h + 1) * hd] * x[:, h * hd:(h + 1) * hd],
                                          axis=-1, keepdims=True))

    q = q * lax.rsqrt(head_sumsq(q) + 1e-6) * (hd ** -0.5)
    k = k * lax.rsqrt(head_sumsq(k) + 1e-6)

    ba = all_rows(ba_ref)
    beta_l = _sigmoid(ba)
    g_l = -jnp.exp(alog_ref[...]) * _softplus(ba + dtb_ref[...])
    row_in_chunk = _iota((rows_all, LANES), 0) % c
    gcum_l = g_l
    d = 1
    while d < c:
        gcum_l = gcum_l + jnp.where(row_in_chunk >= d, pltpu.roll(gcum_l, d, 0), 0.0)
        d *= 2
    per_seq = tt // c
    n_chunks = n_seq * per_seq
    chunk_rows = [slice(g * c, (g + 1) * c) for g in range(n_chunks)]
    g_last_l = jnp.concatenate([jnp.broadcast_to(gcum_l[(g + 1) * c - 1:(g + 1) * c, :], (c, LANES))
                                for g in range(n_chunks)], axis=0)
    e_g_l = jnp.exp(gcum_l)
    e_tail_l = jnp.exp(g_last_l - gcum_l)
    beta = per_head(lambda h: beta_l[:, h:h + 1])
    e_g = per_head(lambda h: e_g_l[:, nh + h:nh + h + 1])
    e_tail = per_head(lambda h: e_tail_l[:, nh + h:nh + h + 1])
    kb = k * beta
    vb = v * beta
    kbe = kb * e_g
    qe = q * e_g
    k_end = k * e_tail

    strict, incl = _tri_masks(c, nh)
    lane_head = _iota((c, nl), 1) // c
    on_diag = (_iota((c, nl), 1) % c) == _iota((c, nl), 0)
    dmats = []
    for rows in chunk_rows:
        gc = gcum_l[rows]
        gcol = jnp.broadcast_to(gc[:, nh:nh + 1], (c, nl))
        for h in range(1, nh):
            gcol = jnp.where(lane_head == h, jnp.broadcast_to(gc[:, nh + h:nh + h + 1], (c, nl)), gcol)
        grow = jnp.sum(jnp.where(on_diag, gcol, 0.0), axis=0, keepdims=True)
        dmats.append(jnp.exp(jnp.where(incl, gcol - grow, NEG_BIG)))
    grams = [_bdot_nt(jnp.concatenate([kb[rows], q[rows]], axis=0), _stack_heads(k[rows], nh))
             for rows in chunk_rows]
    intras = [jnp.where(incl, gm[c:] * dm, 0.0) for gm, dm in zip(grams, dmats)]
    t_invs = _unit_lower_inverses([jnp.where(strict, -(gm[0:c] * dm), 0.0) for gm, dm in zip(grams, dmats)], nh)
    sols = [_bdot(t_inv, jnp.concatenate([_stack_heads(vb[rows], nh), _stack_heads(kbe[rows], nh)], axis=1))
            for t_inv, rows in zip(t_invs, chunk_rows)]

    s_curs = [state[s] for s in range(n_seq)]
    block_diag = _head_block_mask(db, hd)
    outs = [None] * n_chunks
    for g in range(per_seq):
        for s in range(n_seq):
            i = s * per_seq + g
            rows = chunk_rows[i]
            u_s, w_s = sols[i][:, 0:db], sols[i][:, db:]
            wq = _bdot(jnp.concatenate([w_s, qe[rows]], axis=0), s_curs[s])
            v_new = u_s - wq[0:c]
            outs[i] = wq[c:] + _bdot(intras[i], _stack_heads(v_new, nh))
            upd = _bdot_tn(k_end[rows], v_new)
            s_curs[s] = s_curs[s] * e_g[(i + 1) * c - 1:(i + 1) * c, :] + jnp.where(block_diag, upd, 0.0)
    for s in range(n_seq):
        state[s] = s_curs[s]
    o = jnp.concatenate(outs, axis=0)
    o = o * lax.rsqrt(head_sumsq(o) * (1.0 / hd) + NORM_EPS) * nw_ref[...]
    res = o * _silu(all_rows(z_ref))
    for s in range(n_seq):
        o_ref[s] = res[s * tt:(s + 1) * tt]


def _gdn_group(qkv, z, ba, bsz, seq, a_log, dt_bias, norm_w, tt=SCAN_TILE):
    n, w3 = qkv.shape
    db = z.shape[-1]
    nh = GDN_HEADS
    tt = min(tt, seq)
    nchunk = seq // tt
    alog_row = jnp.zeros((1, LANES), F32).at[0, nh:2 * nh].set(a_log)
    dtb_row = jnp.zeros((1, LANES), F32).at[0, nh:2 * nh].set(dt_bias)
    nw_row = jnp.tile(norm_w, nh).reshape(1, db)
    nb = min(SCAN_SEQS, bsz)
    seqs = lambda w: pl.BlockSpec((nb, tt, w), lambda b, c: (b, c, 0))
    const = lambda r, w: pl.BlockSpec((r, w), lambda b, c: (0, 0))
    out = pl.pallas_call(
        _gdn_body,
        grid=(bsz // nb, nchunk),
        in_specs=[seqs(w3), seqs(db), seqs(LANES), const(1, LANES), const(1, LANES), const(1, db)],
        out_specs=seqs(db),
        out_shape=jax.ShapeDtypeStruct((bsz, seq, db), F32),
        scratch_shapes=[pltpu.VMEM((nb, db, db), F32)],
        compiler_params=_cparams("parallel", "arbitrary"),
        name="gdn",
    )(qkv.reshape(bsz, seq, w3), z.reshape(bsz, seq, db), ba.reshape(bsz, seq, LANES), alog_row, dtb_row, nw_row)
    return out.reshape(n, db)


def _lru_body(p_ref, wax_ref, ba_ref, bx_ref, lam_ref, o_ref, hcar):
    tt = p_ref.shape[0]
    dc = o_ref.shape[-1]

    @pl.when(pl.program_id(1) == 0)
    def _():
        hcar[...] = jnp.zeros_like(hcar)

    conv = p_ref[:, 0:dc]
    gate = p_ref[:, dc:]

    ri = _bdot(conv, wax_ref[...])
    r = _sigmoid(ri[:, 0:dc] + ba_ref[...])
    i = _sigmoid(ri[:, dc:] + bx_ref[...])
    log_a = -LRU_C * r * _softplus(-lam_ref[...])
    a = jnp.exp(log_a)
    mult = jnp.sqrt(jnp.tanh(-log_a) * (a * a + 1.0))
    row = _iota((tt, dc), 0)
    mult = jnp.where((row == 0) & (pl.program_id(1) == 0), 1.0, mult)
    bv = conv * i * mult

    av = a
    row_in_group = row % SUBLANES
    d = 1
    while d < SUBLANES:
        a_sh = jnp.where(row_in_group >= d, pltpu.roll(av, d, 0), 1.0)
        b_sh = jnp.where(row_in_group >= d, pltpu.roll(bv, d, 0), 0.0)
        bv = av * b_sh + bv
        av = av * a_sh
        d *= 2
    carry = hcar[...]
    groups = []
    for i in range(tt // SUBLANES):
        rows = slice(i * SUBLANES, (i + 1) * SUBLANES)
        hg = av[rows] * carry + bv[rows]
        groups.append(hg)
        carry = hg[SUBLANES - 1:SUBLANES, :]
    h = jnp.concatenate(groups, axis=0)
    hcar[...] = carry
    gelu = 0.5 * gate * (1.0 + jnp.tanh(0.7978845608028654 * (gate + 0.044715 * gate * gate * gate)))
    o_ref[...] = h * gelu


def _block_diag(w):
    nb, bi, bo = w.shape
    out = jnp.zeros((nb * bi, nb * bo), w.dtype)
    for b in range(nb):
        out = out.at[b * bi:(b + 1) * bi, b * bo:(b + 1) * bo].set(w[b])
    return out


def _lru_group(p_c, bsz, seq, w_a, b_a, w_x, b_x, lam, tt=256):
    n, c_cols = p_c.shape
    dc = c_cols // 2
    tt = min(tt, seq)
    nt = seq // tt
    wax = jnp.concatenate([_block_diag(w_a), _block_diag(w_x)], axis=1).astype(BF16)
    row = lambda t: t.reshape(1, -1).astype(F32)
    const = lambda r, w: pl.BlockSpec((r, w), lambda b, c: (0, 0))
    return pl.pallas_call(
        _lru_body,
        grid=(bsz, nt),
        in_specs=[pl.BlockSpec((tt, c_cols), lambda b, c: (b * nt + c, 0)),
                  const(dc, 2 * dc), const(1, dc), const(1, dc), const(1, dc)],
        out_specs=pl.BlockSpec((tt, dc), lambda b, c: (b * nt + c, 0)),
        out_shape=jax.ShapeDtypeStruct((n, dc), F32),
        scratch_shapes=[pltpu.VMEM((1, dc), F32)],
        compiler_params=_cparams("parallel", "arbitrary"),
        name="rglru",
    )(p_c, wax, row(b_a), row(b_x), row(lam))


def _outproj_body(x_ref, ma_ref, mb_ref, mc_ref, wo_ref, g_ref, wrh_ref, wrl_ref, br_ref,
                  xo_ref, h_ref, ri_ref, rf_ref, rank_ref, cnt_ref, carry):
    da = ma_ref.shape[-1]
    db = mb_ref.shape[-1]
    tm = x_ref.shape[0]

    @pl.when(pl.program_id(0) == 0)
    def _():
        carry[...] = jnp.zeros_like(carry)

    mix = (jnp.dot(ma_ref[...].astype(BF16), wo_ref[0:da, :], preferred_element_type=F32)
           + jnp.dot(mb_ref[...].astype(BF16), wo_ref[da:da + db, :], preferred_element_type=F32)
           + jnp.dot(mc_ref[...].astype(BF16), wo_ref[da + db:, :], preferred_element_type=F32))
    x = x_ref[...] + mix
    xo_ref[...] = x
    h = x * lax.rsqrt(jnp.mean(x * x, axis=-1, keepdims=True) + NORM_EPS) * g_ref[...]
    _store_token_tiles(h_ref, h)

    h_hi = h.astype(BF16)
    h_lo = (h - h_hi.astype(F32)).astype(BF16)
    logits = (jnp.dot(h_hi, wrh_ref[...], preferred_element_type=F32)
              + jnp.dot(h_lo, wrh_ref[...], preferred_element_type=F32)
              + jnp.dot(h_hi, wrl_ref[...], preferred_element_type=F32)) + br_ref[...]
    lane = _iota((tm, LANES), 1)
    far = 4 * LANES
    gmask = lane < MOE_GROUPS
    gl = jnp.where(gmask, logits, NEG_BIG)
    gmax = jnp.max(gl, axis=-1, keepdims=True)
    gidx = jnp.min(jnp.where(gl == gmax, lane, far), axis=-1, keepdims=True)
    gsum = jnp.sum(jnp.where(gmask, jnp.exp(gl - gmax), 0.0), axis=-1, keepdims=True)
    grp_w = 1.0 / gsum
    eidx = lane - MOE_GROUPS
    emask = (eidx >= 0) & (eidx < MOE_EXPERTS) & ((eidx // MOE_PER_GROUP) == gidx)
    el = jnp.where(emask, logits, NEG_BIG)
    v1 = jnp.max(el, axis=-1, keepdims=True)
    i1 = jnp.min(jnp.where(el == v1, lane, far), axis=-1, keepdims=True)
    el2 = jnp.where(lane == i1, NEG_BIG, el)
    v2 = jnp.max(el2, axis=-1, keepdims=True)
    i2 = jnp.min(jnp.where(el2 == v2, lane, far), axis=-1, keepdims=True)
    pr = jnp.exp(v2 - v1)
    g0 = grp_w / (1.0 + pr)
    g1 = g0 * pr
    ri_ref[...] = jnp.where(lane == 0, i1 - MOE_GROUPS, jnp.where(lane == 1, i2 - MOE_GROUPS, 0))
    rf_ref[...] = jnp.where(lane == 0, g0, jnp.where(lane == 1, g1, 0.0))

    oh0 = lane == (i1 - MOE_GROUPS)
    oh1 = lane == (i2 - MOE_GROUPS)
    oh = jnp.where(oh0 | oh1, 1.0, 0.0)
    earlier = jnp.where(_iota((tm, tm), 1) < _iota((tm, tm), 0), 1.0, 0.0).astype(BF16)
    seen = jnp.dot(earlier, oh.astype(BF16), preferred_element_type=F32) + carry[...]
    r0 = jnp.sum(jnp.where(oh0, seen, 0.0), axis=-1, keepdims=True)
    r1 = jnp.sum(jnp.where(oh1, seen, 0.0), axis=-1, keepdims=True)
    rank_ref[...] = jnp.where(lane == 0, r0, jnp.where(lane == 1, r1, 0.0)).astype(jnp.int32)
    carry[...] = carry[...] + jnp.sum(oh, axis=0, keepdims=True)
    cnt_ref[...] = carry[...].astype(jnp.int32)


def _out_proj_router(xt, o_a, o_b, o_c, w_out, gain, w_group, b_group, w_expert, b_expert, tm=512):
    n, d = xt.shape
    tm = min(tm, n)
    ng, ne = w_group.shape[1], w_expert.shape[1]
    wr = jnp.zeros((d, LANES), F32).at[:, 0:ng].set(w_group).at[:, ng:ng + ne].set(w_expert)
    wr_hi = wr.astype(BF16)
    wr_lo = (wr - wr_hi.astype(F32)).astype(BF16)
    br = jnp.zeros((1, LANES), F32).at[0, 0:ng].set(b_group).at[0, ng:ng + ne].set(b_expert)
    rows = lambda w: pl.BlockSpec((tm, w), lambda i: (i, 0))
    const = lambda r, w: pl.BlockSpec((r, w), lambda i: (0, 0))
    return pl.pallas_call(
        _outproj_body,
        grid=(n // tm,),
        in_specs=[rows(d), rows(o_a.shape[1]), rows(o_b.shape[1]), rows(o_c.shape[1]),
                  const(d, d), const(1, d), const(d, LANES), const(d, LANES), const(1, LANES)],
        out_specs=[rows(d), pl.BlockSpec((tm * d // LANES, LANES), lambda i: (i, 0)), rows(LANES), rows(LANES),
                   rows(LANES), const(1, LANES)],
        out_shape=[jax.ShapeDtypeStruct((n, d), F32), jax.ShapeDtypeStruct((n * d // LANES, LANES), F32),
                   jax.ShapeDtypeStruct((n, LANES), jnp.int32), jax.ShapeDtypeStruct((n, LANES), F32),
                   jax.ShapeDtypeStruct((n, LANES), jnp.int32), jax.ShapeDtypeStruct((1, LANES), jnp.int32)],
        scratch_shapes=[pltpu.VMEM((1, LANES), F32)],
        compiler_params=_cparams("arbitrary"),
        name="out_proj_router",
    )(xt, o_a, o_b, o_c, w_out.astype(BF16), gain.reshape(1, d), wr_hi, wr_lo, br)


def _padded_starts(cnt_row):
    cnt = jnp.broadcast_to(cnt_row, (8, LANES))
    padded = (((cnt + (MOE_BLOCK - 1)) // MOE_BLOCK) * MOE_BLOCK).astype(F32)
    before = jnp.where(_iota((LANES, LANES), 0) < _iota((LANES, LANES), 1), 1.0, 0.0).astype(F32)
    start = jnp.dot(padded, before, precision=HIGHEST, preferred_element_type=F32)
    return start, start + padded


def _dest_body(ri_ref, rank_ref, cnt_ref, d0_ref, d1_ref):
    tr = ri_ref.shape[0]
    nr = tr // LANES
    start, _ = _padded_starts(cnt_ref[...])
    start = start[0:1, :]
    ri = ri_ref[...]
    rank = rank_ref[...].astype(F32)
    lane = _iota((tr, LANES), 1)
    own_lane = (_iota((tr, LANES), 0) % LANES) == lane
    fold = jnp.where(_iota((nr, tr), 1) // LANES == _iota((nr, tr), 0), 1.0, 0.0).astype(F32)
    for slot, d_ref in enumerate((d0_ref, d1_ref)):
        begin = jnp.sum(jnp.where(lane == ri[:, slot:slot + 1], start, 0.0), axis=-1, keepdims=True)
        dest = begin + rank[:, slot:slot + 1]
        dense = jnp.dot(fold, jnp.where(own_lane, dest, 0.0), precision=HIGHEST, preferred_element_type=F32)
        d_ref[...] = dense.astype(jnp.int32)


def _destinations(ri, rank, cnt, tr=2048):
    n = ri.shape[0]
    tr = min(tr, n)
    rows = pl.BlockSpec((tr, LANES), lambda i: (i, 0))
    dense = pl.BlockSpec((tr // LANES, LANES), lambda i: (i, 0))
    return pl.pallas_call(
        _dest_body,
        grid=(n // tr,),
        in_specs=[rows, rows, pl.BlockSpec((1, LANES), lambda i: (0, 0))],
        out_specs=[dense, dense],
        out_shape=[jax.ShapeDtypeStruct((n // LANES, LANES), jnp.int32)] * 2,
        compiler_params=_cparams("parallel"),
        name="moe_destinations",
    )(ri, rank, cnt)


def _blkexp_body(cnt_ref, be_ref):
    nbp = be_ref.shape[0]
    _, end = _padded_starts(cnt_ref[...])
    end = end[0:1, :]
    lane = _iota((nbp, LANES), 1)
    row_start = (_iota((nbp, LANES), 0) * MOE_BLOCK).astype(F32)
    done = jnp.where((lane < MOE_EXPERTS) & (end <= row_start), 1.0, 0.0)
    be = jnp.minimum(jnp.sum(done, axis=-1, keepdims=True), MOE_EXPERTS - 1.0)
    be_ref[...] = jnp.broadcast_to(be, (nbp, LANES)).astype(jnp.int32)


def _block_experts(cnt, n_blocks):
    nbp = -(-n_blocks // 8) * 8
    out = pl.pallas_call(
        _blkexp_body,
        grid=(1,),
        in_specs=[pl.BlockSpec((1, LANES), lambda i: (0, 0))],
        out_specs=pl.BlockSpec((nbp, LANES), lambda i: (0, 0)),
        out_shape=jax.ShapeDtypeStruct((nbp, LANES), jnp.int32),
        name="moe_block_experts",
    )(cnt)
    return out[:n_blocks, 0]


def _for_each_row(n_rows, fn):
    def lane_row(i, carry):
        for j in range(LANES):
            fn(i * LANES + j, i, j)
        return carry

    lax.fori_loop(0, n_rows // LANES, lane_row, 0)


def _token_tile(ref, row, per):
    return ref.at[pl.ds(pl.multiple_of(row * per, per), per)]


def _scatter_body(per, d0_ref, d1_ref, cnt_ref, h_ref, xb_hbm, zbuf, sem, zsem):
    ts = h_ref.shape[0] // per
    n_rows = xb_hbm.shape[0] // per

    @pl.when(pl.program_id(0) == 0)
    def _():
        zbuf[...] = jnp.zeros_like(zbuf)

        zrows = zbuf.shape[0] // per

        def zero_copy(row, k):
            dst = xb_hbm.at[pl.ds(pl.multiple_of(row * per, per), k * per)]
            return pltpu.make_async_copy(zbuf.at[pl.ds(0, k * per)], dst, zsem)

        def zero_rows(lo, hi):
            n_full = lax.shift_right_logical(hi - lo, zrows.bit_length() - 1)
            rest_lo = lo + n_full * zrows
            rest = hi - rest_lo

            def each_copy(fn):
                def full(i, carry):
                    fn(zero_copy(lo + i * zrows, zrows))
                    return carry

                lax.fori_loop(0, n_full, full, 0)
                k = zrows // 2
                while k >= 1:
                    above = lax.bitwise_and(rest, -2 * k)

                    @pl.when(lax.bitwise_and(rest, k) != 0)
                    def _(k=k, above=above):
                        fn(zero_copy(rest_lo + above, k))
                    k //= 2

            each_copy(lambda cp: cp.start())
            each_copy(lambda cp: cp.wait())

        def expert(e, slab_start):
            used = cnt_ref[0, e]
            padded = ((used + (MOE_BLOCK - 1)) // MOE_BLOCK) * MOE_BLOCK
            zero_rows(slab_start + used, slab_start + padded)
            return slab_start + padded

        zero_rows(lax.fori_loop(0, MOE_EXPERTS, expert, 0), n_rows)

    def copies(t, i, j):
        src = _token_tile(h_ref, t, per)
        return (pltpu.make_async_copy(src, _token_tile(xb_hbm, d0_ref[i, j], per), sem),
                pltpu.make_async_copy(src, _token_tile(xb_hbm, d1_ref[i, j], per), sem))

    def start(t, i, j):
        for prio, cp in enumerate(copies(t, i, j)):
            cp.start(priority=prio)

    def wait(t, i, j):
        for cp in copies(t, i, j):
            cp.wait()

    _for_each_row(ts, start)
    _for_each_row(ts, wait)


def _scatter_rows(h2, dest0, dest1, cnt, n, n_rows, ts):
    per = h2.shape[0] // n
    smem = pl.BlockSpec((ts // LANES, LANES), lambda i: (i, 0), memory_space=pltpu.SMEM)
    return pl.pallas_call(
        functools.partial(_scatter_body, per),
        grid=(n // ts,),
        in_specs=[smem, smem, pl.BlockSpec((1, LANES), lambda i: (0, 0), memory_space=pltpu.SMEM),
                  pl.BlockSpec((ts * per, LANES), lambda i: (i, 0))],
        out_specs=pl.BlockSpec(memory_space=pl.ANY),
        out_shape=jax.ShapeDtypeStruct((n_rows * per, LANES), F32),
        scratch_shapes=[pltpu.VMEM((ZERO_ROWS * per, LANES), F32), pltpu.SemaphoreType.DMA(()),
                        pltpu.SemaphoreType.DMA(())],
        compiler_params=_cparams("arbitrary"),
        name="moe_scatter",
    )(dest0, dest1, cnt, h2)


def _expert_body(be_ref, x_ref, wg_ref, wu_ref, wd_ref, y_ref):
    del be_ref
    x = _load_token_tiles(x_ref, MOE_BLOCK)
    hid = _silu(_bdot(x, wg_ref[0, 0])) * _bdot(x, wu_ref[0, 0])
    _store_token_tiles(y_ref, _bdot(hid, wd_ref[0, 0]))


def _expert_mlp(xb, blk_expert, w_gate, w_up, w_down, layer):
    de, d = w_down.shape[2:]
    per = d // LANES
    nb = xb.shape[0] // (MOE_BLOCK * per)
    tiles = pl.BlockSpec((MOE_BLOCK * per, LANES), lambda i, be: (i, 0))
    grid_spec = pltpu.PrefetchScalarGridSpec(
        num_scalar_prefetch=1,
        grid=(nb,),
        in_specs=[tiles,
                  pl.BlockSpec((1, 1, d, de), lambda i, be: (layer, be[i], 0, 0)),
                  pl.BlockSpec((1, 1, d, de), lambda i, be: (layer, be[i], 0, 0)),
                  pl.BlockSpec((1, 1, de, d), lambda i, be: (layer, be[i], 0, 0))],
        out_specs=tiles,
    )
    return pl.pallas_call(
        _expert_body,
        grid_spec=grid_spec,
        out_shape=jax.ShapeDtypeStruct(xb.shape, F32),
        compiler_params=_cparams("arbitrary"),
        name="moe_experts",
    )(blk_expert, xb, w_gate, w_up, w_down)


def _combine_body(final_norm, d0_ref, d1_ref, rf_ref, x_ref, g_ref, yb_hbm, o_ref, gbuf, sem):
    ts, d = x_ref.shape
    per = d // LANES

    def copies(t, i, j):
        return (pltpu.make_async_copy(_token_tile(yb_hbm, d0_ref[i, j], per), _token_tile(gbuf.at[0], t, per), sem),
                pltpu.make_async_copy(_token_tile(yb_hbm, d1_ref[i, j], per), _token_tile(gbuf.at[1], t, per), sem))

    def start(t, i, j):
        for prio, cp in enumerate(copies(t, i, j)):
            cp.start(priority=prio)

    def wait(t, i, j):
        for cp in copies(t, i, j):
            cp.wait()

    _for_each_row(ts, start)
    _for_each_row(ts, wait)
    rf = rf_ref[...]
    x = x_ref[...] + (_load_token_tiles(gbuf.at[0], ts) * rf[:, 0:1] + _load_token_tiles(gbuf.at[1], ts) * rf[:, 1:2])
    if final_norm:
        x = x * lax.rsqrt(jnp.mean(x * x, axis=-1, keepdims=True) + NORM_EPS) * g_ref[...]
    o_ref[...] = x


def _combine(x_mid, yb, dest0, dest1, rf, final_gain, ts):
    n, d = x_mid.shape
    final_norm = final_gain is not None
    gain = (final_gain if final_norm else jnp.ones((d,), F32)).reshape(1, d)
    smem = pl.BlockSpec((ts // LANES, LANES), lambda i: (i, 0), memory_space=pltpu.SMEM)
    rows = lambda w: pl.BlockSpec((ts, w), lambda i: (i, 0))
    return pl.pallas_call(
        functools.partial(_combine_body, final_norm),
        grid=(n // ts,),
        in_specs=[smem, smem, rows(LANES), rows(d), pl.BlockSpec((1, d), lambda i: (0, 0)),
                  pl.BlockSpec(memory_space=pl.ANY)],
        out_specs=rows(d),
        out_shape=jax.ShapeDtypeStruct((n, d), F32),
        scratch_shapes=[pltpu.VMEM((2, ts * d // LANES, LANES), F32), pltpu.SemaphoreType.DMA(())],
        compiler_params=_cparams("arbitrary"),
        name="moe_combine",
    )(dest0, dest1, rf, x_mid, gain, yb)


def _moe(x_mid, h2, ri, rf, rank, cnt, w_gate, w_up, w_down, layer, final_gain):
    n, d = x_mid.shape
    n_assign = 2 * n
    n_blocks = (n_assign + MOE_EXPERTS * (MOE_BLOCK - 1) + MOE_BLOCK - 1) // MOE_BLOCK
    ts = min(1024, n)
    dest0, dest1 = _destinations(ri, rank, cnt)
    blk_expert = _block_experts(cnt, n_blocks)
    xb = _scatter_rows(h2, dest0, dest1, cnt, n, n_blocks * MOE_BLOCK, ts)
    yb = _expert_mlp(xb, blk_expert, w_gate, w_up, w_down, layer)
    return _combine(x_mid, yb, dest0, dest1, rf, final_gain, ts)


def kernel(x, norm_mix, norm_ffn, norm_final, w_in, w_out, rwkv_mu, rwkv_w0, rwkv_w2, rwkv_a0, rwkv_a2, rwkv_g2, rwkv_k_k, rwkv_k_a, rwkv_r_k, rwkv_lnx_w, rwkv_lnx_b, gdn_conv_w, gdn_a_log, gdn_dt_bias, gdn_norm_w, lru_conv_w, lru_conv_b, lru_w_a, lru_b_a, lru_w_x, lru_b_x, lru_lambda, moe_w_group, moe_b_group, moe_w_expert, moe_b_expert, moe_w_gate, moe_w_up, moe_w_down):
    bsz, seq, d = x.shape
    depth = w_in.shape[0]
    n = bsz * seq
    a_cols = rwkv_mu.shape[1]
    db = gdn_norm_w.shape[1] * GDN_HEADS
    dc = lru_conv_b.shape[1]
    n_ba = 2 * GDN_HEADS
    widths = (a_cols, 3 * db, db, 2 * dc, LANES)
    xt = x.reshape(n, d)
    for l in range(depth):
        w = w_in[l]
        b0 = a_cols
        c0 = b0 + 4 * db + n_ba
        w_cat = jnp.concatenate(
            [w[:, 0:a_cols], w[:, b0:b0 + 4 * db], w[:, c0:c0 + 2 * dc],
             w[:, b0 + 4 * db:c0], jnp.zeros((d, LANES - n_ba), F32)], axis=1).astype(BF16)
        p_a, p_qkv, p_z, p_c, p_ba = _in_proj(xt, seq, norm_mix[l], w_cat, widths, rwkv_mu[l], gdn_conv_w[l],
                                              lru_conv_w[l], lru_conv_b[l])
        o_a = _rwkv_group(p_a, bsz, seq, rwkv_w0[l], rwkv_w2[l], rwkv_a0[l], rwkv_a2[l], rwkv_g2[l],
                          rwkv_k_k[l], rwkv_k_a[l], rwkv_r_k[l], rwkv_lnx_w[l], rwkv_lnx_b[l])
        o_b = _gdn_group(p_qkv, p_z, p_ba, bsz, seq, gdn_a_log[l], gdn_dt_bias[l], gdn_norm_w[l])
        o_c = _lru_group(p_c, bsz, seq, lru_w_a[l], lru_b_a[l], lru_w_x[l], lru_b_x[l], lru_lambda[l])
        x_mid, h2, ri, rf, rank, cnt = _out_proj_router(xt, o_a, o_b, o_c, w_out[l], norm_ffn[l], moe_w_group[l],
                                                        moe_b_group[l], moe_w_expert[l], moe_b_expert[l])
        xt = _moe(x_mid, h2, ri, rf, rank, cnt, moe_w_gate, moe_w_up, moe_w_down, l,
                  norm_final if l == depth - 1 else None)
    return xt.reshape(bsz, seq, d)
```

```python
import functools

import jax
import jax.numpy as jnp
from jax import lax
from jax.experimental import pallas as pl
from jax.experimental.pallas import tpu as pltpu

F32 = jnp.float32
BF16 = jnp.bfloat16
HIGHEST = lax.Precision.HIGHEST

NORM_EPS = 1e-6
CONV_WIDTH = 4
CHUNK = 64
SCAN_TILE = 256
SCAN_SEQS = 2
RWKV_HEADS = 4
RWKV_GN_EPS = 64e-5
GDN_HEADS = 4
LRU_C = 8.0
MOE_GROUPS = 4
MOE_PER_GROUP = 8
MOE_EXPERTS = MOE_GROUPS * MOE_PER_GROUP
MOE_BLOCK = 512
ZERO_ROWS = 256
LANES = 128
SUBLANES = 8
NEG_BIG = -1e30
VMEM_LIMIT = 56 * 1024 * 1024


def _cparams(*sem):
    return pltpu.CompilerParams(dimension_semantics=sem, vmem_limit_bytes=VMEM_LIMIT)


def _bdot(a, b):
    return jnp.dot(a.astype(BF16), b.astype(BF16), preferred_element_type=F32)


def _bdot_nt(a, b):
    return lax.dot_general(a.astype(BF16), b.astype(BF16), (((1,), (1,)), ((), ())),
                           preferred_element_type=F32)


def _bdot_tn(a, b):
    return lax.dot_general(a.astype(BF16), b.astype(BF16), (((0,), (0,)), ((), ())),
                           preferred_element_type=F32)


def _exact_dot(a, b01):
    hi = a.astype(BF16)
    lo = (a - hi.astype(F32)).astype(BF16)
    return (jnp.dot(hi, b01, preferred_element_type=F32)
            + jnp.dot(lo, b01, preferred_element_type=F32))


def _exact_dot_left(a01, b):
    hi = b.astype(BF16)
    lo = (b - hi.astype(F32)).astype(BF16)
    return (jnp.dot(a01, hi, preferred_element_type=F32)
            + jnp.dot(a01, lo, preferred_element_type=F32))


def _load_token_tiles(ref, rows):
    per = ref.shape[0] // rows
    return jnp.concatenate([ref[pl.ds(c, rows, stride=per), :] for c in range(per)], axis=1)


def _store_token_tiles(ref, value):
    rows, d = value.shape
    per = d // LANES
    for c in range(per):
        ref[pl.ds(c, rows, stride=per), :] = value[:, c * LANES:(c + 1) * LANES]


def _softplus(x):
    return jnp.maximum(x, 0.0) + jnp.log1p(jnp.exp(-jnp.abs(x)))


def _sigmoid(x):
    return 1.0 / (1.0 + jnp.exp(-x))


def _silu(x):
    return x * _sigmoid(x)


def _iota(shape, dim):
    return lax.broadcasted_iota(jnp.int32, shape, dim)


def _head_ones(width, head_dim):
    r = _iota((width, width), 0) // head_dim
    c = _iota((width, width), 1) // head_dim
    return jnp.where(r == c, 1.0, 0.0).astype(BF16)


def _stack_heads(x, n_heads):
    c, w = x.shape
    hd = w // n_heads
    xb = x.astype(BF16)
    lane_head = _iota((c, w), 1) // hd
    return jnp.concatenate([jnp.where(lane_head == h, xb, jnp.zeros_like(xb)) for h in range(n_heads)],
                           axis=0)


def _tri_masks(c, n_heads):
    r = _iota((c, n_heads * c), 0)
    s = _iota((c, n_heads * c), 1) % c
    return s < r, s <= r


def _unit_lower_inverses(l_mats, n_heads):
    shape = l_mats[0].shape
    c = shape[0]
    eye = jnp.where(_iota(shape, 1) % c == _iota(shape, 0), 1.0, 0.0).astype(F32)
    ps = [eye + l for l in l_mats]
    ms = [_bdot(l, _stack_heads(l, n_heads)) for l in l_mats]
    span = 2
    while span < c:
        last = span * 2 >= c
        nxt = [_bdot(p if last else jnp.concatenate([m, p], axis=0), _stack_heads(m, n_heads))
               for m, p in zip(ms, ps)]
        ps = [p + (r if last else r[c:]) for p, r in zip(ps, nxt)]
        ms = [None if last else r[0:c] for r in nxt]
        span *= 2
    return ps


def _head_block_mask(width, head_dim):
    r = _iota((width, width), 0) // head_dim
    c = _iota((width, width), 1) // head_dim
    return r == c


def _shift_rows(y, tail, j):
    rolled = pltpu.roll(y, j, 0)
    head = jnp.where(_iota(tail.shape, 0) < j, pltpu.roll(tail, j, 0), rolled[0:8])
    return jnp.concatenate([head, rolled[8:]], axis=0)


def _causal_conv(y, tail, cw):
    acc = cw[CONV_WIDTH - 1:CONV_WIDTH, :] * y
    for j in range(1, CONV_WIDTH):
        acc = acc + cw[CONV_WIDTH - 1 - j:CONV_WIDTH - j, :] * _shift_rows(y, tail, j)
    return acc


def _inproj_body(tiles_per_seq, x_ref, g_ref, w_ref, mu_ref, gcw_ref, lcw_ref, lcb_ref,
                 a_ref, qkv_ref, z_ref, c_ref, ba_ref, atail, qtail, ctail):
    tm = x_ref.shape[0]
    tails = (atail, qtail, ctail)

    @pl.when(pl.program_id(0) % tiles_per_seq == 0)
    def _():
        for tail in tails:
            tail[...] = jnp.zeros_like(tail)

    x = x_ref[...]
    h = x * lax.rsqrt(jnp.mean(x * x, axis=-1, keepdims=True) + NORM_EPS) * g_ref[...]
    hb = h.astype(BF16)
    wa, wq, wz, wc = a_ref.shape[-1], qkv_ref.shape[-1], z_ref.shape[-1], c_ref.shape[-1]
    dc = ctail.shape[1]

    def proj(off, width):
        return jnp.dot(hb, w_ref[:, off:off + width], preferred_element_type=F32)

    for s in range(3):
        cols = slice(s * (wq // 3), (s + 1) * (wq // 3))
        y = proj(wa + cols.start, wq // 3)
        qkv_ref[:, cols] = _silu(_causal_conv(y, qtail[:, cols], gcw_ref[:, cols]))
        qtail[:, cols] = y[tm - 8:tm, :]
    p = proj(0, wa)
    a_ref[...] = p + (_shift_rows(p, atail[...], 1) - p) * mu_ref[...]
    atail[...] = p[tm - 8:tm, :]
    z_ref[...] = proj(wa + wq, wz)
    pc = proj(wa + wq + wz, wc)
    c_ref[:, 0:dc] = lcb_ref[...] + _causal_conv(pc[:, 0:dc], ctail[...], lcw_ref[...])
    c_ref[:, dc:] = pc[:, dc:]
    ctail[...] = pc[tm - 8:tm, 0:dc]
    ba_ref[...] = proj(wa + wq + wz + wc, ba_ref.shape[-1])


def _in_proj(xt, seq, gain, w_cat, widths, mu, gdn_conv_w, lru_conv_w, lru_conv_b, tm=512):
    n, d = xt.shape
    ntot = w_cat.shape[1]
    tm = min(tm, seq)
    dc = lru_conv_b.shape[0]
    const = lambda r, w: pl.BlockSpec((r, w), lambda i: (0, 0))
    return pl.pallas_call(
        functools.partial(_inproj_body, seq // tm),
        grid=(n // tm,),
        in_specs=[pl.BlockSpec((tm, d), lambda i: (i, 0)), const(1, d), const(d, ntot),
                  const(1, widths[0]), const(CONV_WIDTH, widths[1]), const(CONV_WIDTH, dc), const(1, dc)],
        out_specs=[pl.BlockSpec((tm, w), lambda i: (i, 0)) for w in widths],
        out_shape=[jax.ShapeDtypeStruct((n, w), F32) for w in widths],
        scratch_shapes=[pltpu.VMEM((8, widths[0]), F32), pltpu.VMEM((8, widths[1]), F32),
                        pltpu.VMEM((8, dc), F32)],
        compiler_params=_cparams("arbitrary"),
        name="in_proj",
    )(xt, gain.reshape(1, d), w_cat, mu.reshape(1, -1), gdn_conv_w, lru_conv_w, lru_conv_b.reshape(1, dc))


def _rwkv_body(xs_ref, w0_ref, a0_ref, kk_ref, ka_ref, rk_ref, lnw_ref, lnb_ref,
               w2_ref, a2_ref, g2_ref, o_ref, state):
    c = CHUNK
    n_seq, tt, da = o_ref.shape
    nh = RWKV_HEADS
    hd = da // nh

    @pl.when(pl.program_id(1) == 0)
    def _():
        state[...] = jnp.zeros_like(state)

    xs = jnp.concatenate([xs_ref[s] for s in range(n_seq)], axis=0)
    r = xs[:, 0:da]
    k = xs[:, da:2 * da]
    v = xs[:, 2 * da:3 * da]
    lo = xs[:, 3 * da:]

    w_log = -_softplus(-(w0_ref[...] + _bdot(jnp.tanh(lo), w2_ref[...]))) - 0.5
    lw = -jnp.exp(w_log)
    a = _sigmoid(a0_ref[...] + _bdot(lo, a2_ref[...]))
    g = _bdot(_sigmoid(lo), g2_ref[...])

    ones_h = _head_ones(da, hd)
    kkr = k * kk_ref[...]
    kk = kkr * lax.rsqrt(_exact_dot(kkr * kkr, ones_h) + 1e-6)
    k2 = k * (1.0 + (a - 1.0) * ka_ref[...])
    alpha = -(kk * a)

    rr = _iota((tt, tt), 0)
    cc = _iota((tt, tt), 1)
    tri = jnp.where((cc <= rr) & (cc // c == rr // c), 1.0, 0.0).astype(BF16)
    cum = jnp.concatenate([_exact_dot_left(tri, lw[s * tt:(s + 1) * tt]) for s in range(n_seq)], axis=0)
    per_seq = tt // c
    n_chunks = n_seq * per_seq
    chunk_rows = [slice(g * c, (g + 1) * c) for g in range(n_chunks)]
    cum_ends = [cum[(g + 1) * c - 1:(g + 1) * c, :] for g in range(n_chunks)]
    cum_last = jnp.concatenate([jnp.broadcast_to(ce, (c, da)) for ce in cum_ends], axis=0)
    e_pos = jnp.exp(cum)
    e_neg = jnp.exp(-cum)
    e_tail = jnp.exp(cum_last - cum)
    r_t = r * e_pos
    b_t = kk * jnp.exp(cum - lw)
    a_t = alpha * e_neg
    k_t = k2 * e_neg
    a_end = alpha * e_tail
    k_end = k2 * e_tail

    nl = nh * c
    strict, incl = _tri_masks(c, nh)
    grams = [_bdot_nt(jnp.concatenate([b_t[rows], r_t[rows]], axis=0),
                      jnp.concatenate([_stack_heads(a_t[rows], nh), _stack_heads(k_t[rows], nh)], axis=0))
             for rows in chunk_rows]
    t_invs = _unit_lower_inverses([jnp.where(strict, gm[0:c, 0:nl], 0.0) for gm in grams], nh)
    vss = [_stack_heads(v[rows], nh) for rows in chunk_rows]
    lkvs = [_bdot(jnp.where(strict, gm[0:c, nl:], 0.0), vs) for gm, vs in zip(grams, vss)]
    wus = [_bdot(t_inv, jnp.concatenate([_stack_heads(b_t[rows], nh), _stack_heads(lkv, nh)], axis=1))
           for t_inv, lkv, rows in zip(t_invs, lkvs, chunk_rows)]
    s_as = [jnp.where(incl, gm[c:, 0:nl], 0.0) for gm in grams]
    s_ks = [jnp.where(incl, gm[c:, nl:], 0.0) for gm in grams]
    block_diag = _head_block_mask(da, hd)
    w_mats = [wu[:, 0:da] for wu in wus]
    u0s = [wu[:, da:] for wu in wus]
    q_effs = [r_t[rows] + _bdot(s_a, _stack_heads(w_mat, nh)) for rows, s_a, w_mat in zip(chunk_rows, s_as, w_mats)]
    o_consts = [_bdot(jnp.concatenate([s_a, s_k], axis=1), jnp.concatenate([_stack_heads(u0, nh), vs], axis=0))
                for s_a, s_k, u0, vs in zip(s_as, s_ks, u0s, vss)]
    s_lins = [jnp.where(block_diag, _bdot_tn(w_mat, a_end[rows]), 0.0) for w_mat, rows in zip(w_mats, chunk_rows)]
    s_consts = [jnp.where(block_diag, _bdot_tn(jnp.concatenate([u0, v[rows]], axis=0),
                                               jnp.concatenate([a_end[rows], k_end[rows]], axis=0)), 0.0)
                for u0, rows in zip(u0s, chunk_rows)]

    s_curs = [state[s] for s in range(n_seq)]
    outs = [None] * n_chunks
    for step in range(per_seq):
        for s in range(n_seq):
            i = s * per_seq + step
            outs[i] = _bdot_nt(q_effs[i], s_curs[s]) + o_consts[i]
            s_curs[s] = s_curs[s] * jnp.exp(cum_ends[i]) + _bdot(s_curs[s], s_lins[i]) + s_consts[i]
    for s in range(n_seq):
        state[s] = s_curs[s]
    o = jnp.concatenate(outs, axis=0)

    inv_hd = 1.0 / hd
    mean = _exact_dot(o, ones_h) * inv_hd
    cen = o - mean
    var = _exact_dot(cen * cen, ones_h) * inv_hd
    o_n = cen * lax.rsqrt(var + RWKV_GN_EPS) * lnw_ref[...] + lnb_ref[...]
    bonus = _exact_dot(r * k2 * rk_ref[...], ones_h) * v
    res = (o_n + bonus) * g
    for s in range(n_seq):
        o_ref[s] = res[s * tt:(s + 1) * tt]


def _rwkv_group(p_a, bsz, seq, w0, w2, a0, a2, g2, k_k, k_a, r_k, lnx_w, lnx_b, tt=SCAN_TILE):
    n, a_cols = p_a.shape
    da = w0.shape[-1]
    n_lora = a_cols - 3 * da
    tt = min(tt, seq)
    nchunk = seq // tt
    d_dec, d_aaa = w2.shape[0], a2.shape[0]
    w2p = jnp.zeros((n_lora, da), F32).at[0:d_dec].set(w2).astype(BF16)
    a2p = jnp.zeros((n_lora, da), F32).at[d_dec:d_dec + d_aaa].set(a2).astype(BF16)
    g2p = jnp.zeros((n_lora, da), F32).at[d_dec + d_aaa:].set(g2).astype(BF16)
    row = lambda t: t.reshape(1, -1).astype(F32)
    vec = lambda w: pl.BlockSpec((1, w), lambda b, c: (0, 0))
    mat = pl.BlockSpec((n_lora, da), lambda b, c: (0, 0))
    nb = min(SCAN_SEQS, bsz)
    seqs = lambda w: pl.BlockSpec((nb, tt, w), lambda b, c: (b, c, 0))
    out = pl.pallas_call(
        _rwkv_body,
        grid=(bsz // nb, nchunk),
        in_specs=[seqs(a_cols)] + [vec(da)] * 7 + [mat] * 3,
        out_specs=seqs(da),
        out_shape=jax.ShapeDtypeStruct((bsz, seq, da), F32),
        scratch_shapes=[pltpu.VMEM((nb, da, da), F32)],
        compiler_params=_cparams("parallel", "arbitrary"),
        name="rwkv7",
    )(p_a.reshape(bsz, seq, a_cols), row(w0), row(a0), row(k_k), row(k_a), row(r_k), row(lnx_w),
      row(lnx_b), w2p, a2p, g2p)
    return out.reshape(n, da)


def _gdn_body(qkv_ref, z_ref, ba_ref, alog_ref, dtb_ref, nw_ref, o_ref, state):
    c = CHUNK
    n_seq, tt, db = o_ref.shape
    rows_all = n_seq * tt
    nh = GDN_HEADS
    hd = db // nh
    nl = nh * c

    @pl.when(pl.program_id(1) == 0)
    def _():
        state[...] = jnp.zeros_like(state)

    def all_rows(ref, cols=slice(None)):
        return jnp.concatenate([ref[s, :, cols] for s in range(n_seq)], axis=0)

    q = all_rows(qkv_ref, slice(0, db))
    k = all_rows(qkv_ref, slice(db, 2 * db))
    v = all_rows(qkv_ref, slice(2 * db, 3 * db))

    def per_head(col_of_head):
        return jnp.concatenate([jnp.broadcast_to(col_of_head(h), (rows_all, hd)) for h in range(nh)], axis=1)

    def head_sumsq(x):
        return per_head(lambda h: jnp.sum(x[:, h * hd:(h + 1) * hd] * x[:, h * hd:(h + 1) * hd],
                                          axis=-1, keepdims=True))

    q = q * lax.rsqrt(head_sumsq(q) + 1e-6) * (hd ** -0.5)
    k = k * lax.rsqrt(head_sumsq(k) + 1e-6)

    ba = all_rows(ba_ref)
    beta_l = _sigmoid(ba)
    g_l = -jnp.exp(alog_ref[...]) * _softplus(ba + dtb_ref[...])
    row_in_chunk = _iota((rows_all, LANES), 0) % c
    gcum_l = g_l
    d = 1
    while d < c:
        gcum_l = gcum_l + jnp.where(row_in_chunk >= d, pltpu.roll(gcum_l, d, 0), 0.0)
        d *= 2
    per_seq = tt // c
    n_chunks = n_seq * per_seq
    chunk_rows = [slice(g * c, (g + 1) * c) for g in range(n_chunks)]
    g_last_l = jnp.concatenate([jnp.broadcast_to(gcum_l[(g + 1) * c - 1:(g + 1) * c, :], (c, LANES))
                                for g in range(n_chunks)], axis=0)
    e_g_l = jnp.exp(gcum_l)
    e_tail_l = jnp.exp(g_last_l - gcum_l)
    beta = per_head(lambda h: beta_l[:, h:h + 1])
    e_g = per_head(lambda h: e_g_l[:, nh + h:nh + h + 1])
    e_tail = per_head(lambda h: e_tail_l[:, nh + h:nh + h + 1])
    kb = k * beta
    vb = v * beta
    kbe = kb * e_g
    qe = q * e_g
    k_end = k * e_tail

    strict, incl = _tri_masks(c, nh)
    lane_head = _iota((c, nl), 1) // c
    on_diag = (_iota((c, nl), 1) % c) == _iota((c, nl), 0)
    dmats = []
    for rows in chunk_rows:
        gc = gcum_l[rows]
        gcol = jnp.broadcast_to(gc[:, nh:nh + 1], (c, nl))
        for h in range(1, nh):
            gcol = jnp.where(lane_head == h, jnp.broadcast_to(gc[:, nh + h:nh + h + 1], (c, nl)), gcol)
        grow = jnp.sum(jnp.where(on_diag, gcol, 0.0), axis=0, keepdims=True)
        dmats.append(jnp.exp(jnp.where(incl, gcol - grow, NEG_BIG)))
    grams = [_bdot_nt(jnp.concatenate([kb[rows], q[rows]], axis=0), _stack_heads(k[rows], nh))
             for rows in chunk_rows]
    intras = [jnp.where(incl, gm[c:] * dm, 0.0) for gm, dm in zip(grams, dmats)]
    t_invs = _unit_lower_inverses([jnp.where(strict, -(gm[0:c] * dm), 0.0) for gm, dm in zip(grams, dmats)], nh)
    sols = [_bdot(t_inv, jnp.concatenate([_stack_heads(vb[rows], nh), _stack_heads(kbe[rows], nh)], axis=1))
            for t_inv, rows in zip(t_invs, chunk_rows)]

    s_curs = [state[s] for s in range(n_seq)]
    block_diag = _head_block_mask(db, hd)
    outs = [None] * n_chunks
    for g in range(per_seq):
        for s in range(n_seq):
            i = s * per_seq + g
            rows = chunk_rows[i]
            u_s, w_s = sols[i][:, 0:db], sols[i][:, db:]
            wq = _bdot(jnp.concatenate([w_s, qe[rows]], axis=0), s_curs[s])
            v_new = u_s - wq[0:c]
            outs[i] = wq[c:] + _bdot(intras[i], _stack_heads(v_new, nh))
            upd = _bdot_tn(k_end[rows], v_new)
            s_curs[s] = s_curs[s] * e_g[(i + 1) * c - 1:(i + 1) * c, :] + jnp.where(block_diag, upd, 0.0)
    for s in range(n_seq):
        state[s] = s_curs[s]
    o = jnp.concatenate(outs, axis=0)
    o = o * lax.rsqrt(head_sumsq(o) * (1.0 / hd) + NORM_EPS) * nw_ref[...]
    res = o * _silu(all_rows(z_ref))
    for s in range(n_seq):
        o_ref[s] = res[s * tt:(s + 1) * tt]


def _gdn_group(qkv, z, ba, bsz, seq, a_log, dt_bias, norm_w, tt=SCAN_TILE):
    n, w3 = qkv.shape
    db = z.shape[-1]
    nh = GDN_HEADS
    tt = min(tt, seq)
    nchunk = seq // tt
    alog_row = jnp.zeros((1, LANES), F32).at[0, nh:2 * nh].set(a_log)
    dtb_row = jnp.zeros((1, LANES), F32).at[0, nh:2 * nh].set(dt_bias)
    nw_row = jnp.tile(norm_w, nh).reshape(1, db)
    nb = min(SCAN_SEQS, bsz)
    seqs = lambda w: pl.BlockSpec((nb, tt, w), lambda b, c: (b, c, 0))
    const = lambda r, w: pl.BlockSpec((r, w), lambda b, c: (0, 0))
    out = pl.pallas_call(
        _gdn_body,
        grid=(bsz // nb, nchunk),
        in_specs=[seqs(w3), seqs(db), seqs(LANES), const(1, LANES), const(1, LANES), const(1, db)],
        out_specs=seqs(db),
        out_shape=jax.ShapeDtypeStruct((bsz, seq, db), F32),
        scratch_shapes=[pltpu.VMEM((nb, db, db), F32)],
        compiler_params=_cparams("parallel", "arbitrary"),
        name="gdn",
    )(qkv.reshape(bsz, seq, w3), z.reshape(bsz, seq, db), ba.reshape(bsz, seq, LANES), alog_row, dtb_row, nw_row)
    return out.reshape(n, db)


def _lru_body(p_ref, wax_ref, ba_ref, bx_ref, lam_ref, o_ref, hcar):
    tt = p_ref.shape[0]
    dc = o_ref.shape[-1]

    @pl.when(pl.program_id(1) == 0)
    def _():
        hcar[...] = jnp.zeros_like(hcar)

    conv = p_ref[:, 0:dc]
    gate = p_ref[:, dc:]

    ri = _bdot(conv, wax_ref[...])
    r = _sigmoid(ri[:, 0:dc] + ba_ref[...])
    i = _sigmoid(ri[:, dc:] + bx_ref[...])
    log_a = -LRU_C * r * _softplus(-lam_ref[...])
    a = jnp.exp(log_a)
    mult = jnp.sqrt(jnp.tanh(-log_a) * (a * a + 1.0))
    row = _iota((tt, dc), 0)
    mult = jnp.where((row == 0) & (pl.program_id(1) == 0), 1.0, mult)
    bv = conv * i * mult

    av = a
    row_in_group = row % SUBLANES
    d = 1
    while d < SUBLANES:
        a_sh = jnp.where(row_in_group >= d, pltpu.roll(av, d, 0), 1.0)
        b_sh = jnp.where(row_in_group >= d, pltpu.roll(bv, d, 0), 0.0)
        bv = av * b_sh + bv
        av = av * a_sh
        d *= 2
    carry = hcar[...]
    groups = []
    for i in range(tt // SUBLANES):
        rows = slice(i * SUBLANES, (i + 1) * SUBLANES)
        hg = av[rows] * carry + bv[rows]
        groups.append(hg)
        carry = hg[SUBLANES - 1:SUBLANES, :]
    h = jnp.concatenate(groups, axis=0)
    hcar[...] = carry
    gelu = 0.5 * gate * (1.0 + jnp.tanh(0.7978845608028654 * (gate + 0.044715 * gate * gate * gate)))
    o_ref[...] = h * gelu


def _block_diag(w):
    nb, bi, bo = w.shape
    out = jnp.zeros((nb * bi, nb * bo), w.dtype)
    for b in range(nb):
        out = out.at[b * bi:(b + 1) * bi, b * bo:(b + 1) * bo].set(w[b])
    return out


def _lru_group(p_c, bsz, seq, w_a, b_a, w_x, b_x, lam, tt=256):
    n, c_cols = p_c.shape
    dc = c_cols // 2
    tt = min(tt, seq)
    nt = seq // tt
    wax = jnp.concatenate([_block_diag(w_a), _block_diag(w_x)], axis=1).astype(BF16)
    row = lambda t: t.reshape(1, -1).astype(F32)
    const = lambda r, w: pl.BlockSpec((r, w), lambda b, c: (0, 0))
    return pl.pallas_call(
        _lru_body,
        grid=(bsz, nt),
        in_specs=[pl.BlockSpec((tt, c_cols), lambda b, c: (b * nt + c, 0)),
                  const(dc, 2 * dc), const(1, dc), const(1, dc), const(1, dc)],
        out_specs=pl.BlockSpec((tt, dc), lambda b, c: (b * nt + c, 0)),
        out_shape=jax.ShapeDtypeStruct((n, dc), F32),
        scratch_shapes=[pltpu.VMEM((1, dc), F32)],
        compiler_params=_cparams("parallel", "arbitrary"),
        name="rglru",
    )(p_c, wax, row(b_a), row(b_x), row(lam))


def _outproj_body(x_ref, ma_ref, mb_ref, mc_ref, wo_ref, g_ref, wrh_ref, wrl_ref, br_ref,
                  xo_ref, h_ref, ri_ref, rf_ref, rank_ref, cnt_ref, carry):
    da = ma_ref.shape[-1]
    db = mb_ref.shape[-1]
    tm = x_ref.shape[0]

    @pl.when(pl.program_id(0) == 0)
    def _():
        carry[...] = jnp.zeros_like(carry)

    mix = (jnp.dot(ma_ref[...].astype(BF16), wo_ref[0:da, :], preferred_element_type=F32)
           + jnp.dot(mb_ref[...].astype(BF16), wo_ref[da:da + db, :], preferred_element_type=F32)
           + jnp.dot(mc_ref[...].astype(BF16), wo_ref[da + db:, :], preferred_element_type=F32))
    x = x_ref[...] + mix
    xo_ref[...] = x
    h = x * lax.rsqrt(jnp.mean(x * x, axis=-1, keepdims=True) + NORM_EPS) * g_ref[...]
    _store_token_tiles(h_ref, h)

    h_hi = h.astype(BF16)
    h_lo = (h - h_hi.astype(F32)).astype(BF16)
    logits = (jnp.dot(h_hi, wrh_ref[...], preferred_element_type=F32)
              + jnp.dot(h_lo, wrh_ref[...], preferred_element_type=F32)
              + jnp.dot(h_hi, wrl_ref[...], preferred_element_type=F32)) + br_ref[...]
    lane = _iota((tm, LANES), 1)
    far = 4 * LANES
    gmask = lane < MOE_GROUPS
    gl = jnp.where(gmask, logits, NEG_BIG)
    gmax = jnp.max(gl, axis=-1, keepdims=True)
    gidx = jnp.min(jnp.where(gl == gmax, lane, far), axis=-1, keepdims=True)
    gsum = jnp.sum(jnp.where(gmask, jnp.exp(gl - gmax), 0.0), axis=-1, keepdims=True)
    grp_w = 1.0 / gsum
    eidx = lane - MOE_GROUPS
    emask = (eidx >= 0) & (eidx < MOE_EXPERTS) & ((eidx // MOE_PER_GROUP) == gidx)
    el = jnp.where(emask, logits, NEG_BIG)
    v1 = jnp.max(el, axis=-1, keepdims=True)
    i1 = jnp.min(jnp.where(el == v1, lane, far), axis=-1, keepdims=True)
    el2 = jnp.where(lane == i1, NEG_BIG, el)
    v2 = jnp.max(el2, axis=-1, keepdims=True)
    i2 = jnp.min(jnp.where(el2 == v2, lane, far), axis=-1, keepdims=True)
    pr = jnp.exp(v2 - v1)
    g0 = grp_w / (1.0 + pr)
    g1 = g0 * pr
    ri_ref[...] = jnp.where(lane == 0, i1 - MOE_GROUPS, jnp.where(lane == 1, i2 - MOE_GROUPS, 0))
    rf_ref[...] = jnp.where(lane == 0, g0, jnp.where(lane == 1, g1, 0.0))

    oh0 = lane == (i1 - MOE_GROUPS)
    oh1 = lane == (i2 - MOE_GROUPS)
    oh = jnp.where(oh0 | oh1, 1.0, 0.0)
    earlier = jnp.where(_iota((tm, tm), 1) < _iota((tm, tm), 0), 1.0, 0.0).astype(BF16)
    seen = jnp.dot(earlier, oh.astype(BF16), preferred_element_type=F32) + carry[...]
    r0 = jnp.sum(jnp.where(oh0, seen, 0.0), axis=-1, keepdims=True)
    r1 = jnp.sum(jnp.where(oh1, seen, 0.0), axis=-1, keepdims=True)
    rank_ref[...] = jnp.where(lane == 0, r0, jnp.where(lane == 1, r1, 0.0)).astype(jnp.int32)
    carry[...] = carry[...] + jnp.sum(oh, axis=0, keepdims=True)
    cnt_ref[...] = carry[...].astype(jnp.int32)


def _out_proj_router(xt, o_a, o_b, o_c, w_out, gain, w_group, b_group, w_expert, b_expert, tm=512):
    n, d = xt.shape
    tm = min(tm, n)
    ng, ne = w_group.shape[1], w_expert.shape[1]
    wr = jnp.zeros((d, LANES), F32).at[:, 0:ng].set(w_group).at[:, ng:ng + ne].set(w_expert)
    wr_hi = wr.astype(BF16)
    wr_lo = (wr - wr_hi.astype(F32)).astype(BF16)
    br = jnp.zeros((1, LANES), F32).at[0, 0:ng].set(b_group).at[0, ng:ng + ne].set(b_expert)
    rows = lambda w: pl.BlockSpec((tm, w), lambda i: (i, 0))
    const = lambda r, w: pl.BlockSpec((r, w), lambda i: (0, 0))
    return pl.pallas_call(
        _outproj_body,
        grid=(n // tm,),
        in_specs=[rows(d), rows(o_a.shape[1]), rows(o_b.shape[1]), rows(o_c.shape[1]),
                  const(d, d), const(1, d), const(d, LANES), const(d, LANES), const(1, LANES)],
        out_specs=[rows(d), pl.BlockSpec((tm * d // LANES, LANES), lambda i: (i, 0)), rows(LANES), rows(LANES),
                   rows(LANES), const(1, LANES)],
        out_shape=[jax.ShapeDtypeStruct((n, d), F32), jax.ShapeDtypeStruct((n * d // LANES, LANES), F32),
                   jax.ShapeDtypeStruct((n, LANES), jnp.int32), jax.ShapeDtypeStruct((n, LANES), F32),
                   jax.ShapeDtypeStruct((n, LANES), jnp.int32), jax.ShapeDtypeStruct((1, LANES), jnp.int32)],
        scratch_shapes=[pltpu.VMEM((1, LANES), F32)],
        compiler_params=_cparams("arbitrary"),
        name="out_proj_router",
    )(xt, o_a, o_b, o_c, w_out.astype(BF16), gain.reshape(1, d), wr_hi, wr_lo, br)


def _padded_starts(cnt_row):
    cnt = jnp.broadcast_to(cnt_row, (8, LANES))
    padded = (((cnt + (MOE_BLOCK - 1)) // MOE_BLOCK) * MOE_BLOCK).astype(F32)
    before = jnp.where(_iota((LANES, LANES), 0) < _iota((LANES, LANES), 1), 1.0, 0.0).astype(F32)
    start = jnp.dot(padded, before, precision=HIGHEST, preferred_element_type=F32)
    return start, start + padded


def _dest_body(ri_ref, rank_ref, cnt_ref, d0_ref, d1_ref):
    tr = ri_ref.shape[0]
    nr = tr // LANES
    start, _ = _padded_starts(cnt_ref[...])
    start = start[0:1, :]
    ri = ri_ref[...]
    rank = rank_ref[...].astype(F32)
    lane = _iota((tr, LANES), 1)
    own_lane = (_iota((tr, LANES), 0) % LANES) == lane
    fold = jnp.where(_iota((nr, tr), 1) // LANES == _iota((nr, tr), 0), 1.0, 0.0).astype(F32)
    for slot, d_ref in enumerate((d0_ref, d1_ref)):
        begin = jnp.sum(jnp.where(lane == ri[:, slot:slot + 1], start, 0.0), axis=-1, keepdims=True)
        dest = begin + rank[:, slot:slot + 1]
        dense = jnp.dot(fold, jnp.where(own_lane, dest, 0.0), precision=HIGHEST, preferred_element_type=F32)
        d_ref[...] = dense.astype(jnp.int32)


def _destinations(ri, rank, cnt, tr=2048):
    n = ri.shape[0]
    tr = min(tr, n)
    rows = pl.BlockSpec((tr, LANES), lambda i: (i, 0))
    dense = pl.BlockSpec((tr // LANES, LANES), lambda i: (i, 0))
    return pl.pallas_call(
        _dest_body,
        grid=(n // tr,),
        in_specs=[rows, rows, pl.BlockSpec((1, LANES), lambda i: (0, 0))],
        out_specs=[dense, dense],
        out_shape=[jax.ShapeDtypeStruct((n // LANES, LANES), jnp.int32)] * 2,
        compiler_params=_cparams("parallel"),
        name="moe_destinations",
    )(ri, rank, cnt)


def _blkexp_body(cnt_ref, be_ref):
    nbp = be_ref.shape[0]
    _, end = _padded_starts(cnt_ref[...])
    end = end[0:1, :]
    lane = _iota((nbp, LANES), 1)
    row_start = (_iota((nbp, LANES), 0) * MOE_BLOCK).astype(F32)
    done = jnp.where((lane < MOE_EXPERTS) & (end <= row_start), 1.0, 0.0)
    be = jnp.minimum(jnp.sum(done, axis=-1, keepdims=True), MOE_EXPERTS - 1.0)
    be_ref[...] = jnp.broadcast_to(be, (nbp, LANES)).astype(jnp.int32)


def _block_experts(cnt, n_blocks):
    nbp = -(-n_blocks // 8) * 8
    out = pl.pallas_call(
        _blkexp_body,
        grid=(1,),
        in_specs=[pl.BlockSpec((1, LANES), lambda i: (0, 0))],
        out_specs=pl.BlockSpec((nbp, LANES), lambda i: (0, 0)),
        out_shape=jax.ShapeDtypeStruct((nbp, LANES), jnp.int32),
        name="moe_block_experts",
    )(cnt)
    return out[:n_blocks, 0]


def _for_each_row(n_rows, fn):
    def lane_row(i, carry):
        for j in range(LANES):
            fn(i * LANES + j, i, j)
        return carry

    lax.fori_loop(0, n_rows // LANES, lane_row, 0)


def _token_tile(ref, row, per):
    return ref.at[pl.ds(pl.multiple_of(row * per, per), per)]


def _scatter_body(per, d0_ref, d1_ref, cnt_ref, h_ref, xb_hbm, zbuf, sem, zsem):
    ts = h_ref.shape[0] // per
    n_rows = xb_hbm.shape[0] // per

    @pl.when(pl.program_id(0) == 0)
    def _():
        zbuf[...] = jnp.zeros_like(zbuf)

        zrows = zbuf.shape[0] // per

        def zero_copy(row, k):
            dst = xb_hbm.at[pl.ds(pl.multiple_of(row * per, per), k * per)]
            return pltpu.make_async_copy(zbuf.at[pl.ds(0, k * per)], dst, zsem)

        def zero_rows(lo, hi):
            n_full = lax.shift_right_logical(hi - lo, zrows.bit_length() - 1)
            rest_lo = lo + n_full * zrows
            rest = hi - rest_lo

            def each_copy(fn):
                def full(i, carry):
                    fn(zero_copy(lo + i * zrows, zrows))
                    return carry

                lax.fori_loop(0, n_full, full, 0)
                k = zrows // 2
                while k >= 1:
                    above = lax.bitwise_and(rest, -2 * k)

                    @pl.when(lax.bitwise_and(rest, k) != 0)
                    def _(k=k, above=above):
                        fn(zero_copy(rest_lo + above, k))
                    k //= 2

            each_copy(lambda cp: cp.start())
            each_copy(lambda cp: cp.wait())

        def expert(e, slab_start):
            used = cnt_ref[0, e]
            padded = ((used + (MOE_BLOCK - 1)) // MOE_BLOCK) * MOE_BLOCK
            zero_rows(slab_start + used, slab_start + padded)
            return slab_start + padded

        zero_rows(lax.fori_loop(0, MOE_EXPERTS, expert, 0), n_rows)

    def copies(t, i, j):
        src = _token_tile(h_ref, t, per)
        return (pltpu.make_async_copy(src, _token_tile(xb_hbm, d0_ref[i, j], per), sem),
                pltpu.make_async_copy(src, _token_tile(xb_hbm, d1_ref[i, j], per), sem))

    def start(t, i, j):
        for prio, cp in enumerate(copies(t, i, j)):
            cp.start(priority=prio)

    def wait(t, i, j):
        for cp in copies(t, i, j):
            cp.wait()

    _for_each_row(ts, start)
    _for_each_row(ts, wait)


def _scatter_rows(h2, dest0, dest1, cnt, n, n_rows, ts):
    per = h2.shape[0] // n
    smem = pl.BlockSpec((ts // LANES, LANES), lambda i: (i, 0), memory_space=pltpu.SMEM)
    return pl.pallas_call(
        functools.partial(_scatter_body, per),
        grid=(n // ts,),
        in_specs=[smem, smem, pl.BlockSpec((1, LANES), lambda i: (0, 0), memory_space=pltpu.SMEM),
                  pl.BlockSpec((ts * per, LANES), lambda i: (i, 0))],
        out_specs=pl.BlockSpec(memory_space=pl.ANY),
        out_shape=jax.ShapeDtypeStruct((n_rows * per, LANES), F32),
        scratch_shapes=[pltpu.VMEM((ZERO_ROWS * per, LANES), F32), pltpu.SemaphoreType.DMA(()),
                        pltpu.SemaphoreType.DMA(())],
        compiler_params=_cparams("arbitrary"),
        name="moe_scatter",
    )(dest0, dest1, cnt, h2)


def _expert_body(be_ref, x_ref, wg_ref, wu_ref, wd_ref, y_ref):
    del be_ref
    x = _load_token_tiles(x_ref, MOE_BLOCK)
    hid = _silu(_bdot(x, wg_ref[0, 0])) * _bdot(x, wu_ref[0, 0])
    _store_token_tiles(y_ref, _bdot(hid, wd_ref[0, 0]))


def _expert_mlp(xb, blk_expert, w_gate, w_up, w_down, layer):
    de, d = w_down.shape[2:]
    per = d // LANES
    nb = xb.shape[0] // (MOE_BLOCK * per)
    tiles = pl.BlockSpec((MOE_BLOCK * per, LANES), lambda i, be: (i, 0))
    grid_spec = pltpu.PrefetchScalarGridSpec(
        num_scalar_prefetch=1,
        grid=(nb,),
        in_specs=[tiles,
                  pl.BlockSpec((1, 1, d, de), lambda i, be: (layer, be[i], 0, 0)),
                  pl.BlockSpec((1, 1, d, de), lambda i, be: (layer, be[i], 0, 0)),
                  pl.BlockSpec((1, 1, de, d), lambda i, be: (layer, be[i], 0, 0))],
        out_specs=tiles,
    )
    return pl.pallas_call(
        _expert_body,
        grid_spec=grid_spec,
        out_shape=jax.ShapeDtypeStruct(xb.shape, F32),
        compiler_params=_cparams("arbitrary"),
        name="moe_experts",
    )(blk_expert, xb, w_gate, w_up, w_down)


def _combine_body(final_norm, d0_ref, d1_ref, d0_next, d1_next, rf_ref, x_ref, g_ref, yb_hbm, o_ref, gbuf, sems):
    ts, d = x_ref.shape
    per = d // LANES
    step = pl.program_id(0)
    slot = step % 2

    def gather(dest_refs, buf_slot):
        def copies(t, i, j):
            return [pltpu.make_async_copy(_token_tile(yb_hbm, dref[i, j], per),
                                          _token_tile(gbuf.at[buf_slot, k], t, per), sems.at[buf_slot])
                    for k, dref in enumerate(dest_refs)]
        return copies

    def start_all(copies):
        def start(t, i, j):
            for prio, cp in enumerate(copies(t, i, j)):
                cp.start(priority=prio)
        _for_each_row(ts, start)

    @pl.when(step == 0)
    def _():
        start_all(gather((d0_ref, d1_ref), slot))

    @pl.when(step + 1 < pl.num_programs(0))
    def _():
        start_all(gather((d0_next, d1_next), 1 - slot))

    current = gather((d0_ref, d1_ref), slot)

    def wait(t, i, j):
        for cp in current(t, i, j):
            cp.wait()

    _for_each_row(ts, wait)
    rf = rf_ref[...]
    x = x_ref[...] + (_load_token_tiles(gbuf.at[slot, 0], ts) * rf[:, 0:1]
                      + _load_token_tiles(gbuf.at[slot, 1], ts) * rf[:, 1:2])
    if final_norm:
        x = x * lax.rsqrt(jnp.mean(x * x, axis=-1, keepdims=True) + NORM_EPS) * g_ref[...]
    o_ref[...] = x


def _combine(x_mid, yb, dest0, dest1, rf, final_gain, ts):
    n, d = x_mid.shape
    final_norm = final_gain is not None
    gain = (final_gain if final_norm else jnp.ones((d,), F32)).reshape(1, d)
    last = n // ts - 1
    smem = pl.BlockSpec((ts // LANES, LANES), lambda i: (i, 0), memory_space=pltpu.SMEM)
    smem_next = pl.BlockSpec((ts // LANES, LANES), lambda i: (jnp.minimum(i + 1, last), 0), memory_space=pltpu.SMEM)
    rows = lambda w: pl.BlockSpec((ts, w), lambda i: (i, 0))
    return pl.pallas_call(
        functools.partial(_combine_body, final_norm),
        grid=(n // ts,),
        in_specs=[smem, smem, smem_next, smem_next, rows(LANES), rows(d), pl.BlockSpec((1, d), lambda i: (0, 0)),
                  pl.BlockSpec(memory_space=pl.ANY)],
        out_specs=rows(d),
        out_shape=jax.ShapeDtypeStruct((n, d), F32),
        scratch_shapes=[pltpu.VMEM((2, 2, ts * d // LANES, LANES), F32), pltpu.SemaphoreType.DMA((2,))],
        compiler_params=_cparams("arbitrary"),
        name="moe_combine",
    )(dest0, dest1, dest0, dest1, rf, x_mid, gain, yb)


def _moe(x_mid, h2, ri, rf, rank, cnt, w_gate, w_up, w_down, layer, final_gain):
    n, d = x_mid.shape
    n_assign = 2 * n
    n_blocks = (n_assign + MOE_EXPERTS * (MOE_BLOCK - 1) + MOE_BLOCK - 1) // MOE_BLOCK
    ts = min(1024, n)
    dest0, dest1 = _destinations(ri, rank, cnt)
    blk_expert = _block_experts(cnt, n_blocks)
    xb = _scatter_rows(h2, dest0, dest1, cnt, n, n_blocks * MOE_BLOCK, ts)
    yb = _expert_mlp(xb, blk_expert, w_gate, w_up, w_down, layer)
    return _combine(x_mid, yb, dest0, dest1, rf, final_gain, ts)


def kernel(x, norm_mix, norm_ffn, norm_final, w_in, w_out, rwkv_mu, rwkv_w0, rwkv_w2, rwkv_a0, rwkv_a2, rwkv_g2, rwkv_k_k, rwkv_k_a, rwkv_r_k, rwkv_lnx_w, rwkv_lnx_b, gdn_conv_w, gdn_a_log, gdn_dt_bias, gdn_norm_w, lru_conv_w, lru_conv_b, lru_w_a, lru_b_a, lru_w_x, lru_b_x, lru_lambda, moe_w_group, moe_b_group, moe_w_expert, moe_b_expert, moe_w_gate, moe_w_up, moe_w_down):
    bsz, seq, d = x.shape
    depth = w_in.shape[0]
    n = bsz * seq
    a_cols = rwkv_mu.shape[1]
    db = gdn_norm_w.shape[1] * GDN_HEADS
    dc = lru_conv_b.shape[1]
    n_ba = 2 * GDN_HEADS
    widths = (a_cols, 3 * db, db, 2 * dc, LANES)
    xt = x.reshape(n, d)
    for l in range(depth):
        w = w_in[l]
        b0 = a_cols
        c0 = b0 + 4 * db + n_ba
        w_cat = jnp.concatenate(
            [w[:, 0:a_cols], w[:, b0:b0 + 4 * db], w[:, c0:c0 + 2 * dc],
             w[:, b0 + 4 * db:c0], jnp.zeros((d, LANES - n_ba), F32)], axis=1).astype(BF16)
        p_a, p_qkv, p_z, p_c, p_ba = _in_proj(xt, seq, norm_mix[l], w_cat, widths, rwkv_mu[l], gdn_conv_w[l],
                                              lru_conv_w[l], lru_conv_b[l])
        o_a = _rwkv_group(p_a, bsz, seq, rwkv_w0[l], rwkv_w2[l], rwkv_a0[l], rwkv_a2[l], rwkv_g2[l],
                          rwkv_k_k[l], rwkv_k_a[l], rwkv_r_k[l], rwkv_lnx_w[l], rwkv_lnx_b[l])
        o_b = _gdn_group(p_qkv, p_z, p_ba, bsz, seq, gdn_a_log[l], gdn_dt_bias[l], gdn_norm_w[l])
        o_c = _lru_group(p_c, bsz, seq, lru_w_a[l], lru_b_a[l], lru_w_x[l], lru_b_x[l], lru_lambda[l])
        x_mid, h2, ri, rf, rank, cnt = _out_proj_router(xt, o_a, o_b, o_c, w_out[l], norm_ffn[l], moe_w_group[l],
                                                        moe_b_group[l], moe_w_expert[l], moe_b_expert[l])
        xt = _moe(x_mid, h2, ri, rf, rank, cnt, moe_w_gate, moe_w_up, moe_w_down, l,
                  norm_final if l == depth - 1 else None)
    return xt.reshape(bsz, seq, d)
```

```python
import functools

import jax
import jax.numpy as jnp
from jax import lax
from jax.experimental import pallas as pl
from jax.experimental.pallas import tpu as pltpu

F32 = jnp.float32
BF16 = jnp.bfloat16
HIGHEST = lax.Precision.HIGHEST

NORM_EPS = 1e-6
CONV_WIDTH = 4
CHUNK = 64
SCAN_TILE = 256
SCAN_SEQS = 4
RWKV_HEADS = 4
RWKV_GN_EPS = 64e-5
GDN_HEADS = 4
LRU_C = 8.0
MOE_GROUPS = 4
MOE_PER_GROUP = 8
MOE_EXPERTS = MOE_GROUPS * MOE_PER_GROUP
MOE_BLOCK = 512
ZERO_ROWS = 256
LANES = 128
SUBLANES = 8
NEG_BIG = -1e30
VMEM_LIMIT = 56 * 1024 * 1024


def _cparams(*sem):
    return pltpu.CompilerParams(dimension_semantics=sem, vmem_limit_bytes=VMEM_LIMIT)


def _bdot(a, b):
    return jnp.dot(a.astype(BF16), b.astype(BF16), preferred_element_type=F32)


def _bdot_nt(a, b):
    return lax.dot_general(a.astype(BF16), b.astype(BF16), (((1,), (1,)), ((), ())),
                           preferred_element_type=F32)


def _bdot_tn(a, b):
    return lax.dot_general(a.astype(BF16), b.astype(BF16), (((0,), (0,)), ((), ())),
                           preferred_element_type=F32)


def _exact_dot(a, b01):
    hi = a.astype(BF16)
    lo = (a - hi.astype(F32)).astype(BF16)
    return (jnp.dot(hi, b01, preferred_element_type=F32)
            + jnp.dot(lo, b01, preferred_element_type=F32))


def _exact_dot_left(a01, b):
    hi = b.astype(BF16)
    lo = (b - hi.astype(F32)).astype(BF16)
    return (jnp.dot(a01, hi, preferred_element_type=F32)
            + jnp.dot(a01, lo, preferred_element_type=F32))


def _load_token_tiles(ref, rows):
    per = ref.shape[0] // rows
    return jnp.concatenate([ref[pl.ds(c, rows, stride=per), :] for c in range(per)], axis=1)


def _store_token_tiles(ref, value):
    rows, d = value.shape
    per = d // LANES
    for c in range(per):
        ref[pl.ds(c, rows, stride=per), :] = value[:, c * LANES:(c + 1) * LANES]


def _softplus(x):
    return jnp.maximum(x, 0.0) + jnp.log1p(jnp.exp(-jnp.abs(x)))


def _sigmoid(x):
    return 1.0 / (1.0 + jnp.exp(-x))


def _silu(x):
    return x * _sigmoid(x)


def _iota(shape, dim):
    return lax.broadcasted_iota(jnp.int32, shape, dim)


def _head_ones(width, head_dim):
    r = _iota((width, width), 0) // head_dim
    c = _iota((width, width), 1) // head_dim
    return jnp.where(r == c, 1.0, 0.0).astype(BF16)


def _stack_heads(x, n_heads):
    c, w = x.shape
    hd = w // n_heads
    xb = x.astype(BF16)
    lane_head = _iota((c, w), 1) // hd
    return jnp.concatenate([jnp.where(lane_head == h, xb, jnp.zeros_like(xb)) for h in range(n_heads)],
                           axis=0)


def _tri_masks(c, n_heads):
    r = _iota((c, n_heads * c), 0)
    s = _iota((c, n_heads * c), 1) % c
    return s < r, s <= r


def _unit_lower_inverses(l_mats, n_heads):
    shape = l_mats[0].shape
    c = shape[0]
    eye = jnp.where(_iota(shape, 1) % c == _iota(shape, 0), 1.0, 0.0).astype(F32)
    ps = [eye + l for l in l_mats]
    ms = [_bdot(l, _stack_heads(l, n_heads)) for l in l_mats]
    span = 2
    while span < c:
        last = span * 2 >= c
        nxt = [_bdot(p if last else jnp.concatenate([m, p], axis=0), _stack_heads(m, n_heads))
               for m, p in zip(ms, ps)]
        ps = [p + (r if last else r[c:]) for p, r in zip(ps, nxt)]
        ms = [None if last else r[0:c] for r in nxt]
        span *= 2
    return ps


def _head_block_mask(width, head_dim):
    r = _iota((width, width), 0) // head_dim
    c = _iota((width, width), 1) // head_dim
    return r == c


def _shift_rows(y, tail, j):
    rolled = pltpu.roll(y, j, 0)
    head = jnp.where(_iota(tail.shape, 0) < j, pltpu.roll(tail, j, 0), rolled[0:8])
    return jnp.concatenate([head, rolled[8:]], axis=0)


def _causal_conv(y, tail, cw):
    acc = cw[CONV_WIDTH - 1:CONV_WIDTH, :] * y
    for j in range(1, CONV_WIDTH):
        acc = acc + cw[CONV_WIDTH - 1 - j:CONV_WIDTH - j, :] * _shift_rows(y, tail, j)
    return acc


def _inproj_body(tiles_per_seq, x_ref, g_ref, w_ref, mu_ref, gcw_ref, lcw_ref, lcb_ref,
                 a_ref, qkv_ref, z_ref, c_ref, ba_ref, atail, qtail, ctail):
    tm = x_ref.shape[0]
    tails = (atail, qtail, ctail)

    @pl.when(pl.program_id(0) % tiles_per_seq == 0)
    def _():
        for tail in tails:
            tail[...] = jnp.zeros_like(tail)

    x = x_ref[...]
    h = x * lax.rsqrt(jnp.mean(x * x, axis=-1, keepdims=True) + NORM_EPS) * g_ref[...]
    hb = h.astype(BF16)
    wa, wq, wz, wc = a_ref.shape[-1], qkv_ref.shape[-1], z_ref.shape[-1], c_ref.shape[-1]
    dc = ctail.shape[1]

    def proj(off, width):
        return jnp.dot(hb, w_ref[:, off:off + width], preferred_element_type=F32)

    for s in range(3):
        cols = slice(s * (wq // 3), (s + 1) * (wq // 3))
        y = proj(wa + cols.start, wq // 3)
        qkv_ref[:, cols] = _silu(_causal_conv(y, qtail[:, cols], gcw_ref[:, cols]))
        qtail[:, cols] = y[tm - 8:tm, :]
    p = proj(0, wa)
    a_ref[...] = p + (_shift_rows(p, atail[...], 1) - p) * mu_ref[...]
    atail[...] = p[tm - 8:tm, :]
    z_ref[...] = proj(wa + wq, wz)
    pc = proj(wa + wq + wz, wc)
    c_ref[:, 0:dc] = lcb_ref[...] + _causal_conv(pc[:, 0:dc], ctail[...], lcw_ref[...])
    c_ref[:, dc:] = pc[:, dc:]
    ctail[...] = pc[tm - 8:tm, 0:dc]
    ba_ref[...] = proj(wa + wq + wz + wc, ba_ref.shape[-1])


def _in_proj(xt, seq, gain, w_cat, widths, mu, gdn_conv_w, lru_conv_w, lru_conv_b, tm=512):
    n, d = xt.shape
    ntot = w_cat.shape[1]
    tm = min(tm, seq)
    dc = lru_conv_b.shape[0]
    const = lambda r, w: pl.BlockSpec((r, w), lambda i: (0, 0))
    return pl.pallas_call(
        functools.partial(_inproj_body, seq // tm),
        grid=(n // tm,),
        in_specs=[pl.BlockSpec((tm, d), lambda i: (i, 0)), const(1, d), const(d, ntot),
                  const(1, widths[0]), const(CONV_WIDTH, widths[1]), const(CONV_WIDTH, dc), const(1, dc)],
        out_specs=[pl.BlockSpec((tm, w), lambda i: (i, 0)) for w in widths],
        out_shape=[jax.ShapeDtypeStruct((n, w), F32) for w in widths],
        scratch_shapes=[pltpu.VMEM((8, widths[0]), F32), pltpu.VMEM((8, widths[1]), F32),
                        pltpu.VMEM((8, dc), F32)],
        compiler_params=_cparams("arbitrary"),
        name="in_proj",
    )(xt, gain.reshape(1, d), w_cat, mu.reshape(1, -1), gdn_conv_w, lru_conv_w, lru_conv_b.reshape(1, dc))


def _rwkv_body(xs_ref, w0_ref, a0_ref, kk_ref, ka_ref, rk_ref, lnw_ref, lnb_ref,
               w2_ref, a2_ref, g2_ref, o_ref, state):
    c = CHUNK
    n_seq, tt, da = o_ref.shape
    nh = RWKV_HEADS
    hd = da // nh

    @pl.when(pl.program_id(1) == 0)
    def _():
        state[...] = jnp.zeros_like(state)

    xs = jnp.concatenate([xs_ref[s] for s in range(n_seq)], axis=0)
    r = xs[:, 0:da]
    k = xs[:, da:2 * da]
    v = xs[:, 2 * da:3 * da]
    lo = xs[:, 3 * da:]

    w_log = -_softplus(-(w0_ref[...] + _bdot(jnp.tanh(lo), w2_ref[...]))) - 0.5
    lw = -jnp.exp(w_log)
    a = _sigmoid(a0_ref[...] + _bdot(lo, a2_ref[...]))
    g = _bdot(_sigmoid(lo), g2_ref[...])

    ones_h = _head_ones(da, hd)
    kkr = k * kk_ref[...]
    kk = kkr * lax.rsqrt(_exact_dot(kkr * kkr, ones_h) + 1e-6)
    k2 = k * (1.0 + (a - 1.0) * ka_ref[...])
    alpha = -(kk * a)

    rr = _iota((tt, tt), 0)
    cc = _iota((tt, tt), 1)
    tri = jnp.where((cc <= rr) & (cc // c == rr // c), 1.0, 0.0).astype(BF16)
    cum = jnp.concatenate([_exact_dot_left(tri, lw[s * tt:(s + 1) * tt]) for s in range(n_seq)], axis=0)
    per_seq = tt // c
    n_chunks = n_seq * per_seq
    chunk_rows = [slice(g * c, (g + 1) * c) for g in range(n_chunks)]
    cum_ends = [cum[(g + 1) * c - 1:(g + 1) * c, :] for g in range(n_chunks)]
    cum_last = jnp.concatenate([jnp.broadcast_to(ce, (c, da)) for ce in cum_ends], axis=0)
    e_pos = jnp.exp(cum)
    e_neg = jnp.exp(-cum)
    e_tail = jnp.exp(cum_last - cum)
    r_t = r * e_pos
    b_t = kk * jnp.exp(cum - lw)
    a_t = alpha * e_neg
    k_t = k2 * e_neg
    a_end = alpha * e_tail
    k_end = k2 * e_tail

    nl = nh * c
    strict, incl = _tri_masks(c, nh)
    grams = [_bdot_nt(jnp.concatenate([b_t[rows], r_t[rows]], axis=0),
                      jnp.concatenate([_stack_heads(a_t[rows], nh), _stack_heads(k_t[rows], nh)], axis=0))
             for rows in chunk_rows]
    t_invs = _unit_lower_inverses([jnp.where(strict, gm[0:c, 0:nl], 0.0) for gm in grams], nh)
    vss = [_stack_heads(v[rows], nh) for rows in chunk_rows]
    lkvs = [_bdot(jnp.where(strict, gm[0:c, nl:], 0.0), vs) for gm, vs in zip(grams, vss)]
    wus = [_bdot(t_inv, jnp.concatenate([_stack_heads(b_t[rows], nh), _stack_heads(lkv, nh)], axis=1))
           for t_inv, lkv, rows in zip(t_invs, lkvs, chunk_rows)]
    s_as = [jnp.where(incl, gm[c:, 0:nl], 0.0) for gm in grams]
    s_ks = [jnp.where(incl, gm[c:, nl:], 0.0) for gm in grams]
    block_diag = _head_block_mask(da, hd)
    w_mats = [wu[:, 0:da] for wu in wus]
    u0s = [wu[:, da:] for wu in wus]
    q_effs = [r_t[rows] + _bdot(s_a, _stack_heads(w_mat, nh)) for rows, s_a, w_mat in zip(chunk_rows, s_as, w_mats)]
    o_consts = [_bdot(jnp.concatenate([s_a, s_k], axis=1), jnp.concatenate([_stack_heads(u0, nh), vs], axis=0))
                for s_a, s_k, u0, vs in zip(s_as, s_ks, u0s, vss)]
    s_lins = [jnp.where(block_diag, _bdot_tn(w_mat, a_end[rows]), 0.0) for w_mat, rows in zip(w_mats, chunk_rows)]
    s_consts = [jnp.where(block_diag, _bdot_tn(jnp.concatenate([u0, v[rows]], axis=0),
                                               jnp.concatenate([a_end[rows], k_end[rows]], axis=0)), 0.0)
                for u0, rows in zip(u0s, chunk_rows)]

    s_curs = [state[s] for s in range(n_seq)]
    outs = [None] * n_chunks
    for step in range(per_seq):
        for s in range(n_seq):
            i = s * per_seq + step
            outs[i] = _bdot_nt(q_effs[i], s_curs[s]) + o_consts[i]
            s_curs[s] = s_curs[s] * jnp.exp(cum_ends[i]) + _bdot(s_curs[s], s_lins[i]) + s_consts[i]
    for s in range(n_seq):
        state[s] = s_curs[s]
    o = jnp.concatenate(outs, axis=0)

    inv_hd = 1.0 / hd
    mean = _exact_dot(o, ones_h) * inv_hd
    cen = o - mean
    var = _exact_dot(cen * cen, ones_h) * inv_hd
    o_n = cen * lax.rsqrt(var + RWKV_GN_EPS) * lnw_ref[...] + lnb_ref[...]
    bonus = _exact_dot(r * k2 * rk_ref[...], ones_h) * v
    res = (o_n + bonus) * g
    for s in range(n_seq):
        o_ref[s] = res[s * tt:(s + 1) * tt]


def _rwkv_group(p_a, bsz, seq, w0, w2, a0, a2, g2, k_k, k_a, r_k, lnx_w, lnx_b, tt=SCAN_TILE):
    n, a_cols = p_a.shape
    da = w0.shape[-1]
    n_lora = a_cols - 3 * da
    tt = min(tt, seq)
    nchunk = seq // tt
    d_dec, d_aaa = w2.shape[0], a2.shape[0]
    w2p = jnp.zeros((n_lora, da), F32).at[0:d_dec].set(w2).astype(BF16)
    a2p = jnp.zeros((n_lora, da), F32).at[d_dec:d_dec + d_aaa].set(a2).astype(BF16)
    g2p = jnp.zeros((n_lora, da), F32).at[d_dec + d_aaa:].set(g2).astype(BF16)
    row = lambda t: t.reshape(1, -1).astype(F32)
    vec = lambda w: pl.BlockSpec((1, w), lambda b, c: (0, 0))
    mat = pl.BlockSpec((n_lora, da), lambda b, c: (0, 0))
    nb = min(SCAN_SEQS, bsz)
    seqs = lambda w: pl.BlockSpec((nb, tt, w), lambda b, c: (b, c, 0))
    out = pl.pallas_call(
        _rwkv_body,
        grid=(bsz // nb, nchunk),
        in_specs=[seqs(a_cols)] + [vec(da)] * 7 + [mat] * 3,
        out_specs=seqs(da),
        out_shape=jax.ShapeDtypeStruct((bsz, seq, da), F32),
        scratch_shapes=[pltpu.VMEM((nb, da, da), F32)],
        compiler_params=_cparams("parallel", "arbitrary"),
        name="rwkv7",
    )(p_a.reshape(bsz, seq, a_cols), row(w0), row(a0), row(k_k), row(k_a), row(r_k), row(lnx_w),
      row(lnx_b), w2p, a2p, g2p)
    return out.reshape(n, da)


def _gdn_body(qkv_ref, z_ref, ba_ref, alog_ref, dtb_ref, nw_ref, o_ref, state):
    c = CHUNK
    n_seq, tt, db = o_ref.shape
    rows_all = n_seq * tt
    nh = GDN_HEADS
    hd = db // nh
    nl = nh * c

    @pl.when(pl.program_id(1) == 0)
    def _():
        state[...] = jnp.zeros_like(state)

    def all_rows(ref, cols=slice(None)):
        return jnp.concatenate([ref[s, :, cols] for s in range(n_seq)], axis=0)

    q = all_rows(qkv_ref, slice(0, db))
    k = all_rows(qkv_ref, slice(db, 2 * db))
    v = all_rows(qkv_ref, slice(2 * db, 3 * db))

    def per_head(col_of_head):
        return jnp.concatenate([jnp.broadcast_to(col_of_head(h), (rows_all, hd)) for h in range(nh)], axis=1)

    def head_sumsq(x):
        return per_head(lambda h: jnp.sum(x[:, h * hd:(h + 1) * hd] * x[:, h * hd:(h + 1) * hd],
                                          axis=-1, keepdims=True))

    q = q * lax.rsqrt(head_sumsq(q) + 1e-6) * (hd ** -0.5)
    k = k * lax.rsqrt(head_sumsq(k) + 1e-6)

    ba = all_rows(ba_ref)
    beta_l = _sigmoid(ba)
    g_l = -jnp.exp(alog_ref[...]) * _softplus(ba + dtb_ref[...])
    row_in_chunk = _iota((rows_all, LANES), 0) % c
    gcum_l = g_l
    d = 1
    while d < c:
        gcum_l = gcum_l + jnp.where(row_in_chunk >= d, pltpu.roll(gcum_l, d, 0), 0.0)
        d *= 2
    per_seq = tt // c
    n_chunks = n_seq * per_seq
    chunk_rows = [slice(g * c, (g + 1) * c) for g in range(n_chunks)]
    g_last_l = jnp.concatenate([jnp.broadcast_to(gcum_l[(g + 1) * c - 1:(g + 1) * c, :], (c, LANES))
                                for g in range(n_chunks)], axis=0)
    e_g_l = jnp.exp(gcum_l)
    e_tail_l = jnp.exp(g_last_l - gcum_l)
    beta = per_head(lambda h: beta_l[:, h:h + 1])
    e_g = per_head(lambda h: e_g_l[:, nh + h:nh + h + 1])
    e_tail = per_head(lambda h: e_tail_l[:, nh + h:nh + h + 1])
    kb = k * beta
    vb = v * beta
    kbe = kb * e_g
    qe = q * e_g
    k_end = k * e_tail

    strict, incl = _tri_masks(c, nh)
    lane_head = _iota((c, nl), 1) // c
    on_diag = (_iota((c, nl), 1) % c) == _iota((c, nl), 0)
    dmats = []
    for rows in chunk_rows:
        gc = gcum_l[rows]
        gcol = jnp.broadcast_to(gc[:, nh:nh + 1], (c, nl))
        for h in range(1, nh):
            gcol = jnp.where(lane_head == h, jnp.broadcast_to(gc[:, nh + h:nh + h + 1], (c, nl)), gcol)
        grow = jnp.sum(jnp.where(on_diag, gcol, 0.0), axis=0, keepdims=True)
        dmats.append(jnp.exp(jnp.where(incl, gcol - grow, NEG_BIG)))
    grams = [_bdot_nt(jnp.concatenate([kb[rows], q[rows]], axis=0), _stack_heads(k[rows], nh))
             for rows in chunk_rows]
    intras = [jnp.where(incl, gm[c:] * dm, 0.0) for gm, dm in zip(grams, dmats)]
    t_invs = _unit_lower_inverses([jnp.where(strict, -(gm[0:c] * dm), 0.0) for gm, dm in zip(grams, dmats)], nh)
    sols = [_bdot(t_inv, jnp.concatenate([_stack_heads(vb[rows], nh), _stack_heads(kbe[rows], nh)], axis=1))
            for t_inv, rows in zip(t_invs, chunk_rows)]

    s_curs = [state[s] for s in range(n_seq)]
    block_diag = _head_block_mask(db, hd)
    outs = [None] * n_chunks
    for g in range(per_seq):
        for s in range(n_seq):
            i = s * per_seq + g
            rows = chunk_rows[i]
            u_s, w_s = sols[i][:, 0:db], sols[i][:, db:]
            wq = _bdot(jnp.concatenate([w_s, qe[rows]], axis=0), s_curs[s])
            v_new = u_s - wq[0:c]
            outs[i] = wq[c:] + _bdot(intras[i], _stack_heads(v_new, nh))
            upd = _bdot_tn(k_end[rows], v_new)
            s_curs[s] = s_curs[s] * e_g[(i + 1) * c - 1:(i + 1) * c, :] + jnp.where(block_diag, upd, 0.0)
    for s in range(n_seq):
        state[s] = s_curs[s]
    o = jnp.concatenate(outs, axis=0)
    o = o * lax.rsqrt(head_sumsq(o) * (1.0 / hd) + NORM_EPS) * nw_ref[...]
    res = o * _silu(all_rows(z_ref))
    for s in range(n_seq):
        o_ref[s] = res[s * tt:(s + 1) * tt]


def _gdn_group(qkv, z, ba, bsz, seq, a_log, dt_bias, norm_w, tt=SCAN_TILE):
    n, w3 = qkv.shape
    db = z.shape[-1]
    nh = GDN_HEADS
    tt = min(tt, seq)
    nchunk = seq // tt
    alog_row = jnp.zeros((1, LANES), F32).at[0, nh:2 * nh].set(a_log)
    dtb_row = jnp.zeros((1, LANES), F32).at[0, nh:2 * nh].set(dt_bias)
    nw_row = jnp.tile(norm_w, nh).reshape(1, db)
    nb = min(SCAN_SEQS, bsz)
    seqs = lambda w: pl.BlockSpec((nb, tt, w), lambda b, c: (b, c, 0))
    const = lambda r, w: pl.BlockSpec((r, w), lambda b, c: (0, 0))
    out = pl.pallas_call(
        _gdn_body,
        grid=(bsz // nb, nchunk),
        in_specs=[seqs(w3), seqs(db), seqs(LANES), const(1, LANES), const(1, LANES), const(1, db)],
        out_specs=seqs(db),
        out_shape=jax.ShapeDtypeStruct((bsz, seq, db), F32),
        scratch_shapes=[pltpu.VMEM((nb, db, db), F32)],
        compiler_params=_cparams("parallel", "arbitrary"),
        name="gdn",
    )(qkv.reshape(bsz, seq, w3), z.reshape(bsz, seq, db), ba.reshape(bsz, seq, LANES), alog_row, dtb_row, nw_row)
    return out.reshape(n, db)


def _lru_body(p_ref, wax_ref, ba_ref, bx_ref, lam_ref, o_ref, hcar):
    tt = p_ref.shape[0]
    dc = o_ref.shape[-1]

    @pl.when(pl.program_id(1) == 0)
    def _():
        hcar[...] = jnp.zeros_like(hcar)

    conv = p_ref[:, 0:dc]
    gate = p_ref[:, dc:]

    ri = _bdot(conv, wax_ref[...])
    r = _sigmoid(ri[:, 0:dc] + ba_ref[...])
    i = _sigmoid(ri[:, dc:] + bx_ref[...])
    log_a = -LRU_C * r * _softplus(-lam_ref[...])
    a = jnp.exp(log_a)
    mult = jnp.sqrt(jnp.tanh(-log_a) * (a * a + 1.0))
    row = _iota((tt, dc), 0)
    mult = jnp.where((row == 0) & (pl.program_id(1) == 0), 1.0, mult)
    bv = conv * i * mult

    av = a
    row_in_group = row % SUBLANES
    d = 1
    while d < SUBLANES:
        a_sh = jnp.where(row_in_group >= d, pltpu.roll(av, d, 0), 1.0)
        b_sh = jnp.where(row_in_group >= d, pltpu.roll(bv, d, 0), 0.0)
        bv = av * b_sh + bv
        av = av * a_sh
        d *= 2
    carry = hcar[...]
    groups = []
    for i in range(tt // SUBLANES):
        rows = slice(i * SUBLANES, (i + 1) * SUBLANES)
        hg = av[rows] * carry + bv[rows]
        groups.append(hg)
        carry = hg[SUBLANES - 1:SUBLANES, :]
    h = jnp.concatenate(groups, axis=0)
    hcar[...] = carry
    gelu = 0.5 * gate * (1.0 + jnp.tanh(0.7978845608028654 * (gate + 0.044715 * gate * gate * gate)))
    o_ref[...] = h * gelu


def _block_diag(w):
    nb, bi, bo = w.shape
    out = jnp.zeros((nb * bi, nb * bo), w.dtype)
    for b in range(nb):
        out = out.at[b * bi:(b + 1) * bi, b * bo:(b + 1) * bo].set(w[b])
    return out


def _lru_group(p_c, bsz, seq, w_a, b_a, w_x, b_x, lam, tt=256):
    n, c_cols = p_c.shape
    dc = c_cols // 2
    tt = min(tt, seq)
    nt = seq // tt
    wax = jnp.concatenate([_block_diag(w_a), _block_diag(w_x)], axis=1).astype(BF16)
    row = lambda t: t.reshape(1, -1).astype(F32)
    const = lambda r, w: pl.BlockSpec((r, w), lambda b, c: (0, 0))
    return pl.pallas_call(
        _lru_body,
        grid=(bsz, nt),
        in_specs=[pl.BlockSpec((tt, c_cols), lambda b, c: (b * nt + c, 0)),
                  const(dc, 2 * dc), const(1, dc), const(1, dc), const(1, dc)],
        out_specs=pl.BlockSpec((tt, dc), lambda b, c: (b * nt + c, 0)),
        out_shape=jax.ShapeDtypeStruct((n, dc), F32),
        scratch_shapes=[pltpu.VMEM((1, dc), F32)],
        compiler_params=_cparams("parallel", "arbitrary"),
        name="rglru",
    )(p_c, wax, row(b_a), row(b_x), row(lam))


def _outproj_body(x_ref, ma_ref, mb_ref, mc_ref, wo_ref, g_ref, wrh_ref, wrl_ref, br_ref,
                  xo_ref, h_ref, ri_ref, rf_ref, rank_ref, cnt_ref, carry):
    da = ma_ref.shape[-1]
    db = mb_ref.shape[-1]
    tm = x_ref.shape[0]

    @pl.when(pl.program_id(0) == 0)
    def _():
        carry[...] = jnp.zeros_like(carry)

    mix = (jnp.dot(ma_ref[...].astype(BF16), wo_ref[0:da, :], preferred_element_type=F32)
           + jnp.dot(mb_ref[...].astype(BF16), wo_ref[da:da + db, :], preferred_element_type=F32)
           + jnp.dot(mc_ref[...].astype(BF16), wo_ref[da + db:, :], preferred_element_type=F32))
    x = x_ref[...] + mix
    xo_ref[...] = x
    h = x * lax.rsqrt(jnp.mean(x * x, axis=-1, keepdims=True) + NORM_EPS) * g_ref[...]
    _store_token_tiles(h_ref, h)

    h_hi = h.astype(BF16)
    h_lo = (h - h_hi.astype(F32)).astype(BF16)
    logits = (jnp.dot(h_hi, wrh_ref[...], preferred_element_type=F32)
              + jnp.dot(h_lo, wrh_ref[...], preferred_element_type=F32)
              + jnp.dot(h_hi, wrl_ref[...], preferred_element_type=F32)) + br_ref[...]
    lane = _iota((tm, LANES), 1)
    far = 4 * LANES
    gmask = lane < MOE_GROUPS
    gl = jnp.where(gmask, logits, NEG_BIG)
    gmax = jnp.max(gl, axis=-1, keepdims=True)
    gidx = jnp.min(jnp.where(gl == gmax, lane, far), axis=-1, keepdims=True)
    gsum = jnp.sum(jnp.where(gmask, jnp.exp(gl - gmax), 0.0), axis=-1, keepdims=True)
    grp_w = 1.0 / gsum
    eidx = lane - MOE_GROUPS
    emask = (eidx >= 0) & (eidx < MOE_EXPERTS) & ((eidx // MOE_PER_GROUP) == gidx)
    el = jnp.where(emask, logits, NEG_BIG)
    v1 = jnp.max(el, axis=-1, keepdims=True)
    i1 = jnp.min(jnp.where(el == v1, lane, far), axis=-1, keepdims=True)
    el2 = jnp.where(lane == i1, NEG_BIG, el)
    v2 = jnp.max(el2, axis=-1, keepdims=True)
    i2 = jnp.min(jnp.where(el2 == v2, lane, far), axis=-1, keepdims=True)
    pr = jnp.exp(v2 - v1)
    g0 = grp_w / (1.0 + pr)
    g1 = g0 * pr
    ri_ref[...] = jnp.where(lane == 0, i1 - MOE_GROUPS, jnp.where(lane == 1, i2 - MOE_GROUPS, 0))
    rf_ref[...] = jnp.where(lane == 0, g0, jnp.where(lane == 1, g1, 0.0))

    oh0 = lane == (i1 - MOE_GROUPS)
    oh1 = lane == (i2 - MOE_GROUPS)
    oh = jnp.where(oh0 | oh1, 1.0, 0.0)
    earlier = jnp.where(_iota((tm, tm), 1) < _iota((tm, tm), 0), 1.0, 0.0).astype(BF16)
    seen = jnp.dot(earlier, oh.astype(BF16), preferred_element_type=F32) + carry[...]
    r0 = jnp.sum(jnp.where(oh0, seen, 0.0), axis=-1, keepdims=True)
    r1 = jnp.sum(jnp.where(oh1, seen, 0.0), axis=-1, keepdims=True)
    rank_ref[...] = jnp.where(lane == 0, r0, jnp.where(lane == 1, r1, 0.0)).astype(jnp.int32)
    carry[...] = carry[...] + jnp.sum(oh, axis=0, keepdims=True)
    cnt_ref[...] = carry[...].astype(jnp.int32)


def _out_proj_router(xt, o_a, o_b, o_c, w_out, gain, w_group, b_group, w_expert, b_expert, tm=512):
    n, d = xt.shape
    tm = min(tm, n)
    ng, ne = w_group.shape[1], w_expert.shape[1]
    wr = jnp.zeros((d, LANES), F32).at[:, 0:ng].set(w_group).at[:, ng:ng + ne].set(w_expert)
    wr_hi = wr.astype(BF16)
    wr_lo = (wr - wr_hi.astype(F32)).astype(BF16)
    br = jnp.zeros((1, LANES), F32).at[0, 0:ng].set(b_group).at[0, ng:ng + ne].set(b_expert)
    rows = lambda w: pl.BlockSpec((tm, w), lambda i: (i, 0))
    const = lambda r, w: pl.BlockSpec((r, w), lambda i: (0, 0))
    return pl.pallas_call(
        _outproj_body,
        grid=(n // tm,),
        in_specs=[rows(d), rows(o_a.shape[1]), rows(o_b.shape[1]), rows(o_c.shape[1]),
                  const(d, d), const(1, d), const(d, LANES), const(d, LANES), const(1, LANES)],
        out_specs=[rows(d), pl.BlockSpec((tm * d // LANES, LANES), lambda i: (i, 0)), rows(LANES), rows(LANES),
                   rows(LANES), const(1, LANES)],
        out_shape=[jax.ShapeDtypeStruct((n, d), F32), jax.ShapeDtypeStruct((n * d // LANES, LANES), F32),
                   jax.ShapeDtypeStruct((n, LANES), jnp.int32), jax.ShapeDtypeStruct((n, LANES), F32),
                   jax.ShapeDtypeStruct((n, LANES), jnp.int32), jax.ShapeDtypeStruct((1, LANES), jnp.int32)],
        scratch_shapes=[pltpu.VMEM((1, LANES), F32)],
        compiler_params=_cparams("arbitrary"),
        name="out_proj_router",
    )(xt, o_a, o_b, o_c, w_out.astype(BF16), gain.reshape(1, d), wr_hi, wr_lo, br)


def _padded_starts(cnt_row):
    cnt = jnp.broadcast_to(cnt_row, (8, LANES))
    padded = (((cnt + (MOE_BLOCK - 1)) // MOE_BLOCK) * MOE_BLOCK).astype(F32)
    before = jnp.where(_iota((LANES, LANES), 0) < _iota((LANES, LANES), 1), 1.0, 0.0).astype(F32)
    start = jnp.dot(padded, before, precision=HIGHEST, preferred_element_type=F32)
    return start, start + padded


def _dest_body(ri_ref, rank_ref, cnt_ref, d0_ref, d1_ref):
    tr = ri_ref.shape[0]
    nr = tr // LANES
    start, _ = _padded_starts(cnt_ref[...])
    start = start[0:1, :]
    ri = ri_ref[...]
    rank = rank_ref[...].astype(F32)
    lane = _iota((tr, LANES), 1)
    own_lane = (_iota((tr, LANES), 0) % LANES) == lane
    fold = jnp.where(_iota((nr, tr), 1) // LANES == _iota((nr, tr), 0), 1.0, 0.0).astype(F32)
    for slot, d_ref in enumerate((d0_ref, d1_ref)):
        begin = jnp.sum(jnp.where(lane == ri[:, slot:slot + 1], start, 0.0), axis=-1, keepdims=True)
        dest = begin + rank[:, slot:slot + 1]
        dense = jnp.dot(fold, jnp.where(own_lane, dest, 0.0), precision=HIGHEST, preferred_element_type=F32)
        d_ref[...] = dense.astype(jnp.int32)


def _destinations(ri, rank, cnt, tr=2048):
    n = ri.shape[0]
    tr = min(tr, n)
    rows = pl.BlockSpec((tr, LANES), lambda i: (i, 0))
    dense = pl.BlockSpec((tr // LANES, LANES), lambda i: (i, 0))
    return pl.pallas_call(
        _dest_body,
        grid=(n // tr,),
        in_specs=[rows, rows, pl.BlockSpec((1, LANES), lambda i: (0, 0))],
        out_specs=[dense, dense],
        out_shape=[jax.ShapeDtypeStruct((n // LANES, LANES), jnp.int32)] * 2,
        compiler_params=_cparams("parallel"),
        name="moe_destinations",
    )(ri, rank, cnt)


def _blkexp_body(cnt_ref, be_ref):
    nbp = be_ref.shape[0]
    _, end = _padded_starts(cnt_ref[...])
    end = end[0:1, :]
    lane = _iota((nbp, LANES), 1)
    row_start = (_iota((nbp, LANES), 0) * MOE_BLOCK).astype(F32)
    done = jnp.where((lane < MOE_EXPERTS) & (end <= row_start), 1.0, 0.0)
    be = jnp.minimum(jnp.sum(done, axis=-1, keepdims=True), MOE_EXPERTS - 1.0)
    be_ref[...] = jnp.broadcast_to(be, (nbp, LANES)).astype(jnp.int32)


def _block_experts(cnt, n_blocks):
    nbp = -(-n_blocks // 8) * 8
    out = pl.pallas_call(
        _blkexp_body,
        grid=(1,),
        in_specs=[pl.BlockSpec((1, LANES), lambda i: (0, 0))],
        out_specs=pl.BlockSpec((nbp, LANES), lambda i: (0, 0)),
        out_shape=jax.ShapeDtypeStruct((nbp, LANES), jnp.int32),
        name="moe_block_experts",
    )(cnt)
    return out[:n_blocks, 0]


def _for_each_row(n_rows, fn):
    def lane_row(i, carry):
        for j in range(LANES):
            fn(i * LANES + j, i, j)
        return carry

    lax.fori_loop(0, n_rows // LANES, lane_row, 0)


def _token_tile(ref, row, per):
    return ref.at[pl.ds(pl.multiple_of(row * per, per), per)]


def _scatter_body(per, d0_ref, d1_ref, cnt_ref, h_ref, xb_hbm, zbuf, sem, zsem):
    ts = h_ref.shape[0] // per
    n_rows = xb_hbm.shape[0] // per

    @pl.when(pl.program_id(0) == 0)
    def _():
        zbuf[...] = jnp.zeros_like(zbuf)

        zrows = zbuf.shape[0] // per

        def zero_copy(row, k):
            dst = xb_hbm.at[pl.ds(pl.multiple_of(row * per, per), k * per)]
            return pltpu.make_async_copy(zbuf.at[pl.ds(0, k * per)], dst, zsem)

        def zero_rows(lo, hi):
            n_full = lax.shift_right_logical(hi - lo, zrows.bit_length() - 1)
            rest_lo = lo + n_full * zrows
            rest = hi - rest_lo

            def each_copy(fn):
                def full(i, carry):
                    fn(zero_copy(lo + i * zrows, zrows))
                    return carry

                lax.fori_loop(0, n_full, full, 0)
                k = zrows // 2
                while k >= 1:
                    above = lax.bitwise_and(rest, -2 * k)

                    @pl.when(lax.bitwise_and(rest, k) != 0)
                    def _(k=k, above=above):
                        fn(zero_copy(rest_lo + above, k))
                    k //= 2

            each_copy(lambda cp: cp.start())
            each_copy(lambda cp: cp.wait())

        def expert(e, slab_start):
            used = cnt_ref[0, e]
            padded = ((used + (MOE_BLOCK - 1)) // MOE_BLOCK) * MOE_BLOCK
            zero_rows(slab_start + used, slab_start + padded)
            return slab_start + padded

        zero_rows(lax.fori_loop(0, MOE_EXPERTS, expert, 0), n_rows)

    def copies(t, i, j):
        src = _token_tile(h_ref, t, per)
        return (pltpu.make_async_copy(src, _token_tile(xb_hbm, d0_ref[i, j], per), sem),
                pltpu.make_async_copy(src, _token_tile(xb_hbm, d1_ref[i, j], per), sem))

    def start(t, i, j):
        for prio, cp in enumerate(copies(t, i, j)):
            cp.start(priority=prio)

    def wait(t, i, j):
        for cp in copies(t, i, j):
            cp.wait()

    _for_each_row(ts, start)
    _for_each_row(ts, wait)


def _scatter_rows(h2, dest0, dest1, cnt, n, n_rows, ts):
    per = h2.shape[0] // n
    smem = pl.BlockSpec((ts // LANES, LANES), lambda i: (i, 0), memory_space=pltpu.SMEM)
    return pl.pallas_call(
        functools.partial(_scatter_body, per),
        grid=(n // ts,),
        in_specs=[smem, smem, pl.BlockSpec((1, LANES), lambda i: (0, 0), memory_space=pltpu.SMEM),
                  pl.BlockSpec((ts * per, LANES), lambda i: (i, 0))],
        out_specs=pl.BlockSpec(memory_space=pl.ANY),
        out_shape=jax.ShapeDtypeStruct((n_rows * per, LANES), F32),
        scratch_shapes=[pltpu.VMEM((ZERO_ROWS * per, LANES), F32), pltpu.SemaphoreType.DMA(()),
                        pltpu.SemaphoreType.DMA(())],
        compiler_params=_cparams("arbitrary"),
        name="moe_scatter",
    )(dest0, dest1, cnt, h2)


def _expert_body(be_ref, x_ref, wg_ref, wu_ref, wd_ref, y_ref):
    del be_ref
    x = _load_token_tiles(x_ref, MOE_BLOCK)
    hid = _silu(_bdot(x, wg_ref[0, 0])) * _bdot(x, wu_ref[0, 0])
    _store_token_tiles(y_ref, _bdot(hid, wd_ref[0, 0]))


def _expert_mlp(xb, blk_expert, w_gate, w_up, w_down, layer):
    de, d = w_down.shape[2:]
    per = d // LANES
    nb = xb.shape[0] // (MOE_BLOCK * per)
    tiles = pl.BlockSpec((MOE_BLOCK * per, LANES), lambda i, be: (i, 0))
    grid_spec = pltpu.PrefetchScalarGridSpec(
        num_scalar_prefetch=1,
        grid=(nb,),
        in_specs=[tiles,
                  pl.BlockSpec((1, 1, d, de), lambda i, be: (layer, be[i], 0, 0)),
                  pl.BlockSpec((1, 1, d, de), lambda i, be: (layer, be[i], 0, 0)),
                  pl.BlockSpec((1, 1, de, d), lambda i, be: (layer, be[i], 0, 0))],
        out_specs=tiles,
    )
    return pl.pallas_call(
        _expert_body,
        grid_spec=grid_spec,
        out_shape=jax.ShapeDtypeStruct(xb.shape, F32),
        compiler_params=_cparams("arbitrary"),
        name="moe_experts",
    )(blk_expert, xb, w_gate, w_up, w_down)


def _combine_body(final_norm, d0_ref, d1_ref, d0_next, d1_next, rf_ref, x_ref, g_ref, yb_hbm, o_ref, gbuf, sems):
    ts, d = x_ref.shape
    per = d // LANES
    step = pl.program_id(0)
    slot = step % 2

    def gather(dest_refs, buf_slot):
        def copies(t, i, j):
            return [pltpu.make_async_copy(_token_tile(yb_hbm, dref[i, j], per),
                                          _token_tile(gbuf.at[buf_slot, k], t, per), sems.at[buf_slot])
                    for k, dref in enumerate(dest_refs)]
        return copies

    def start_all(copies):
        def start(t, i, j):
            for prio, cp in enumerate(copies(t, i, j)):
                cp.start(priority=prio)
        _for_each_row(ts, start)

    @pl.when(step == 0)
    def _():
        start_all(gather((d0_ref, d1_ref), slot))

    @pl.when(step + 1 < pl.num_programs(0))
    def _():
        start_all(gather((d0_next, d1_next), 1 - slot))

    current = gather((d0_ref, d1_ref), slot)

    def wait(t, i, j):
        for cp in current(t, i, j):
            cp.wait()

    _for_each_row(ts, wait)
    rf = rf_ref[...]
    x = x_ref[...] + (_load_token_tiles(gbuf.at[slot, 0], ts) * rf[:, 0:1]
                      + _load_token_tiles(gbuf.at[slot, 1], ts) * rf[:, 1:2])
    if final_norm:
        x = x * lax.rsqrt(jnp.mean(x * x, axis=-1, keepdims=True) + NORM_EPS) * g_ref[...]
    o_ref[...] = x


def _combine(x_mid, yb, dest0, dest1, rf, final_gain, ts):
    n, d = x_mid.shape
    final_norm = final_gain is not None
    gain = (final_gain if final_norm else jnp.ones((d,), F32)).reshape(1, d)
    last = n // ts - 1
    smem = pl.BlockSpec((ts // LANES, LANES), lambda i: (i, 0), memory_space=pltpu.SMEM)
    smem_next = pl.BlockSpec((ts // LANES, LANES), lambda i: (jnp.minimum(i + 1, last), 0), memory_space=pltpu.SMEM)
    rows = lambda w: pl.BlockSpec((ts, w), lambda i: (i, 0))
    return pl.pallas_call(
        functools.partial(_combine_body, final_norm),
        grid=(n // ts,),
        in_specs=[smem, smem, smem_next, smem_next, rows(LANES), rows(d), pl.BlockSpec((1, d), lambda i: (0, 0)),
                  pl.BlockSpec(memory_space=pl.ANY)],
        out_specs=rows(d),
        out_shape=jax.ShapeDtypeStruct((n, d), F32),
        scratch_shapes=[pltpu.VMEM((2, 2, ts * d // LANES, LANES), F32), pltpu.SemaphoreType.DMA((2,))],
        compiler_params=_cparams("arbitrary"),
        name="moe_combine",
    )(dest0, dest1, dest0, dest1, rf, x_mid, gain, yb)


def _moe(x_mid, h2, ri, rf, rank, cnt, w_gate, w_up, w_down, layer, final_gain):
    n, d = x_mid.shape
    n_assign = 2 * n
    n_blocks = (n_assign + MOE_EXPERTS * (MOE_BLOCK - 1) + MOE_BLOCK - 1) // MOE_BLOCK
    ts = min(1024, n)
    dest0, dest1 = _destinations(ri, rank, cnt)
    blk_expert = _block_experts(cnt, n_blocks)
    xb = _scatter_rows(h2, dest0, dest1, cnt, n, n_blocks * MOE_BLOCK, ts)
    yb = _expert_mlp(xb, blk_expert, w_gate, w_up, w_down, layer)
    return _combine(x_mid, yb, dest0, dest1, rf, final_gain, ts)


def kernel(x, norm_mix, norm_ffn, norm_final, w_in, w_out, rwkv_mu, rwkv_w0, rwkv_w2, rwkv_a0, rwkv_a2, rwkv_g2, rwkv_k_k, rwkv_k_a, rwkv_r_k, rwkv_lnx_w, rwkv_lnx_b, gdn_conv_w, gdn_a_log, gdn_dt_bias, gdn_norm_w, lru_conv_w, lru_conv_b, lru_w_a, lru_b_a, lru_w_x, lru_b_x, lru_lambda, moe_w_group, moe_b_group, moe_w_expert, moe_b_expert, moe_w_gate, moe_w_up, moe_w_down):
    bsz, seq, d = x.shape
    depth = w_in.shape[0]
    n = bsz * seq
    a_cols = rwkv_mu.shape[1]
    db = gdn_norm_w.shape[1] * GDN_HEADS
    dc = lru_conv_b.shape[1]
    n_ba = 2 * GDN_HEADS
    widths = (a_cols, 3 * db, db, 2 * dc, LANES)
    xt = x.reshape(n, d)
    for l in range(depth):
        w = w_in[l]
        b0 = a_cols
        c0 = b0 + 4 * db + n_ba
        w_cat = jnp.concatenate(
            [w[:, 0:a_cols], w[:, b0:b0 + 4 * db], w[:, c0:c0 + 2 * dc],
             w[:, b0 + 4 * db:c0], jnp.zeros((d, LANES - n_ba), F32)], axis=1).astype(BF16)
        p_a, p_qkv, p_z, p_c, p_ba = _in_proj(xt, seq, norm_mix[l], w_cat, widths, rwkv_mu[l], gdn_conv_w[l],
                                              lru_conv_w[l], lru_conv_b[l])
        o_a = _rwkv_group(p_a, bsz, seq, rwkv_w0[l], rwkv_w2[l], rwkv_a0[l], rwkv_a2[l], rwkv_g2[l],
                          rwkv_k_k[l], rwkv_k_a[l], rwkv_r_k[l], rwkv_lnx_w[l], rwkv_lnx_b[l])
        o_b = _gdn_group(p_qkv, p_z, p_ba, bsz, seq, gdn_a_log[l], gdn_dt_bias[l], gdn_norm_w[l])
        o_c = _lru_group(p_c, bsz, seq, lru_w_a[l], lru_b_a[l], lru_w_x[l], lru_b_x[l], lru_lambda[l])
        x_mid, h2, ri, rf, rank, cnt = _out_proj_router(xt, o_a, o_b, o_c, w_out[l], norm_ffn[l], moe_w_group[l],
                                                        moe_b_group[l], moe_w_expert[l], moe_b_expert[l])
        xt = _moe(x_mid, h2, ri, rf, rank, cnt, moe_w_gate, moe_w_up, moe_w_down, l,
                  norm_final if l == depth - 1 else None)
    return xt.reshape(bsz, seq, d)
```

```python
import functools

import jax
import jax.numpy as jnp
from jax import lax
from jax.experimental import pallas as pl
from jax.experimental.pallas import tpu as pltpu

F32 = jnp.float32
BF16 = jnp.bfloat16
HIGHEST = lax.Precision.HIGHEST

NORM_EPS = 1e-6
CONV_WIDTH = 4
CHUNK = 64
SCAN_TILE = 256
SCAN_SEQS = 4
RWKV_HEADS = 4
RWKV_GN_EPS = 64e-5
GDN_HEADS = 4
LRU_C = 8.0
MOE_GROUPS = 4
MOE_PER_GROUP = 8
MOE_EXPERTS = MOE_GROUPS * MOE_PER_GROUP
MOE_BLOCK = 512
ZERO_ROWS = 256
LANES = 128
SUBLANES = 8
NEG_BIG = -1e30
VMEM_LIMIT = 56 * 1024 * 1024


def _cparams(*sem):
    return pltpu.CompilerParams(dimension_semantics=sem, vmem_limit_bytes=VMEM_LIMIT)


def _bdot(a, b):
    return jnp.dot(a.astype(BF16), b.astype(BF16), preferred_element_type=F32)


def _bdot_nt(a, b):
    return lax.dot_general(a.astype(BF16), b.astype(BF16), (((1,), (1,)), ((), ())),
                           preferred_element_type=F32)


def _bdot_tn(a, b):
    return lax.dot_general(a.astype(BF16), b.astype(BF16), (((0,), (0,)), ((), ())),
                           preferred_element_type=F32)


def _exact_dot(a, b01):
    hi = a.astype(BF16)
    lo = (a - hi.astype(F32)).astype(BF16)
    return (jnp.dot(hi, b01, preferred_element_type=F32)
            + jnp.dot(lo, b01, preferred_element_type=F32))


def _exact_dot_left(a01, b):
    hi = b.astype(BF16)
    lo = (b - hi.astype(F32)).astype(BF16)
    return (jnp.dot(a01, hi, preferred_element_type=F32)
            + jnp.dot(a01, lo, preferred_element_type=F32))


def _load_token_tiles(ref, rows):
    per = ref.shape[0] // rows
    return jnp.concatenate([ref[pl.ds(c, rows, stride=per), :] for c in range(per)], axis=1)


def _store_token_tiles(ref, value):
    rows, d = value.shape
    per = d // LANES
    for c in range(per):
        ref[pl.ds(c, rows, stride=per), :] = value[:, c * LANES:(c + 1) * LANES]


def _softplus(x):
    return jnp.maximum(x, 0.0) + jnp.log1p(jnp.exp(-jnp.abs(x)))


def _sigmoid(x):
    return 1.0 / (1.0 + jnp.exp(-x))


def _silu(x):
    return x * _sigmoid(x)


def _iota(shape, dim):
    return lax.broadcasted_iota(jnp.int32, shape, dim)


def _head_ones(width, head_dim):
    r = _iota((width, width), 0) // head_dim
    c = _iota((width, width), 1) // head_dim
    return jnp.where(r == c, 1.0, 0.0).astype(BF16)


def _stack_heads(x, n_heads):
    c, w = x.shape
    hd = w // n_heads
    xb = x.astype(BF16)
    lane_head = _iota((c, w), 1) // hd
    return jnp.concatenate([jnp.where(lane_head == h, xb, jnp.zeros_like(xb)) for h in range(n_heads)],
                           axis=0)


def _tri_masks(c, n_heads):
    r = _iota((c, n_heads * c), 0)
    s = _iota((c, n_heads * c), 1) % c
    return s < r, s <= r


def _unit_lower_inverses(l_mats, n_heads):
    shape = l_mats[0].shape
    c = shape[0]
    eye = jnp.where(_iota(shape, 1) % c == _iota(shape, 0), 1.0, 0.0).astype(F32)
    ps = [eye + l for l in l_mats]
    ms = [_bdot(l, _stack_heads(l, n_heads)) for l in l_mats]
    span = 2
    while span < c:
        last = span * 2 >= c
        nxt = [_bdot(p if last else jnp.concatenate([m, p], axis=0), _stack_heads(m, n_heads))
               for m, p in zip(ms, ps)]
        ps = [p + (r if last else r[c:]) for p, r in zip(ps, nxt)]
        ms = [None if last else r[0:c] for r in nxt]
        span *= 2
    return ps


def _head_block_mask(width, head_dim):
    r = _iota((width, width), 0) // head_dim
    c = _iota((width, width), 1) // head_dim
    return r == c


def _shift_rows(y, tail, j):
    rolled = pltpu.roll(y, j, 0)
    head = jnp.where(_iota(tail.shape, 0) < j, pltpu.roll(tail, j, 0), rolled[0:8])
    return jnp.concatenate([head, rolled[8:]], axis=0)


def _causal_conv(y, tail, cw):
    acc = cw[CONV_WIDTH - 1:CONV_WIDTH, :] * y
    for j in range(1, CONV_WIDTH):
        acc = acc + cw[CONV_WIDTH - 1 - j:CONV_WIDTH - j, :] * _shift_rows(y, tail, j)
    return acc


def _inproj_body(tiles_per_seq, x_ref, g_ref, w_ref, mu_ref, gcw_ref, lcw_ref, lcb_ref,
                 a_ref, qkv_ref, z_ref, c_ref, ba_ref, atail, qtail, ctail):
    tm = x_ref.shape[0]
    tails = (atail, qtail, ctail)

    @pl.when(pl.program_id(0) % tiles_per_seq == 0)
    def _():
        for tail in tails:
            tail[...] = jnp.zeros_like(tail)

    x = x_ref[...]
    h = x * lax.rsqrt(jnp.mean(x * x, axis=-1, keepdims=True) + NORM_EPS) * g_ref[...]
    hb = h.astype(BF16)
    wa, wq, wz, wc = a_ref.shape[-1], qkv_ref.shape[-1], z_ref.shape[-1], c_ref.shape[-1]
    dc = ctail.shape[1]

    def proj(off, width):
        return jnp.dot(hb, w_ref[:, off:off + width], preferred_element_type=F32)

    for s in range(3):
        cols = slice(s * (wq // 3), (s + 1) * (wq // 3))
        y = proj(wa + cols.start, wq // 3)
        qkv_ref[:, cols] = _silu(_causal_conv(y, qtail[:, cols], gcw_ref[:, cols]))
        qtail[:, cols] = y[tm - 8:tm, :]
    p = proj(0, wa)
    a_ref[...] = p + (_shift_rows(p, atail[...], 1) - p) * mu_ref[...]
    atail[...] = p[tm - 8:tm, :]
    z_ref[...] = proj(wa + wq, wz)
    pc = proj(wa + wq + wz, wc)
    c_ref[:, 0:dc] = lcb_ref[...] + _causal_conv(pc[:, 0:dc], ctail[...], lcw_ref[...])
    c_ref[:, dc:] = pc[:, dc:]
    ctail[...] = pc[tm - 8:tm, 0:dc]
    ba_ref[...] = proj(wa + wq + wz + wc, ba_ref.shape[-1])


def _in_proj(xt, seq, gain, w_cat, widths, mu, gdn_conv_w, lru_conv_w, lru_conv_b, tm=512):
    n, d = xt.shape
    ntot = w_cat.shape[1]
    tm = min(tm, seq)
    dc = lru_conv_b.shape[0]
    const = lambda r, w: pl.BlockSpec((r, w), lambda i: (0, 0))
    return pl.pallas_call(
        functools.partial(_inproj_body, seq // tm),
        grid=(n // tm,),
        in_specs=[pl.BlockSpec((tm, d), lambda i: (i, 0)), const(1, d), const(d, ntot),
                  const(1, widths[0]), const(CONV_WIDTH, widths[1]), const(CONV_WIDTH, dc), const(1, dc)],
        out_specs=[pl.BlockSpec((tm, w), lambda i: (i, 0)) for w in widths],
        out_shape=[jax.ShapeDtypeStruct((n, w), F32) for w in widths],
        scratch_shapes=[pltpu.VMEM((8, widths[0]), F32), pltpu.VMEM((8, widths[1]), F32),
                        pltpu.VMEM((8, dc), F32)],
        compiler_params=_cparams("arbitrary"),
        name="in_proj",
    )(xt, gain.reshape(1, d), w_cat, mu.reshape(1, -1), gdn_conv_w, lru_conv_w, lru_conv_b.reshape(1, dc))


def _rwkv_body(xs_ref, w0_ref, a0_ref, kk_ref, ka_ref, rk_ref, lnw_ref, lnb_ref,
               w2_ref, a2_ref, g2_ref, o_ref, state):
    c = CHUNK
    n_seq, tt, da = o_ref.shape
    nh = RWKV_HEADS
    hd = da // nh

    @pl.when(pl.program_id(1) == 0)
    def _():
        state[...] = jnp.zeros_like(state)

    xs = jnp.concatenate([xs_ref[s] for s in range(n_seq)], axis=0)
    r = xs[:, 0:da]
    k = xs[:, da:2 * da]
    v = xs[:, 2 * da:3 * da]
    lo = xs[:, 3 * da:]

    w_log = -_softplus(-(w0_ref[...] + _bdot(jnp.tanh(lo), w2_ref[...]))) - 0.5
    lw = -jnp.exp(w_log)
    a = _sigmoid(a0_ref[...] + _bdot(lo, a2_ref[...]))
    g = _bdot(_sigmoid(lo), g2_ref[...])

    ones_h = _head_ones(da, hd)
    kkr = k * kk_ref[...]
    kk = kkr * lax.rsqrt(_exact_dot(kkr * kkr, ones_h) + 1e-6)
    k2 = k * (1.0 + (a - 1.0) * ka_ref[...])
    alpha = -(kk * a)

    rr = _iota((tt, tt), 0)
    cc = _iota((tt, tt), 1)
    tri = jnp.where((cc <= rr) & (cc // c == rr // c), 1.0, 0.0).astype(BF16)
    cum = jnp.concatenate([_exact_dot_left(tri, lw[s * tt:(s + 1) * tt]) for s in range(n_seq)], axis=0)
    per_seq = tt // c
    n_chunks = n_seq * per_seq
    chunk_rows = [slice(g * c, (g + 1) * c) for g in range(n_chunks)]
    cum_ends = [cum[(g + 1) * c - 1:(g + 1) * c, :] for g in range(n_chunks)]
    cum_last = jnp.concatenate([jnp.broadcast_to(ce, (c, da)) for ce in cum_ends], axis=0)
    e_pos = jnp.exp(cum)
    e_neg = jnp.exp(-cum)
    e_tail = jnp.exp(cum_last - cum)
    r_t = r * e_pos
    b_t = kk * jnp.exp(cum - lw)
    a_t = alpha * e_neg
    k_t = k2 * e_neg
    a_end = alpha * e_tail
    k_end = k2 * e_tail

    nl = nh * c
    strict, incl = _tri_masks(c, nh)
    grams = [_bdot_nt(jnp.concatenate([b_t[rows], r_t[rows]], axis=0),
                      jnp.concatenate([_stack_heads(a_t[rows], nh), _stack_heads(k_t[rows], nh)], axis=0))
             for rows in chunk_rows]
    t_invs = _unit_lower_inverses([jnp.where(strict, gm[0:c, 0:nl], 0.0) for gm in grams], nh)
    vss = [_stack_heads(v[rows], nh) for rows in chunk_rows]
    lkvs = [_bdot(jnp.where(strict, gm[0:c, nl:], 0.0), vs) for gm, vs in zip(grams, vss)]
    wus = [_bdot(t_inv, jnp.concatenate([_stack_heads(b_t[rows], nh), _stack_heads(lkv, nh)], axis=1))
           for t_inv, lkv, rows in zip(t_invs, lkvs, chunk_rows)]
    s_as = [jnp.where(incl, gm[c:, 0:nl], 0.0) for gm in grams]
    s_ks = [jnp.where(incl, gm[c:, nl:], 0.0) for gm in grams]
    block_diag = _head_block_mask(da, hd)
    w_mats = [wu[:, 0:da] for wu in wus]
    u0s = [wu[:, da:] for wu in wus]
    q_effs = [r_t[rows] + _bdot(s_a, _stack_heads(w_mat, nh)) for rows, s_a, w_mat in zip(chunk_rows, s_as, w_mats)]
    o_consts = [_bdot(jnp.concatenate([s_a, s_k], axis=1), jnp.concatenate([_stack_heads(u0, nh), vs], axis=0))
                for s_a, s_k, u0, vs in zip(s_as, s_ks, u0s, vss)]
    s_lins = [jnp.where(block_diag, _bdot_tn(w_mat, a_end[rows]), 0.0) for w_mat, rows in zip(w_mats, chunk_rows)]
    s_consts = [jnp.where(block_diag, _bdot_tn(jnp.concatenate([u0, v[rows]], axis=0),
                                               jnp.concatenate([a_end[rows], k_end[rows]], axis=0)), 0.0)
                for u0, rows in zip(u0s, chunk_rows)]

    s_curs = [state[s] for s in range(n_seq)]
    outs = [None] * n_chunks
    for step in range(per_seq):
        for s in range(n_seq):
            i = s * per_seq + step
            outs[i] = _bdot_nt(q_effs[i], s_curs[s]) + o_consts[i]
            s_curs[s] = s_curs[s] * jnp.exp(cum_ends[i]) + _bdot(s_curs[s], s_lins[i]) + s_consts[i]
    for s in range(n_seq):
        state[s] = s_curs[s]
    o = jnp.concatenate(outs, axis=0)

    inv_hd = 1.0 / hd
    mean = _exact_dot(o, ones_h) * inv_hd
    cen = o - mean
    var = _exact_dot(cen * cen, ones_h) * inv_hd
    o_n = cen * lax.rsqrt(var + RWKV_GN_EPS) * lnw_ref[...] + lnb_ref[...]
    bonus = _exact_dot(r * k2 * rk_ref[...], ones_h) * v
    res = (o_n + bonus) * g
    for s in range(n_seq):
        o_ref[s] = res[s * tt:(s + 1) * tt].astype(o_ref.dtype)


def _rwkv_group(p_a, bsz, seq, w0, w2, a0, a2, g2, k_k, k_a, r_k, lnx_w, lnx_b, tt=SCAN_TILE):
    n, a_cols = p_a.shape
    da = w0.shape[-1]
    n_lora = a_cols - 3 * da
    tt = min(tt, seq)
    nchunk = seq // tt
    d_dec, d_aaa = w2.shape[0], a2.shape[0]
    w2p = jnp.zeros((n_lora, da), F32).at[0:d_dec].set(w2).astype(BF16)
    a2p = jnp.zeros((n_lora, da), F32).at[d_dec:d_dec + d_aaa].set(a2).astype(BF16)
    g2p = jnp.zeros((n_lora, da), F32).at[d_dec + d_aaa:].set(g2).astype(BF16)
    row = lambda t: t.reshape(1, -1).astype(F32)
    vec = lambda w: pl.BlockSpec((1, w), lambda b, c: (0, 0))
    mat = pl.BlockSpec((n_lora, da), lambda b, c: (0, 0))
    nb = min(SCAN_SEQS, bsz)
    seqs = lambda w: pl.BlockSpec((nb, tt, w), lambda b, c: (b, c, 0))
    out = pl.pallas_call(
        _rwkv_body,
        grid=(bsz // nb, nchunk),
        in_specs=[seqs(a_cols)] + [vec(da)] * 7 + [mat] * 3,
        out_specs=seqs(da),
        out_shape=jax.ShapeDtypeStruct((bsz, seq, da), BF16),
        scratch_shapes=[pltpu.VMEM((nb, da, da), F32)],
        compiler_params=_cparams("parallel", "arbitrary"),
        name="rwkv7",
    )(p_a.reshape(bsz, seq, a_cols), row(w0), row(a0), row(k_k), row(k_a), row(r_k), row(lnx_w),
      row(lnx_b), w2p, a2p, g2p)
    return out.reshape(n, da)


def _gdn_body(qkv_ref, z_ref, ba_ref, alog_ref, dtb_ref, nw_ref, o_ref, state):
    c = CHUNK
    n_seq, tt, db = o_ref.shape
    rows_all = n_seq * tt
    nh = GDN_HEADS
    hd = db // nh
    nl = nh * c

    @pl.when(pl.program_id(1) == 0)
    def _():
        state[...] = jnp.zeros_like(state)

    def all_rows(ref, cols=slice(None)):
        return jnp.concatenate([ref[s, :, cols] for s in range(n_seq)], axis=0)

    q = all_rows(qkv_ref, slice(0, db))
    k = all_rows(qkv_ref, slice(db, 2 * db))
    v = all_rows(qkv_ref, slice(2 * db, 3 * db))

    def per_head(col_of_head):
        return jnp.concatenate([jnp.broadcast_to(col_of_head(h), (rows_all, hd)) for h in range(nh)], axis=1)

    def head_sumsq(x):
        return per_head(lambda h: jnp.sum(x[:, h * hd:(h + 1) * hd] * x[:, h * hd:(h + 1) * hd],
                                          axis=-1, keepdims=True))

    q = q * lax.rsqrt(head_sumsq(q) + 1e-6) * (hd ** -0.5)
    k = k * lax.rsqrt(head_sumsq(k) + 1e-6)

    ba = all_rows(ba_ref)
    beta_l = _sigmoid(ba)
    g_l = -jnp.exp(alog_ref[...]) * _softplus(ba + dtb_ref[...])
    row_in_chunk = _iota((rows_all, LANES), 0) % c
    gcum_l = g_l
    d = 1
    while d < c:
        gcum_l = gcum_l + jnp.where(row_in_chunk >= d, pltpu.roll(gcum_l, d, 0), 0.0)
        d *= 2
    per_seq = tt // c
    n_chunks = n_seq * per_seq
    chunk_rows = [slice(g * c, (g + 1) * c) for g in range(n_chunks)]
    g_last_l = jnp.concatenate([jnp.broadcast_to(gcum_l[(g + 1) * c - 1:(g + 1) * c, :], (c, LANES))
                                for g in range(n_chunks)], axis=0)
    e_g_l = jnp.exp(gcum_l)
    e_tail_l = jnp.exp(g_last_l - gcum_l)
    beta = per_head(lambda h: beta_l[:, h:h + 1])
    e_g = per_head(lambda h: e_g_l[:, nh + h:nh + h + 1])
    e_tail = per_head(lambda h: e_tail_l[:, nh + h:nh + h + 1])
    kb = k * beta
    vb = v * beta
    kbe = kb * e_g
    qe = q * e_g
    k_end = k * e_tail

    strict, incl = _tri_masks(c, nh)
    lane_head = _iota((c, nl), 1) // c
    on_diag = (_iota((c, nl), 1) % c) == _iota((c, nl), 0)
    dmats = []
    for rows in chunk_rows:
        gc = gcum_l[rows]
        gcol = jnp.broadcast_to(gc[:, nh:nh + 1], (c, nl))
        for h in range(1, nh):
            gcol = jnp.where(lane_head == h, jnp.broadcast_to(gc[:, nh + h:nh + h + 1], (c, nl)), gcol)
        grow = jnp.sum(jnp.where(on_diag, gcol, 0.0), axis=0, keepdims=True)
        dmats.append(jnp.exp(jnp.where(incl, gcol - grow, NEG_BIG)))
    grams = [_bdot_nt(jnp.concatenate([kb[rows], q[rows]], axis=0), _stack_heads(k[rows], nh))
             for rows in chunk_rows]
    intras = [jnp.where(incl, gm[c:] * dm, 0.0) for gm, dm in zip(grams, dmats)]
    t_invs = _unit_lower_inverses([jnp.where(strict, -(gm[0:c] * dm), 0.0) for gm, dm in zip(grams, dmats)], nh)
    sols = [_bdot(t_inv, jnp.concatenate([_stack_heads(vb[rows], nh), _stack_heads(kbe[rows], nh)], axis=1))
            for t_inv, rows in zip(t_invs, chunk_rows)]

    s_curs = [state[s] for s in range(n_seq)]
    block_diag = _head_block_mask(db, hd)
    outs = [None] * n_chunks
    for g in range(per_seq):
        for s in range(n_seq):
            i = s * per_seq + g
            rows = chunk_rows[i]
            u_s, w_s = sols[i][:, 0:db], sols[i][:, db:]
            wq = _bdot(jnp.concatenate([w_s, qe[rows]], axis=0), s_curs[s])
            v_new = u_s - wq[0:c]
            outs[i] = wq[c:] + _bdot(intras[i], _stack_heads(v_new, nh))
            upd = _bdot_tn(k_end[rows], v_new)
            s_curs[s] = s_curs[s] * e_g[(i + 1) * c - 1:(i + 1) * c, :] + jnp.where(block_diag, upd, 0.0)
    for s in range(n_seq):
        state[s] = s_curs[s]
    o = jnp.concatenate(outs, axis=0)
    o = o * lax.rsqrt(head_sumsq(o) * (1.0 / hd) + NORM_EPS) * nw_ref[...]
    res = o * _silu(all_rows(z_ref))
    for s in range(n_seq):
        o_ref[s] = res[s * tt:(s + 1) * tt].astype(o_ref.dtype)


def _gdn_group(qkv, z, ba, bsz, seq, a_log, dt_bias, norm_w, tt=SCAN_TILE):
    n, w3 = qkv.shape
    db = z.shape[-1]
    nh = GDN_HEADS
    tt = min(tt, seq)
    nchunk = seq // tt
    alog_row = jnp.zeros((1, LANES), F32).at[0, nh:2 * nh].set(a_log)
    dtb_row = jnp.zeros((1, LANES), F32).at[0, nh:2 * nh].set(dt_bias)
    nw_row = jnp.tile(norm_w, nh).reshape(1, db)
    nb = min(SCAN_SEQS, bsz)
    seqs = lambda w: pl.BlockSpec((nb, tt, w), lambda b, c: (b, c, 0))
    const = lambda r, w: pl.BlockSpec((r, w), lambda b, c: (0, 0))
    out = pl.pallas_call(
        _gdn_body,
        grid=(bsz // nb, nchunk),
        in_specs=[seqs(w3), seqs(db), seqs(LANES), const(1, LANES), const(1, LANES), const(1, db)],
        out_specs=seqs(db),
        out_shape=jax.ShapeDtypeStruct((bsz, seq, db), BF16),
        scratch_shapes=[pltpu.VMEM((nb, db, db), F32)],
        compiler_params=_cparams("parallel", "arbitrary"),
        name="gdn",
    )(qkv.reshape(bsz, seq, w3), z.reshape(bsz, seq, db), ba.reshape(bsz, seq, LANES), alog_row, dtb_row, nw_row)
    return out.reshape(n, db)


def _lru_body(p_ref, wax_ref, ba_ref, bx_ref, lam_ref, o_ref, hcar):
    tt = p_ref.shape[0]
    dc = o_ref.shape[-1]

    @pl.when(pl.program_id(1) == 0)
    def _():
        hcar[...] = jnp.zeros_like(hcar)

    conv = p_ref[:, 0:dc]
    gate = p_ref[:, dc:]

    ri = _bdot(conv, wax_ref[...])
    r = _sigmoid(ri[:, 0:dc] + ba_ref[...])
    i = _sigmoid(ri[:, dc:] + bx_ref[...])
    log_a = -LRU_C * r * _softplus(-lam_ref[...])
    a = jnp.exp(log_a)
    mult = jnp.sqrt(jnp.tanh(-log_a) * (a * a + 1.0))
    row = _iota((tt, dc), 0)
    mult = jnp.where((row == 0) & (pl.program_id(1) == 0), 1.0, mult)
    bv = conv * i * mult

    av = a
    row_in_group = row % SUBLANES
    d = 1
    while d < SUBLANES:
        a_sh = jnp.where(row_in_group >= d, pltpu.roll(av, d, 0), 1.0)
        b_sh = jnp.where(row_in_group >= d, pltpu.roll(bv, d, 0), 0.0)
        bv = av * b_sh + bv
        av = av * a_sh
        d *= 2
    carry = hcar[...]
    groups = []
    for i in range(tt // SUBLANES):
        rows = slice(i * SUBLANES, (i + 1) * SUBLANES)
        hg = av[rows] * carry + bv[rows]
        groups.append(hg)
        carry = hg[SUBLANES - 1:SUBLANES, :]
    h = jnp.concatenate(groups, axis=0)
    hcar[...] = carry
    gelu = 0.5 * gate * (1.0 + jnp.tanh(0.7978845608028654 * (gate + 0.044715 * gate * gate * gate)))
    o_ref[...] = (h * gelu).astype(o_ref.dtype)


def _block_diag(w):
    nb, bi, bo = w.shape
    out = jnp.zeros((nb * bi, nb * bo), w.dtype)
    for b in range(nb):
        out = out.at[b * bi:(b + 1) * bi, b * bo:(b + 1) * bo].set(w[b])
    return out


def _lru_group(p_c, bsz, seq, w_a, b_a, w_x, b_x, lam, tt=256):
    n, c_cols = p_c.shape
    dc = c_cols // 2
    tt = min(tt, seq)
    nt = seq // tt
    wax = jnp.concatenate([_block_diag(w_a), _block_diag(w_x)], axis=1).astype(BF16)
    row = lambda t: t.reshape(1, -1).astype(F32)
    const = lambda r, w: pl.BlockSpec((r, w), lambda b, c: (0, 0))
    return pl.pallas_call(
        _lru_body,
        grid=(bsz, nt),
        in_specs=[pl.BlockSpec((tt, c_cols), lambda b, c: (b * nt + c, 0)),
                  const(dc, 2 * dc), const(1, dc), const(1, dc), const(1, dc)],
        out_specs=pl.BlockSpec((tt, dc), lambda b, c: (b * nt + c, 0)),
        out_shape=jax.ShapeDtypeStruct((n, dc), BF16),
        scratch_shapes=[pltpu.VMEM((1, dc), F32)],
        compiler_params=_cparams("parallel", "arbitrary"),
        name="rglru",
    )(p_c, wax, row(b_a), row(b_x), row(lam))


def _outproj_body(x_ref, ma_ref, mb_ref, mc_ref, wo_ref, g_ref, wrh_ref, wrl_ref, br_ref,
                  xo_ref, h_ref, ri_ref, rf_ref, rank_ref, cnt_ref, carry):
    da = ma_ref.shape[-1]
    db = mb_ref.shape[-1]
    tm = x_ref.shape[0]

    @pl.when(pl.program_id(0) == 0)
    def _():
        carry[...] = jnp.zeros_like(carry)

    mix = (jnp.dot(ma_ref[...].astype(BF16), wo_ref[0:da, :], preferred_element_type=F32)
           + jnp.dot(mb_ref[...].astype(BF16), wo_ref[da:da + db, :], preferred_element_type=F32)
           + jnp.dot(mc_ref[...].astype(BF16), wo_ref[da + db:, :], preferred_element_type=F32))
    x = x_ref[...] + mix
    xo_ref[...] = x
    h = x * lax.rsqrt(jnp.mean(x * x, axis=-1, keepdims=True) + NORM_EPS) * g_ref[...]
    _store_token_tiles(h_ref, h)

    h_hi = h.astype(BF16)
    h_lo = (h - h_hi.astype(F32)).astype(BF16)
    logits = (jnp.dot(h_hi, wrh_ref[...], preferred_element_type=F32)
              + jnp.dot(h_lo, wrh_ref[...], preferred_element_type=F32)
              + jnp.dot(h_hi, wrl_ref[...], preferred_element_type=F32)) + br_ref[...]
    lane = _iota((tm, LANES), 1)
    far = 4 * LANES
    gmask = lane < MOE_GROUPS
    gl = jnp.where(gmask, logits, NEG_BIG)
    gmax = jnp.max(gl, axis=-1, keepdims=True)
    gidx = jnp.min(jnp.where(gl == gmax, lane, far), axis=-1, keepdims=True)
    gsum = jnp.sum(jnp.where(gmask, jnp.exp(gl - gmax), 0.0), axis=-1, keepdims=True)
    grp_w = 1.0 / gsum
    eidx = lane - MOE_GROUPS
    emask = (eidx >= 0) & (eidx < MOE_EXPERTS) & ((eidx // MOE_PER_GROUP) == gidx)
    el = jnp.where(emask, logits, NEG_BIG)
    v1 = jnp.max(el, axis=-1, keepdims=True)
    i1 = jnp.min(jnp.where(el == v1, lane, far), axis=-1, keepdims=True)
    el2 = jnp.where(lane == i1, NEG_BIG, el)
    v2 = jnp.max(el2, axis=-1, keepdims=True)
    i2 = jnp.min(jnp.where(el2 == v2, lane, far), axis=-1, keepdims=True)
    pr = jnp.exp(v2 - v1)
    g0 = grp_w / (1.0 + pr)
    g1 = g0 * pr
    ri_ref[...] = jnp.where(lane == 0, i1 - MOE_GROUPS, jnp.where(lane == 1, i2 - MOE_GROUPS, 0))
    rf_ref[...] = jnp.where(lane == 0, g0, jnp.where(lane == 1, g1, 0.0))

    oh0 = lane == (i1 - MOE_GROUPS)
    oh1 = lane == (i2 - MOE_GROUPS)
    oh = jnp.where(oh0 | oh1, 1.0, 0.0)
    earlier = jnp.where(_iota((tm, tm), 1) < _iota((tm, tm), 0), 1.0, 0.0).astype(BF16)
    seen = jnp.dot(earlier, oh.astype(BF16), preferred_element_type=F32) + carry[...]
    r0 = jnp.sum(jnp.where(oh0, seen, 0.0), axis=-1, keepdims=True)
    r1 = jnp.sum(jnp.where(oh1, seen, 0.0), axis=-1, keepdims=True)
    rank_ref[...] = jnp.where(lane == 0, r0, jnp.where(lane == 1, r1, 0.0)).astype(jnp.int32)
    carry[...] = carry[...] + jnp.sum(oh, axis=0, keepdims=True)
    cnt_ref[...] = carry[...].astype(jnp.int32)


def _out_proj_router(xt, o_a, o_b, o_c, w_out, gain, w_group, b_group, w_expert, b_expert, tm=512):
    n, d = xt.shape
    tm = min(tm, n)
    ng, ne = w_group.shape[1], w_expert.shape[1]
    wr = jnp.zeros((d, LANES), F32).at[:, 0:ng].set(w_group).at[:, ng:ng + ne].set(w_expert)
    wr_hi = wr.astype(BF16)
    wr_lo = (wr - wr_hi.astype(F32)).astype(BF16)
    br = jnp.zeros((1, LANES), F32).at[0, 0:ng].set(b_group).at[0, ng:ng + ne].set(b_expert)
    rows = lambda w: pl.BlockSpec((tm, w), lambda i: (i, 0))
    const = lambda r, w: pl.BlockSpec((r, w), lambda i: (0, 0))
    return pl.pallas_call(
        _outproj_body,
        grid=(n // tm,),
        in_specs=[rows(d), rows(o_a.shape[1]), rows(o_b.shape[1]), rows(o_c.shape[1]),
                  const(d, d), const(1, d), const(d, LANES), const(d, LANES), const(1, LANES)],
        out_specs=[rows(d), pl.BlockSpec((tm * d // LANES, LANES), lambda i: (i, 0)), rows(LANES), rows(LANES),
                   rows(LANES), const(1, LANES)],
        out_shape=[jax.ShapeDtypeStruct((n, d), F32), jax.ShapeDtypeStruct((n * d // LANES, LANES), F32),
                   jax.ShapeDtypeStruct((n, LANES), jnp.int32), jax.ShapeDtypeStruct((n, LANES), F32),
                   jax.ShapeDtypeStruct((n, LANES), jnp.int32), jax.ShapeDtypeStruct((1, LANES), jnp.int32)],
        scratch_shapes=[pltpu.VMEM((1, LANES), F32)],
        compiler_params=_cparams("arbitrary"),
        name="out_proj_router",
    )(xt, o_a, o_b, o_c, w_out.astype(BF16), gain.reshape(1, d), wr_hi, wr_lo, br)


def _padded_starts(cnt_row):
    cnt = jnp.broadcast_to(cnt_row, (8, LANES))
    padded = (((cnt + (MOE_BLOCK - 1)) // MOE_BLOCK) * MOE_BLOCK).astype(F32)
    before = jnp.where(_iota((LANES, LANES), 0) < _iota((LANES, LANES), 1), 1.0, 0.0).astype(F32)
    start = jnp.dot(padded, before, precision=HIGHEST, preferred_element_type=F32)
    return start, start + padded


def _dest_body(ri_ref, rank_ref, cnt_ref, d0_ref, d1_ref):
    tr = ri_ref.shape[0]
    nr = tr // LANES
    start, _ = _padded_starts(cnt_ref[...])
    start = start[0:1, :]
    ri = ri_ref[...]
    rank = rank_ref[...].astype(F32)
    lane = _iota((tr, LANES), 1)
    own_lane = (_iota((tr, LANES), 0) % LANES) == lane
    fold = jnp.where(_iota((nr, tr), 1) // LANES == _iota((nr, tr), 0), 1.0, 0.0).astype(F32)
    for slot, d_ref in enumerate((d0_ref, d1_ref)):
        begin = jnp.sum(jnp.where(lane == ri[:, slot:slot + 1], start, 0.0), axis=-1, keepdims=True)
        dest = begin + rank[:, slot:slot + 1]
        dense = jnp.dot(fold, jnp.where(own_lane, dest, 0.0), precision=HIGHEST, preferred_element_type=F32)
        d_ref[...] = dense.astype(jnp.int32)


def _destinations(ri, rank, cnt, tr=2048):
    n = ri.shape[0]
    tr = min(tr, n)
    rows = pl.BlockSpec((tr, LANES), lambda i: (i, 0))
    dense = pl.BlockSpec((tr // LANES, LANES), lambda i: (i, 0))
    return pl.pallas_call(
        _dest_body,
        grid=(n // tr,),
        in_specs=[rows, rows, pl.BlockSpec((1, LANES), lambda i: (0, 0))],
        out_specs=[dense, dense],
        out_shape=[jax.ShapeDtypeStruct((n // LANES, LANES), jnp.int32)] * 2,
        compiler_params=_cparams("parallel"),
        name="moe_destinations",
    )(ri, rank, cnt)


def _blkexp_body(cnt_ref, be_ref):
    nbp = be_ref.shape[0]
    _, end = _padded_starts(cnt_ref[...])
    end = end[0:1, :]
    lane = _iota((nbp, LANES), 1)
    row_start = (_iota((nbp, LANES), 0) * MOE_BLOCK).astype(F32)
    done = jnp.where((lane < MOE_EXPERTS) & (end <= row_start), 1.0, 0.0)
    be = jnp.minimum(jnp.sum(done, axis=-1, keepdims=True), MOE_EXPERTS - 1.0)
    be_ref[...] = jnp.broadcast_to(be, (nbp, LANES)).astype(jnp.int32)


def _block_experts(cnt, n_blocks):
    nbp = -(-n_blocks // 8) * 8
    out = pl.pallas_call(
        _blkexp_body,
        grid=(1,),
        in_specs=[pl.BlockSpec((1, LANES), lambda i: (0, 0))],
        out_specs=pl.BlockSpec((nbp, LANES), lambda i: (0, 0)),
        out_shape=jax.ShapeDtypeStruct((nbp, LANES), jnp.int32),
        name="moe_block_experts",
    )(cnt)
    return out[:n_blocks, 0]


def _for_each_row(n_rows, fn):
    def lane_row(i, carry):
        for j in range(LANES):
            fn(i * LANES + j, i, j)
        return carry

    lax.fori_loop(0, n_rows // LANES, lane_row, 0)


def _token_tile(ref, row, per):
    return ref.at[pl.ds(pl.multiple_of(row * per, per), per)]


def _scatter_body(per, d0_ref, d1_ref, cnt_ref, h_ref, xb_hbm, zbuf, sem, zsem):
    ts = h_ref.shape[0] // per
    n_rows = xb_hbm.shape[0] // per

    @pl.when(pl.program_id(0) == 0)
    def _():
        zbuf[...] = jnp.zeros_like(zbuf)

        zrows = zbuf.shape[0] // per

        def zero_copy(row, k):
            dst = xb_hbm.at[pl.ds(pl.multiple_of(row * per, per), k * per)]
            return pltpu.make_async_copy(zbuf.at[pl.ds(0, k * per)], dst, zsem)

        def zero_rows(lo, hi):
            n_full = lax.shift_right_logical(hi - lo, zrows.bit_length() - 1)
            rest_lo = lo + n_full * zrows
            rest = hi - rest_lo

            def each_copy(fn):
                def full(i, carry):
                    fn(zero_copy(lo + i * zrows, zrows))
                    return carry

                lax.fori_loop(0, n_full, full, 0)
                k = zrows // 2
                while k >= 1:
                    above = lax.bitwise_and(rest, -2 * k)

                    @pl.when(lax.bitwise_and(rest, k) != 0)
                    def _(k=k, above=above):
                        fn(zero_copy(rest_lo + above, k))
                    k //= 2

            each_copy(lambda cp: cp.start())
            each_copy(lambda cp: cp.wait())

        def expert(e, slab_start):
            used = cnt_ref[0, e]
            padded = ((used + (MOE_BLOCK - 1)) // MOE_BLOCK) * MOE_BLOCK
            zero_rows(slab_start + used, slab_start + padded)
            return slab_start + padded

        zero_rows(lax.fori_loop(0, MOE_EXPERTS, expert, 0), n_rows)

    def copies(t, i, j):
        src = _token_tile(h_ref, t, per)
        return (pltpu.make_async_copy(src, _token_tile(xb_hbm, d0_ref[i, j], per), sem),
                pltpu.make_async_copy(src, _token_tile(xb_hbm, d1_ref[i, j], per), sem))

    def start(t, i, j):
        for prio, cp in enumerate(copies(t, i, j)):
            cp.start(priority=prio)

    def wait(t, i, j):
        for cp in copies(t, i, j):
            cp.wait()

    _for_each_row(ts, start)
    _for_each_row(ts, wait)


def _scatter_rows(h2, dest0, dest1, cnt, n, n_rows, ts):
    per = h2.shape[0] // n
    smem = pl.BlockSpec((ts // LANES, LANES), lambda i: (i, 0), memory_space=pltpu.SMEM)
    return pl.pallas_call(
        functools.partial(_scatter_body, per),
        grid=(n // ts,),
        in_specs=[smem, smem, pl.BlockSpec((1, LANES), lambda i: (0, 0), memory_space=pltpu.SMEM),
                  pl.BlockSpec((ts * per, LANES), lambda i: (i, 0))],
        out_specs=pl.BlockSpec(memory_space=pl.ANY),
        out_shape=jax.ShapeDtypeStruct((n_rows * per, LANES), F32),
        scratch_shapes=[pltpu.VMEM((ZERO_ROWS * per, LANES), F32), pltpu.SemaphoreType.DMA(()),
                        pltpu.SemaphoreType.DMA(())],
        compiler_params=_cparams("arbitrary"),
        name="moe_scatter",
    )(dest0, dest1, cnt, h2)


def _expert_body(be_ref, x_ref, wg_ref, wu_ref, wd_ref, y_ref):
    del be_ref
    x = _load_token_tiles(x_ref, MOE_BLOCK)
    hid = _silu(_bdot(x, wg_ref[0, 0])) * _bdot(x, wu_ref[0, 0])
    _store_token_tiles(y_ref, _bdot(hid, wd_ref[0, 0]))


def _expert_mlp(xb, blk_expert, w_gate, w_up, w_down, layer):
    de, d = w_down.shape[2:]
    per = d // LANES
    nb = xb.shape[0] // (MOE_BLOCK * per)
    tiles = pl.BlockSpec((MOE_BLOCK * per, LANES), lambda i, be: (i, 0))
    grid_spec = pltpu.PrefetchScalarGridSpec(
        num_scalar_prefetch=1,
        grid=(nb,),
        in_specs=[tiles,
                  pl.BlockSpec((1, 1, d, de), lambda i, be: (layer, be[i], 0, 0)),
                  pl.BlockSpec((1, 1, d, de), lambda i, be: (layer, be[i], 0, 0)),
                  pl.BlockSpec((1, 1, de, d), lambda i, be: (layer, be[i], 0, 0))],
        out_specs=tiles,
    )
    return pl.pallas_call(
        _expert_body,
        grid_spec=grid_spec,
        out_shape=jax.ShapeDtypeStruct(xb.shape, F32),
        compiler_params=_cparams("arbitrary"),
        name="moe_experts",
    )(blk_expert, xb, w_gate, w_up, w_down)


def _combine_body(final_norm, d0_ref, d1_ref, d0_next, d1_next, rf_ref, x_ref, g_ref, yb_hbm, o_ref, gbuf, sems):
    ts, d = x_ref.shape
    per = d // LANES
    step = pl.program_id(0)
    slot = step % 2

    def gather(dest_refs, buf_slot):
        def copies(t, i, j):
            return [pltpu.make_async_copy(_token_tile(yb_hbm, dref[i, j], per),
                                          _token_tile(gbuf.at[buf_slot, k], t, per), sems.at[buf_slot])
                    for k, dref in enumerate(dest_refs)]
        return copies

    def start_all(copies):
        def start(t, i, j):
            for prio, cp in enumerate(copies(t, i, j)):
                cp.start(priority=prio)
        _for_each_row(ts, start)

    @pl.when(step == 0)
    def _():
        start_all(gather((d0_ref, d1_ref), slot))

    @pl.when(step + 1 < pl.num_programs(0))
    def _():
        start_all(gather((d0_next, d1_next), 1 - slot))

    current = gather((d0_ref, d1_ref), slot)

    def wait(t, i, j):
        for cp in current(t, i, j):
            cp.wait()

    _for_each_row(ts, wait)
    rf = rf_ref[...]
    x = x_ref[...] + (_load_token_tiles(gbuf.at[slot, 0], ts) * rf[:, 0:1]
                      + _load_token_tiles(gbuf.at[slot, 1], ts) * rf[:, 1:2])
    if final_norm:
        x = x * lax.rsqrt(jnp.mean(x * x, axis=-1, keepdims=True) + NORM_EPS) * g_ref[...]
    o_ref[...] = x


def _combine(x_mid, yb, dest0, dest1, rf, final_gain, ts):
    n, d = x_mid.shape
    final_norm = final_gain is not None
    gain = (final_gain if final_norm else jnp.ones((d,), F32)).reshape(1, d)
    last = n // ts - 1
    smem = pl.BlockSpec((ts // LANES, LANES), lambda i: (i, 0), memory_space=pltpu.SMEM)
    smem_next = pl.BlockSpec((ts // LANES, LANES), lambda i: (jnp.minimum(i + 1, last), 0), memory_space=pltpu.SMEM)
    rows = lambda w: pl.BlockSpec((ts, w), lambda i: (i, 0))
    return pl.pallas_call(
        functools.partial(_combine_body, final_norm),
        grid=(n // ts,),
        in_specs=[smem, smem, smem_next, smem_next, rows(LANES), rows(d), pl.BlockSpec((1, d), lambda i: (0, 0)),
                  pl.BlockSpec(memory_space=pl.ANY)],
        out_specs=rows(d),
        out_shape=jax.ShapeDtypeStruct((n, d), F32),
        scratch_shapes=[pltpu.VMEM((2, 2, ts * d // LANES, LANES), F32), pltpu.SemaphoreType.DMA((2,))],
        compiler_params=_cparams("arbitrary"),
        name="moe_combine",
    )(dest0, dest1, dest0, dest1, rf, x_mid, gain, yb)


def _moe(x_mid, h2, ri, rf, rank, cnt, w_gate, w_up, w_down, layer, final_gain):
    n, d = x_mid.shape
    n_assign = 2 * n
    n_blocks = (n_assign + MOE_EXPERTS * (MOE_BLOCK - 1) + MOE_BLOCK - 1) // MOE_BLOCK
    ts = min(1024, n)
    dest0, dest1 = _destinations(ri, rank, cnt)
    blk_expert = _block_experts(cnt, n_blocks)
    xb = _scatter_rows(h2, dest0, dest1, cnt, n, n_blocks * MOE_BLOCK, ts)
    yb = _expert_mlp(xb, blk_expert, w_gate, w_up, w_down, layer)
    return _combine(x_mid, yb, dest0, dest1, rf, final_gain, ts)


def kernel(x, norm_mix, norm_ffn, norm_final, w_in, w_out, rwkv_mu, rwkv_w0, rwkv_w2, rwkv_a0, rwkv_a2, rwkv_g2, rwkv_k_k, rwkv_k_a, rwkv_r_k, rwkv_lnx_w, rwkv_lnx_b, gdn_conv_w, gdn_a_log, gdn_dt_bias, gdn_norm_w, lru_conv_w, lru_conv_b, lru_w_a, lru_b_a, lru_w_x, lru_b_x, lru_lambda, moe_w_group, moe_b_group, moe_w_expert, moe_b_expert, moe_w_gate, moe_w_up, moe_w_down):
    bsz, seq, d = x.shape
    depth = w_in.shape[0]
    n = bsz * seq
    a_cols = rwkv_mu.shape[1]
    db = gdn_norm_w.shape[1] * GDN_HEADS
    dc = lru_conv_b.shape[1]
    n_ba = 2 * GDN_HEADS
    widths = (a_cols, 3 * db, db, 2 * dc, LANES)
    xt = x.reshape(n, d)
    for l in range(depth):
        w = w_in[l]
        b0 = a_cols
        c0 = b0 + 4 * db + n_ba
        w_cat = jnp.concatenate(
            [w[:, 0:a_cols], w[:, b0:b0 + 4 * db], w[:, c0:c0 + 2 * dc],
             w[:, b0 + 4 * db:c0], jnp.zeros((d, LANES - n_ba), F32)], axis=1).astype(BF16)
        p_a, p_qkv, p_z, p_c, p_ba = _in_proj(xt, seq, norm_mix[l], w_cat, widths, rwkv_mu[l], gdn_conv_w[l],
                                              lru_conv_w[l], lru_conv_b[l])
        o_a = _rwkv_group(p_a, bsz, seq, rwkv_w0[l], rwkv_w2[l], rwkv_a0[l], rwkv_a2[l], rwkv_g2[l],
                          rwkv_k_k[l], rwkv_k_a[l], rwkv_r_k[l], rwkv_lnx_w[l], rwkv_lnx_b[l])
        o_b = _gdn_group(p_qkv, p_z, p_ba, bsz, seq, gdn_a_log[l], gdn_dt_bias[l], gdn_norm_w[l])
        o_c = _lru_group(p_c, bsz, seq, lru_w_a[l], lru_b_a[l], lru_w_x[l], lru_b_x[l], lru_lambda[l])
        x_mid, h2, ri, rf, rank, cnt = _out_proj_router(xt, o_a, o_b, o_c, w_out[l], norm_ffn[l], moe_w_group[l],
                                                        moe_b_group[l], moe_w_expert[l], moe_b_expert[l])
        xt = _moe(x_mid, h2, ri, rf, rank, cnt, moe_w_gate, moe_w_up, moe_w_down, l,
                  norm_final if l == depth - 1 else None)
    return xt.reshape(bsz, seq, d)
```
